```python
import math
import jax, jax.numpy as jnp
from jax import lax
import numpy as np


D_MODEL = 2048
BATCH = 2
SEQ = 16384
DEPTH = 1

MEM_TOKENS = 256
GLA_HEADS = 4
GLA_DK = D_MODEL // 8
GLA_DV = D_MODEL // 4
GLA_GATE_RANK = 16
GLA_GATE_TEMP = 16.0
GLA_CHUNK = 64
DSA_HEADS = 8
DSA_HEAD_DIM = D_MODEL // 16
IDX_HEADS = 16
IDX_DIM = D_MODEL // 32
DSA_TOPK_MAX = 256
Q_BLOCK = 128
MEM_HEADS = 4
MEM_HEAD_DIM = D_MODEL // 8
D_FF = 4 * D_MODEL
N_BRANCH = 3
LN_EPS = 1e-5
RMS_EPS = 1e-6
DEEPNORM_ALPHA = (2 * DEPTH) ** 0.25
DEEPNORM_BETA = (8 * DEPTH) ** -0.25

PARTITION = (
    ('gla_q', GLA_HEADS * GLA_DK),
    ('gla_k', GLA_HEADS * GLA_DK),
    ('gla_v', GLA_HEADS * GLA_DV),
    ('gla_r', GLA_HEADS * GLA_DV),
    ('gla_gate_lr', GLA_GATE_RANK),
    ('dsa_q', DSA_HEADS * DSA_HEAD_DIM),
    ('dsa_k', DSA_HEADS * DSA_HEAD_DIM),
    ('dsa_v', DSA_HEADS * DSA_HEAD_DIM),
    ('idx_q', IDX_HEADS * IDX_DIM),
    ('idx_k', IDX_DIM),
    ('idx_w', IDX_HEADS),
    ('mem_q', MEM_HEADS * MEM_HEAD_DIM),
    ('gate', N_BRANCH * D_MODEL),
)
IN_WIDTH = sum(w for _, w in PARTITION)

kernel_name = 'hybrid_gla_dsa_memory_deepnorm'


def _layer_norm(x, g, b):
    xf = x.astype(jnp.float32)
    mu = jnp.mean(xf, axis=-1, keepdims=True)
    var = jnp.mean(jnp.square(xf - mu), axis=-1, keepdims=True)
    return ((xf - mu) * lax.rsqrt(var + LN_EPS) * g + b).astype(x.dtype)


def _split_columns(x, w_in):
    parts = {}
    lo = 0
    for name, width in PARTITION:
        parts[name] = x @ w_in[:, lo:lo + width]
        lo += width
    return parts


def _gla_branch(q, k, v, r, gate_lr, w_gate_up, b_gate, norm_g):
    B, T, _ = q.shape
    H, dk, dv, C = GLA_HEADS, GLA_DK, GLA_DV, GLA_CHUNK
    N = T // C
    f32 = jnp.float32

    def chunks(a, d):
        return a.astype(f32).reshape(B, N, C, H, d).transpose(0, 1, 3, 2, 4)

    log_a = jax.nn.log_sigmoid((gate_lr @ w_gate_up + b_gate).astype(f32)) / GLA_GATE_TEMP
    qc = chunks(q, dk) * (dk ** -0.5)
    kc = chunks(k, dk)
    vc = chunks(v, dv)
    b = jnp.cumsum(chunks(log_a, dk), axis=3)
    b_last = b[:, :, :, -1:, :]
    b_ref = b[:, :, :, C // 2:C // 2 + 1, :]

    causal = jnp.tril(jnp.ones((C, C), dtype=bool))
    scores = jnp.einsum('bnhcd,bnhsd->bnhcs', qc * jnp.exp(b - b_ref), kc * jnp.exp(b_ref - b))
    scores = jnp.where(causal, scores, 0.0)
    o_intra = jnp.einsum('bnhcs,bnhse->bnhce', scores, vc)

    q_inter = qc * jnp.exp(b)
    k_state = kc * jnp.exp(b_last - b)
    decay = jnp.exp(b_last[:, :, :, 0, :])

    def step(S, xs):
        qn, kn, vn, dn = xs
        o = jnp.einsum('bhcd,bhde->bhce', qn, S)
        S = S * dn[..., None] + jnp.einsum('bhcd,bhce->bhde', kn, vn)
        return S, o

    xs = tuple(jnp.moveaxis(a, 1, 0) for a in (q_inter, k_state, vc, decay))
    S0 = jnp.zeros((B, H, dk, dv), f32)
    _, o_inter = lax.scan(step, S0, xs)
    o = o_intra + jnp.moveaxis(o_inter, 0, 1)
    o = o.transpose(0, 1, 3, 2, 4).reshape(B, T, H, dv)
    o = o * lax.rsqrt(jnp.mean(jnp.square(o), axis=-1, keepdims=True) + RMS_EPS)
    o = o.reshape(B, T, H * dv) * norm_g * jax.nn.silu(r.astype(f32))
    return o.astype(r.dtype)


def _dsa_branch(q, k, v, idx_q, idx_k, idx_w):
    B, T, _ = q.shape
    H, d = DSA_HEADS, DSA_HEAD_DIM
    topk = min(DSA_TOPK_MAX, T // 4)
    nblk = T // Q_BLOCK
    f32 = jnp.float32
    q = q.reshape(B, T, H, d)
    k = k.reshape(B, T, H, d)
    v = v.reshape(B, T, H, d)
    iq = idx_q.reshape(B, T, IDX_HEADS, IDX_DIM)
    slopes = 2.0 ** (-8.0 * jnp.arange(1, H + 1, dtype=f32) / H)
    key_pos = jnp.arange(T, dtype=jnp.int32)

    def blocks(a):
        return jnp.moveaxis(a.reshape(B, nblk, Q_BLOCK, *a.shape[2:]), 1, 0)

    gather = jax.vmap(lambda src, ids: src[ids])

    def one_block(args):
        qb, iqb, wb, tq = args
        rel = jax.nn.relu(jnp.einsum('bqgd,bsd->bqgs', iqb, idx_k).astype(f32) * (IDX_DIM ** -0.5))
        score = jnp.einsum('bqg,bqgs->bqs', wb.astype(f32) * (IDX_HEADS ** -0.5), rel)
        causal = key_pos[None, :] <= tq[:, None]
        score = jnp.where(causal[None], score, -jnp.inf)
        _, sel = lax.top_k(score, topk)
        k_sel = gather(k, sel)
        v_sel = gather(v, sel)
        logits = jnp.einsum('bqhd,bqkhd->bqhk', qb, k_sel).astype(f32) * (d ** -0.5)
        dist = (tq[None, :, None] - sel).astype(f32)
        logits = logits - slopes[None, None, :, None] * dist[:, :, None, :]
        valid = sel <= tq[None, :, None]
        logits = jnp.where(valid[:, :, None, :], logits, -jnp.inf)
        p = jax.nn.softmax(logits, axis=-1)
        return jnp.einsum('bqhk,bqkhd->bqhd', p.astype(v.dtype), v_sel)

    xs = (blocks(q), blocks(iq), blocks(idx_w), key_pos.reshape(nblk, Q_BLOCK))
    out = lax.map(one_block, xs)
    return jnp.moveaxis(out, 0, 1).reshape(B, T, H * d)


def _memory_branch(q, mem, w_mem_kv):
    B, T, _ = q.shape
    M = mem.shape[1]
    Hm, dm = MEM_HEADS, MEM_HEAD_DIM
    kv = mem @ w_mem_kv
    km = kv[..., :Hm * dm].reshape(B, M, Hm, dm)
    vm = kv[..., Hm * dm:].reshape(B, M, Hm, dm)
    qm = q.reshape(B, T, Hm, dm)
    logits = jnp.einsum('bthd,bmhd->bhtm', qm, km).astype(jnp.float32) * (dm ** -0.5)
    p = jax.nn.softmax(logits, axis=-1)
    out = jnp.einsum('bhtm,bmhd->bthd', p.astype(vm.dtype), vm)
    return out.reshape(B, T, Hm * dm)


def setup_inputs(seed: int = 0) -> dict:
    key = jax.random.key(seed)
    ks = jax.random.split(key, 22)
    f32 = jnp.float32
    L = DEPTH

    def nrm(k, shape, scale):
        return jax.random.normal(k, shape, f32) * scale

    gla_v = GLA_HEADS * GLA_DV
    dsa_w = DSA_HEADS * DSA_HEAD_DIM
    mem_w = MEM_HEADS * MEM_HEAD_DIM
    return {
        'x': nrm(ks[0], (BATCH, SEQ, D_MODEL), 1.0),
        'mem': nrm(ks[1], (BATCH, MEM_TOKENS, D_MODEL), 1.0),
        'w_in': nrm(ks[2], (L, D_MODEL, IN_WIDTH), D_MODEL ** -0.5),
        'w_gla_gate_up': nrm(ks[3], (L, GLA_GATE_RANK, GLA_HEADS * GLA_DK), GLA_GATE_RANK ** -0.5),
        'b_gla_gate': nrm(ks[4], (L, GLA_HEADS * GLA_DK), 0.01),
        'gla_norm_g': 1.0 + nrm(ks[5], (L, gla_v), 0.01),
        'w_mem_kv': nrm(ks[6], (L, D_MODEL, 2 * mem_w), D_MODEL ** -0.5),
        'w_br_gla': nrm(ks[7], (L, gla_v, D_MODEL), gla_v ** -0.5),
        'w_br_dsa': nrm(ks[8], (L, dsa_w, D_MODEL), dsa_w ** -0.5),
        'w_br_mem': nrm(ks[9], (L, mem_w, D_MODEL), mem_w ** -0.5),
        'b_merge': nrm(ks[10], (L, N_BRANCH * D_MODEL), 0.01),
        'w_o': nrm(ks[11], (L, D_MODEL, D_MODEL), DEEPNORM_BETA * D_MODEL ** -0.5),
        'ln1_g': 1.0 + nrm(ks[12], (L, D_MODEL), 0.01),
        'ln1_b': nrm(ks[13], (L, D_MODEL), 0.01),
        'w_up': nrm(ks[14], (L, D_MODEL, D_FF), D_MODEL ** -0.5),
        'b_up': nrm(ks[15], (L, D_FF), 0.01),
        'w_down': nrm(ks[16], (L, D_FF, D_MODEL), DEEPNORM_BETA * D_FF ** -0.5),
        'b_down': nrm(ks[17], (L, D_MODEL), 0.01),
        'ln2_g': 1.0 + nrm(ks[18], (L, D_MODEL), 0.01),
        'ln2_b': nrm(ks[19], (L, D_MODEL), 0.01),
    }


def reference(x, mem, w_in, w_gla_gate_up, b_gla_gate, gla_norm_g, w_mem_kv, w_br_gla, w_br_dsa,
              w_br_mem, b_merge, w_o, ln1_g, ln1_b, w_up, b_up, w_down, b_down, ln2_g, ln2_b):
    B, T, D = x.shape
    for l in range(DEPTH):
        p = _split_columns(x, w_in[l])
        o_gla = _gla_branch(p['gla_q'], p['gla_k'], p['gla_v'], p['gla_r'], p['gla_gate_lr'],
                            w_gla_gate_up[l], b_gla_gate[l], gla_norm_g[l])
        o_dsa = _dsa_branch(p['dsa_q'], p['dsa_k'], p['dsa_v'], p['idx_q'], p['idx_k'], p['idx_w'])
        o_mem = _memory_branch(p['mem_q'], mem, w_mem_kv[l])
        gates = jax.nn.sigmoid(p['gate'] + b_merge[l]).reshape(B, T, N_BRANCH, D)
        merged = (gates[:, :, 0, :] * (o_gla @ w_br_gla[l])
                  + gates[:, :, 1, :] * (o_dsa @ w_br_dsa[l])
                  + gates[:, :, 2, :] * (o_mem @ w_br_mem[l]))
        y = merged @ w_o[l]
        x = _layer_norm(DEEPNORM_ALPHA * x + y, ln1_g[l], ln1_b[l])
        h = jnp.square(jax.nn.relu(x @ w_up[l] + b_up[l])) @ w_down[l] + b_down[l]
        x = _layer_norm(DEEPNORM_ALPHA * x + h, ln2_g[l], ln2_b[l])
    return x
```

```python
import functools

import numpy as np
import jax
import jax.numpy as jnp
from jax import lax
from jax.experimental import pallas as pl
from jax.experimental.pallas import tpu as pltpu

F32 = jnp.float32
BF16 = jnp.bfloat16

GLA_HEADS = 4
GLA_GATE_RANK = 16
GLA_GATE_TEMP = 16.0
GLA_CHUNK = 64
DSA_HEADS = 8
IDX_HEADS = 16
DSA_TOPK_MAX = 256
MEM_HEADS = 4
N_BRANCH = 3
LN_EPS = 1e-5
RMS_EPS = 1e-6

V7X_VMEM_LIMIT_BYTES = 56 * 1024 * 1024
LANES = 128
MASK_NEG = -1e30
INT_MIN = -(2 ** 31)


def _cparams(sem):
    return pltpu.CompilerParams(dimension_semantics=sem, vmem_limit_bytes=V7X_VMEM_LIMIT_BYTES)


def _mm_cast_kernel(x_ref, w_ref, o_ref, xb_ref):
    @pl.when(pl.program_id(1) == 0)
    def _():
        xb_ref[...] = x_ref[...].astype(BF16)

    o_ref[...] = jnp.dot(xb_ref[...], w_ref[...], preferred_element_type=F32).astype(o_ref.dtype)


def _matmul(x, w, out_dtype, tm, tn, name):
    M, K = x.shape
    N = w.shape[1]
    tm, tn = min(tm, M), min(tn, N)
    assert M % tm == 0 and N % tn == 0
    return pl.pallas_call(
        _mm_cast_kernel,
        out_shape=jax.ShapeDtypeStruct((M, N), out_dtype),
        grid=(M // tm, N // tn),
        in_specs=[pl.BlockSpec((tm, K), lambda i, j: (i, 0)),
                  pl.BlockSpec((K, tn), lambda i, j: (0, j))],
        out_specs=pl.BlockSpec((tm, tn), lambda i, j: (i, j)),
        scratch_shapes=[pltpu.VMEM((tm, K), BF16)],
        compiler_params=_cparams(("parallel", "arbitrary")),
        name=name,
    )(x, w)


def _log_sigmoid(z):
    return jnp.minimum(z, 0.0) - jnp.log(1.0 + jnp.exp(-jnp.abs(z)))


def _gla_kernel(q_ref, k_ref, v_ref, r_ref, sm_ref, wg_ref, bg_ref, ng_ref, o_ref, st_ref, *, chunk, rank):
    @pl.when(pl.program_id(2) == 0)
    def _():
        st_ref[...] = jnp.zeros_like(st_ref)

    tb, dk = q_ref.shape
    C = chunk
    glr = sm_ref[:, 0:rank]
    z = jnp.dot(glr, wg_ref[...], preferred_element_type=F32) + bg_ref[...]
    la = _log_sigmoid(z) * (1.0 / GLA_GATE_TEMP)
    ri = lax.broadcasted_iota(jnp.int32, (tb, tb), 0)
    ci = lax.broadcasted_iota(jnp.int32, (tb, tb), 1)
    tri = jnp.where((ci <= ri) & ((ri // C) == (ci // C)), 1.0, 0.0).astype(F32)
    b = jnp.dot(tri, la, preferred_element_type=F32, precision=lax.Precision.HIGHEST)
    causal = (lax.broadcasted_iota(jnp.int32, (C, C), 1) <= lax.broadcasted_iota(jnp.int32, (C, C), 0))

    for c in range(tb // C):
        rows = slice(c * C, (c + 1) * C)
        bc = b[rows]
        br = bc[C // 2:C // 2 + 1]
        bl = bc[C - 1:C]
        qc = q_ref[rows, :].astype(F32)
        kc = k_ref[rows, :].astype(F32)
        vc = v_ref[rows, :]
        q_in = (qc * jnp.exp(bc - br)).astype(BF16)
        k_in = (kc * jnp.exp(br - bc)).astype(BF16)
        sc = lax.dot_general(q_in, k_in, (((1,), (1,)), ((), ())), preferred_element_type=F32)
        sc = jnp.where(causal, sc, 0.0)
        o = jnp.dot(sc.astype(BF16), vc, preferred_element_type=F32)
        q_it = (qc * jnp.exp(bc)).astype(BF16)
        st = st_ref[...]
        o = o + lax.dot_general(q_it, st.astype(BF16), (((1,), (1,)), ((), ())), preferred_element_type=F32)
        k_st = (kc * jnp.exp(bl - bc)).astype(BF16)
        upd = lax.dot_general(vc, k_st, (((0,), (0,)), ((), ())), preferred_element_type=F32)
        st_ref[...] = st * jnp.exp(bl) + upd
        o = o * lax.rsqrt(jnp.mean(o * o, axis=-1, keepdims=True) + RMS_EPS)
        r = r_ref[rows, :].astype(F32)
        o = o * ng_ref[...] * (r / (1.0 + jnp.exp(-r)))
        o_ref[rows, :] = o.astype(o_ref.dtype)


def _gla(p_big, small, w_gate_up, b_gate, norm_g, B, T, dk, dv, tb):
    M = B * T
    H = GLA_HEADS
    nb = T // tb
    row = lambda b, h, n: b * nb + n
    kern = functools.partial(_gla_kernel, chunk=GLA_CHUNK, rank=GLA_GATE_RANK)
    koff = (H * dk) // dk
    voff = (2 * H * dk) // dv
    roff = (2 * H * dk + H * dv) // dv
    return pl.pallas_call(
        kern,
        out_shape=jax.ShapeDtypeStruct((M, H * dv), BF16),
        grid=(B, H, nb),
        in_specs=[
            pl.BlockSpec((tb, dk), lambda b, h, n: (row(b, h, n), h)),
            pl.BlockSpec((tb, dk), lambda b, h, n: (row(b, h, n), koff + h)),
            pl.BlockSpec((tb, dv), lambda b, h, n: (row(b, h, n), voff + h)),
            pl.BlockSpec((tb, dv), lambda b, h, n: (row(b, h, n), roff + h)),
            pl.BlockSpec((tb, LANES), lambda b, h, n: (row(b, h, n), 0)),
            pl.BlockSpec((GLA_GATE_RANK, dk), lambda b, h, n: (0, h)),
            pl.BlockSpec((1, dk), lambda b, h, n: (0, h)),
            pl.BlockSpec((1, dv), lambda b, h, n: (0, h)),
        ],
        out_specs=pl.BlockSpec((tb, dv), lambda b, h, n: (row(b, h, n), h)),
        scratch_shapes=[pltpu.VMEM((dv, dk), F32)],
        compiler_params=_cparams(("parallel", "parallel", "arbitrary")),
        name="gla",
    )(p_big, p_big, p_big, p_big, small, w_gate_up, b_gate, norm_g)


def _idx_kernel(iq_ref, sm_ref, ikt_ref, bias_ref, keys_ref, iqs_ref, wb_ref, *, topk, w_off, idim):
    i = pl.program_id(1)
    nT, tq, tk = keys_ref.shape
    G = iqs_ref.shape[0]
    n_lane = tk // LANES
    ntile = ((i + 1) * tq + tk - 1) // tk

    w = sm_ref[:, w_off:w_off + G] * ((G ** -0.5) * (idim ** -0.5))
    for g in range(G):
        iqs_ref[g] = iq_ref[:, g * idim:(g + 1) * idim]
        wb_ref[g] = jnp.broadcast_to(w[:, g:g + 1], (tq, LANES))
    row_pos = i * tq + lax.broadcasted_iota(jnp.int32, (tq, 1), 0)

    def score_tile(j, carry):
        ik = ikt_ref[j]
        acc = jnp.zeros((tq, tk), F32)
        for g in range(G):
            a = jnp.dot(iqs_ref[g], ik, preferred_element_type=F32)
            wg = jnp.concatenate([wb_ref[g]] * n_lane, axis=1)
            acc = acc + wg * jnp.maximum(a, 0.0)
        bits = pltpu.bitcast(acc, jnp.int32)
        key = bits ^ ((bits >> 31) & 0x7FFFFFFF)
        col_pos = j * tk + lax.broadcasted_iota(jnp.int32, (1, tk), 1)
        keys_ref[j] = jnp.where(col_pos <= row_pos, key, INT_MIN)
        return carry

    lax.fori_loop(0, ntile, score_tile, 0)

    def bit_pass(t, ans):
        cand = ans + lax.shift_left(jnp.int32(1), 31 - t)
        cand_b = jnp.broadcast_to(cand, (tq, LANES))

        def count_tile(j, cnt):
            kt = keys_ref[j]
            for c in range(n_lane):
                cnt = cnt + jnp.where(kt[:, c * LANES:(c + 1) * LANES] >= cand_b, 1.0, 0.0)
            return cnt

        cnt = lax.fori_loop(0, ntile, count_tile, jnp.zeros((tq, LANES), F32))
        tot = jnp.sum(cnt, axis=1, keepdims=True)
        return jnp.where(tot >= float(topk), cand, ans)

    ans = lax.fori_loop(0, 32, bit_pass, jnp.full((tq, 1), INT_MIN, jnp.int32))
    thr_b = jnp.broadcast_to(jnp.maximum(ans, INT_MIN + 1), (tq, LANES))

    def write_tile(j, carry):
        kt = keys_ref[j]
        parts = [jnp.where(kt[:, c * LANES:(c + 1) * LANES] >= thr_b, 0.0, MASK_NEG) for c in range(n_lane)]
        bias_ref[j] = jnp.concatenate(parts, axis=1).astype(bias_ref.dtype)
        return carry

    lax.fori_loop(0, ntile, write_tile, 0)

    def fill_tile(j, carry):
        bias_ref[j] = jnp.full((tq, tk), MASK_NEG, bias_ref.dtype)
        return carry

    lax.fori_loop(ntile, nT, fill_tile, 0)


def _dsa_mask(p_big, small, ikt, B, T, tq, tk, iq_off, w_off, topk):
    G = IDX_HEADS
    idim = ikt.shape[2]
    nQ, nT = T // tq, T // tk
    kern = functools.partial(_idx_kernel, topk=topk, w_off=w_off, idim=idim)
    return pl.pallas_call(
        kern,
        out_shape=jax.ShapeDtypeStruct((B, nQ, nT, tq, tk), BF16),
        grid=(B, nQ),
        in_specs=[
            pl.BlockSpec((tq, G * idim), lambda b, i: (b * nQ + i, iq_off // (G * idim))),
            pl.BlockSpec((tq, LANES), lambda b, i: (b * nQ + i, 0)),
            pl.BlockSpec((None, nT, idim, tk), lambda b, i: (b, 0, 0, 0)),
        ],
        out_specs=pl.BlockSpec((None, None, nT, tq, tk), lambda b, i: (b, i, 0, 0, 0)),
        scratch_shapes=[pltpu.VMEM((nT, tq, tk), jnp.int32),
                        pltpu.VMEM((G, tq, idim), BF16),
                        pltpu.VMEM((G, tq, LANES), F32)],
        compiler_params=_cparams(("parallel", "arbitrary")),
        name="dsa_index_topk",
    )(p_big, small, ikt)


def _dsa_attn_kernel(qi_ref, kj_ref, q_ref, k_ref, v_ref, bias_ref, o_ref, m_ref, l_ref, acc_ref, *, heads):
    p = pl.program_id(1)
    qi, kj = qi_ref[p], kj_ref[p]
    tq, tk = bias_ref.shape
    d = q_ref.shape[1] // heads

    @pl.when(kj == 0)
    def _():
        m_ref[...] = jnp.full_like(m_ref, MASK_NEG)
        l_ref[...] = jnp.zeros_like(l_ref)
        acc_ref[...] = jnp.zeros_like(acc_ref)

    t_pos = qi * tq + lax.broadcasted_iota(jnp.int32, (tq, 1), 0)
    s_pos = kj * tk + lax.broadcasted_iota(jnp.int32, (1, tk), 1)
    dist = (t_pos - s_pos).astype(F32)
    bias = bias_ref[...].astype(F32)
    for h in range(heads):
        cols = slice(h * d, (h + 1) * d)
        slope = 2.0 ** (-8.0 * (h + 1) / heads)
        s = lax.dot_general(q_ref[:, cols], k_ref[:, cols], (((1,), (1,)), ((), ())), preferred_element_type=F32)
        s = s - slope * dist + bias
        m_old = m_ref[h]
        m_new = jnp.maximum(m_old, jnp.max(s, axis=1, keepdims=True))
        pr = jnp.exp(s - m_new)
        corr = jnp.exp(m_old - m_new)
        l_ref[h] = corr * l_ref[h] + jnp.sum(pr, axis=1, keepdims=True)
        acc_ref[:, cols] = corr * acc_ref[:, cols] + jnp.dot(pr.astype(BF16), v_ref[:, cols],
                                                             preferred_element_type=F32)
        m_ref[h] = m_new

    @pl.when((kj + 1) * tk >= (qi + 1) * tq)
    def _():
        for h in range(heads):
            cols = slice(h * d, (h + 1) * d)
            o_ref[:, cols] = (acc_ref[:, cols] / l_ref[h]).astype(o_ref.dtype)


def _dsa_attn(p_big, bias, B, T, tq, tk, q_off, width):
    nQ, nT = T // tq, T // tk
    pairs = [(i, j) for i in range(nQ) for j in range(((i + 1) * tq + tk - 1) // tk)]
    qi_tbl = jnp.asarray(np.array([p[0] for p in pairs], np.int32))
    kj_tbl = jnp.asarray(np.array([p[1] for p in pairs], np.int32))
    qb = q_off // width
    kern = functools.partial(_dsa_attn_kernel, heads=DSA_HEADS)
    grid_spec = pltpu.PrefetchScalarGridSpec(
        num_scalar_prefetch=2,
        grid=(B, len(pairs)),
        in_specs=[
            pl.BlockSpec((tq, width), lambda b, p, qi, kj: (b * nQ + qi[p], qb)),
            pl.BlockSpec((tk, width), lambda b, p, qi, kj: (b * nT + kj[p], qb + 1)),
            pl.BlockSpec((tk, width), lambda b, p, qi, kj: (b * nT + kj[p], qb + 2)),
            pl.BlockSpec((None, None, None, tq, tk), lambda b, p, qi, kj: (b, qi[p], kj[p], 0, 0)),
        ],
        out_specs=pl.BlockSpec((tq, width), lambda b, p, qi, kj: (b * nQ + qi[p], 0)),
        scratch_shapes=[pltpu.VMEM((DSA_HEADS, tq, 1), F32),
                        pltpu.VMEM((DSA_HEADS, tq, 1), F32),
                        pltpu.VMEM((tq, width), F32)],
    )
    return pl.pallas_call(
        kern,
        out_shape=jax.ShapeDtypeStruct((B * T, width), BF16),
        grid_spec=grid_spec,
        compiler_params=_cparams(("arbitrary", "arbitrary")),
        name="dsa_attention",
    )(qi_tbl, kj_tbl, p_big, p_big, p_big, bias)


def _mem_attn_kernel(q_ref, k_ref, v_ref, o_ref):
    s = lax.dot_general(q_ref[...], k_ref[...], (((1,), (1,)), ((), ())), preferred_element_type=F32)
    m = jnp.max(s, axis=1, keepdims=True)
    p = jnp.exp(s - m)
    l = jnp.sum(p, axis=1, keepdims=True)
    o = jnp.dot(p.astype(BF16), v_ref[...], preferred_element_type=F32)
    o_ref[...] = (o / l).astype(o_ref.dtype)


def _mem_attn(p_big, kv, B, T, Mt, dm, q_off, tq):
    Hm = MEM_HEADS
    nq = T // tq
    qb = q_off // dm
    return pl.pallas_call(
        _mem_attn_kernel,
        out_shape=jax.ShapeDtypeStruct((B * T, Hm * dm), BF16),
        grid=(B, nq, Hm),
        in_specs=[
            pl.BlockSpec((tq, dm), lambda b, i, h: (b * nq + i, qb + h)),
            pl.BlockSpec((Mt, dm), lambda b, i, h: (b, h)),
            pl.BlockSpec((Mt, dm), lambda b, i, h: (b, Hm + h)),
        ],
        out_specs=pl.BlockSpec((tq, dm), lambda b, i, h: (b * nq + i, h)),
        compiler_params=_cparams(("parallel", "parallel", "arbitrary")),
        name="memory_attention",
    )(p_big, kv, kv)


def _merge_kernel(a_ref, d_ref, m_ref, wa_ref, wd_ref, wm_ref, g0_ref, g1_ref, g2_ref,
                  b0_ref, b1_ref, b2_ref, o_ref):
    def gate(g_ref, b_ref):
        return jax.nn.sigmoid(g_ref[...].astype(F32) + b_ref[...])

    acc = gate(g0_ref, b0_ref) * jnp.dot(a_ref[...], wa_ref[...], preferred_element_type=F32)
    acc = acc + gate(g1_ref, b1_ref) * jnp.dot(d_ref[...], wd_ref[...], preferred_element_type=F32)
    acc = acc + gate(g2_ref, b2_ref) * jnp.dot(m_ref[...], wm_ref[...], preferred_element_type=F32)
    o_ref[...] = acc.astype(o_ref.dtype)


def _merge(o_gla, o_dsa, o_mem, wa, wd, wm, p_big, b_merge, D, gate_off, tm, tn):
    M = o_gla.shape[0]
    gb = gate_off // tn
    nd = D // tn
    row = lambda w: pl.BlockSpec((tm, w), lambda i, j: (i, 0))
    col = lambda k: pl.BlockSpec((k, tn), lambda i, j: (0, j))
    gspec = lambda r: pl.BlockSpec((tm, tn), lambda i, j: (i, gb + r * nd + j))
    bspec = lambda r: pl.BlockSpec((1, tn), lambda i, j: (0, r * nd + j))
    return pl.pallas_call(
        _merge_kernel,
        out_shape=jax.ShapeDtypeStruct((M, D), BF16),
        grid=(M // tm, D // tn),
        in_specs=[row(o_gla.shape[1]), row(o_dsa.shape[1]), row(o_mem.shape[1]),
                  col(wa.shape[0]), col(wd.shape[0]), col(wm.shape[0]),
                  gspec(0), gspec(1), gspec(2), bspec(0), bspec(1), bspec(2)],
        out_specs=pl.BlockSpec((tm, tn), lambda i, j: (i, j)),
        compiler_params=_cparams(("parallel", "arbitrary")),
        name="gated_merge",
    )(o_gla, o_dsa, o_mem, wa, wd, wm, p_big, p_big, p_big, b_merge, b_merge, b_merge)


def _layer_norm(y, g, b):
    mu = jnp.mean(y, axis=-1, keepdims=True)
    yc = y - mu
    var = jnp.mean(yc * yc, axis=-1, keepdims=True)
    return yc * lax.rsqrt(var + LN_EPS) * g + b


def _proj_ln_kernel(mg_ref, wo_ref, x_ref, g_ref, b_ref, o_ref, *, alpha):
    y = jnp.dot(mg_ref[...], wo_ref[...], preferred_element_type=F32)
    o_ref[...] = _layer_norm(alpha * x_ref[...] + y, g_ref[...], b_ref[...])


def _proj_ln(merged, w_o, x2, g, b, alpha, tm):
    M, D = x2.shape
    vec = pl.BlockSpec((1, D), lambda i: (0, 0))
    return pl.pallas_call(
        functools.partial(_proj_ln_kernel, alpha=alpha),
        out_shape=jax.ShapeDtypeStruct((M, D), F32),
        grid=(M // tm,),
        in_specs=[pl.BlockSpec((tm, D), lambda i: (i, 0)),
                  pl.BlockSpec((D, D), lambda i: (0, 0)),
                  pl.BlockSpec((tm, D), lambda i: (i, 0)), vec, vec],
        out_specs=pl.BlockSpec((tm, D), lambda i: (i, 0)),
        compiler_params=_cparams(("parallel",)),
        name="out_proj_ln",
    )(merged, w_o, x2, g, b)


def _mlp_kernel(x_ref, wu_ref, bu_ref, wd_ref, bd_ref, g_ref, b_ref, o_ref, xb_ref, acc_ref, *, alpha):
    f = pl.program_id(1)

    @pl.when(f == 0)
    def _():
        xb_ref[...] = x_ref[...].astype(BF16)
        acc_ref[...] = jnp.zeros_like(acc_ref)

    h = jnp.dot(xb_ref[...], wu_ref[...], preferred_element_type=F32) + bu_ref[...]
    h = jnp.square(jnp.maximum(h, 0.0))
    acc_ref[...] += jnp.dot(h.astype(BF16), wd_ref[...], preferred_element_type=F32)

    @pl.when(f == pl.num_programs(1) - 1)
    def _():
        y = alpha * x_ref[...] + acc_ref[...] + bd_ref[...]
        o_ref[...] = _layer_norm(y, g_ref[...], b_ref[...])


def _mlp(x1, w_up, b_up, w_down, b_down, g, b, alpha, tm, tf):
    M, D = x1.shape
    F = w_up.shape[1]
    vec = pl.BlockSpec((1, D), lambda i, f: (0, 0))
    return pl.pallas_call(
        functools.partial(_mlp_kernel, alpha=alpha),
        out_shape=jax.ShapeDtypeStruct((M, D), F32),
        grid=(M // tm, F // tf),
        in_specs=[pl.BlockSpec((tm, D), lambda i, f: (i, 0)),
                  pl.BlockSpec((D, tf), lambda i, f: (0, f)),
                  pl.BlockSpec((1, tf), lambda i, f: (0, f)),
                  pl.BlockSpec((tf, D), lambda i, f: (f, 0)),
                  vec, vec, vec],
        out_specs=pl.BlockSpec((tm, D), lambda i, f: (i, 0)),
        scratch_shapes=[pltpu.VMEM((tm, D), BF16), pltpu.VMEM((tm, D), F32)],
        compiler_params=_cparams(("parallel", "arbitrary")),
        name="mlp_ln",
    )(x1, w_up, b_up, w_down, b_down, g, b)


def _tile(n, pref):
    t = min(n, pref)
    assert n % t == 0
    return t


def _layer(x2, mem2, B, T, w_in, w_gate_up, b_gate, norm_g, w_mem_kv, w_br_gla, w_br_dsa, w_br_mem,
           b_merge, w_o, ln1_g, ln1_b, w_up, b_up, w_down, b_down, ln2_g, ln2_b, alpha):
    M, D = x2.shape
    Mt = mem2.shape[0] // B
    dk, dv = D // 8, D // 4
    dh = D // 16
    idim = D // 32
    dm = D // 8
    H, G = GLA_HEADS, IDX_HEADS
    widths = [H * dk, H * dk, H * dv, H * dv, GLA_GATE_RANK, DSA_HEADS * dh, DSA_HEADS * dh, DSA_HEADS * dh,
              G * idim, idim, G, MEM_HEADS * dm, N_BRANCH * D]
    names = ['gla_q', 'gla_k', 'gla_v', 'gla_r', 'glr', 'dsa_q', 'dsa_k', 'dsa_v', 'idx_q', 'idx_k', 'idx_w',
             'mem_q', 'gate']
    offs = dict(zip(names, np.cumsum([0] + widths[:-1]).tolist()))
    wd = dict(zip(names, widths))
    col = lambda n, scale=None: (w_in[:, offs[n]:offs[n] + wd[n]] if scale is None
                                 else w_in[:, offs[n]:offs[n] + wd[n]] * scale)
    big_parts = [('gla_q', dk ** -0.5), ('gla_k', None), ('gla_v', None), ('gla_r', None),
                 ('dsa_q', dh ** -0.5), ('dsa_k', None), ('dsa_v', None), ('idx_q', None),
                 ('mem_q', dm ** -0.5), ('gate', None)]
    w_big = jnp.concatenate([col(n, s) for n, s in big_parts], axis=1).astype(BF16)
    boff = dict(zip([n for n, _ in big_parts], np.cumsum([0] + [wd[n] for n, _ in big_parts][:-1]).tolist()))
    n_small = wd['glr'] + wd['idx_k'] + wd['idx_w']
    w_small = jnp.concatenate([col('glr'), col('idx_k'), col('idx_w'),
                               jnp.zeros((D, LANES - n_small), w_in.dtype)], axis=1).astype(BF16)

    p_big = _matmul(x2, w_big, BF16, 1024, 512, "in_proj")
    small = _matmul(x2, w_small, F32, 1024, LANES, "in_proj_small")

    o_gla = _gla(p_big, small, w_gate_up, b_gate.reshape(1, -1), norm_g.reshape(1, -1), B, T, dk, dv,
                 _tile(T, 256))

    tq, tk = _tile(T, 256), _tile(T, 512)
    topk = min(DSA_TOPK_MAX, T // 4)
    ik = small[:, wd['glr']:wd['glr'] + idim].astype(BF16)
    ikt = ik.reshape(B, T // tk, tk, idim).transpose(0, 1, 3, 2)
    bias = _dsa_mask(p_big, small, ikt, B, T, tq, tk, boff['idx_q'], wd['glr'] + idim, topk)
    o_dsa = _dsa_attn(p_big, bias, B, T, tq, tk, boff['dsa_q'], DSA_HEADS * dh)

    kv = _matmul(mem2, w_mem_kv.astype(BF16), BF16, 512, 512, "mem_kv")
    o_mem = _mem_attn(p_big, kv, B, T, Mt, dm, boff['mem_q'], _tile(T, 1024))

    merged = _merge(o_gla, o_dsa, o_mem, w_br_gla.astype(BF16), w_br_dsa.astype(BF16), w_br_mem.astype(BF16),
                    p_big, b_merge.reshape(1, -1), D, boff['gate'], _tile(M, 1024), 512)
    x1 = _proj_ln(merged, w_o.astype(BF16), x2, ln1_g.reshape(1, -1), ln1_b.reshape(1, -1), alpha, _tile(M, 256))
    return _mlp(x1, w_up.astype(BF16), b_up.reshape(1, -1), w_down.astype(BF16), b_down.reshape(1, -1),
                ln2_g.reshape(1, -1), ln2_b.reshape(1, -1), alpha, _tile(M, 512), 512)


def kernel(x, mem, w_in, w_gla_gate_up, b_gla_gate, gla_norm_g, w_mem_kv, w_br_gla, w_br_dsa, w_br_mem,
           b_merge, w_o, ln1_g, ln1_b, w_up, b_up, w_down, b_down, ln2_g, ln2_b):
    B, T, D = x.shape
    depth = w_in.shape[0]
    alpha = (2 * depth) ** 0.25
    x2 = x.reshape(B * T, D)
    mem2 = mem.reshape(-1, D)
    for l in range(depth):
        x2 = _layer(x2, mem2, B, T, w_in[l], w_gla_gate_up[l], b_gla_gate[l], gla_norm_g[l], w_mem_kv[l],
                    w_br_gla[l], w_br_dsa[l], w_br_mem[l], b_merge[l], w_o[l], ln1_g[l], ln1_b[l],
                    w_up[l], b_up[l], w_down[l], b_down[l], ln2_g[l], ln2_b[l], alpha)
    return x2.reshape(B, T, D)
```

```python
import functools

import numpy as np
import jax
import jax.numpy as jnp
from jax import lax
from jax.experimental import pallas as pl
from jax.experimental.pallas import tpu as pltpu

F32 = jnp.float32
BF16 = jnp.bfloat16

GLA_HEADS = 4
GLA_GATE_RANK = 16
GLA_GATE_TEMP = 16.0
GLA_CHUNK = 64
DSA_HEADS = 8
IDX_HEADS = 16
DSA_TOPK_MAX = 256
MEM_HEADS = 4
N_BRANCH = 3
LN_EPS = 1e-5
RMS_EPS = 1e-6

V7X_VMEM_LIMIT_BYTES = 56 * 1024 * 1024
LANES = 128
MASK_NEG = -1e30
ALIBI_SPLIT = 3
INT_MIN = -(2 ** 31)


def _cparams(sem):
    return pltpu.CompilerParams(dimension_semantics=sem, vmem_limit_bytes=V7X_VMEM_LIMIT_BYTES)


def _mm_cast_kernel(x_ref, w_ref, o_ref, xb_ref):
    @pl.when(pl.program_id(1) == 0)
    def _():
        xb_ref[...] = x_ref[...].astype(BF16)

    o_ref[...] = jnp.dot(xb_ref[...], w_ref[...], preferred_element_type=F32).astype(o_ref.dtype)


def _matmul(x, w, out_dtype, tm, tn, name):
    M, K = x.shape
    N = w.shape[1]
    tm, tn = min(tm, M), min(tn, N)
    assert M % tm == 0 and N % tn == 0
    return pl.pallas_call(
        _mm_cast_kernel,
        out_shape=jax.ShapeDtypeStruct((M, N), out_dtype),
        grid=(M // tm, N // tn),
        in_specs=[pl.BlockSpec((tm, K), lambda i, j: (i, 0)),
                  pl.BlockSpec((K, tn), lambda i, j: (0, j))],
        out_specs=pl.BlockSpec((tm, tn), lambda i, j: (i, j)),
        scratch_shapes=[pltpu.VMEM((tm, K), BF16)],
        compiler_params=_cparams(("parallel", "arbitrary")),
        name=name,
    )(x, w)


def _log_sigmoid(z):
    return jnp.minimum(z, 0.0) - jnp.log(1.0 + jnp.exp(-jnp.abs(z)))


def _gla_kernel(q_ref, k_ref, v_ref, r_ref, sm_ref, wg_ref, bg_ref, ng_ref, o_ref, st_ref, *, chunk, rank):
    @pl.when(pl.program_id(2) == 0)
    def _():
        st_ref[...] = jnp.zeros_like(st_ref)

    tb, dk = q_ref.shape
    C = chunk
    glr = sm_ref[:, 0:rank]
    z = jnp.dot(glr, wg_ref[...], preferred_element_type=F32) + bg_ref[...]
    la = _log_sigmoid(z) * (1.0 / GLA_GATE_TEMP)
    ri = lax.broadcasted_iota(jnp.int32, (tb, tb), 0)
    ci = lax.broadcasted_iota(jnp.int32, (tb, tb), 1)
    tri = jnp.where((ci <= ri) & ((ri // C) == (ci // C)), 1.0, 0.0).astype(F32)
    b = jnp.dot(tri, la, preferred_element_type=F32, precision=lax.Precision.HIGHEST)
    causal = (lax.broadcasted_iota(jnp.int32, (C, C), 1) <= lax.broadcasted_iota(jnp.int32, (C, C), 0))

    for c in range(tb // C):
        rows = slice(c * C, (c + 1) * C)
        bc = b[rows]
        br = bc[C // 2:C // 2 + 1]
        bl = bc[C - 1:C]
        qc = q_ref[rows, :].astype(F32)
        kc = k_ref[rows, :].astype(F32)
        vc = v_ref[rows, :]
        q_in = (qc * jnp.exp(bc - br)).astype(BF16)
        k_in = (kc * jnp.exp(br - bc)).astype(BF16)
        sc = lax.dot_general(q_in, k_in, (((1,), (1,)), ((), ())), preferred_element_type=F32)
        sc = jnp.where(causal, sc, 0.0)
        o = jnp.dot(sc.astype(BF16), vc, preferred_element_type=F32)
        q_it = (qc * jnp.exp(bc)).astype(BF16)
        st = st_ref[...]
        o = o + lax.dot_general(q_it, st.astype(BF16), (((1,), (1,)), ((), ())), preferred_element_type=F32)
        k_st = (kc * jnp.exp(bl - bc)).astype(BF16)
        upd = lax.dot_general(vc, k_st, (((0,), (0,)), ((), ())), preferred_element_type=F32)
        st_ref[...] = st * jnp.exp(bl) + upd
        o = o * lax.rsqrt(jnp.mean(o * o, axis=-1, keepdims=True) + RMS_EPS)
        r = r_ref[rows, :].astype(F32)
        o = o * ng_ref[...] * (r / (1.0 + jnp.exp(-r)))
        o_ref[rows, :] = o.astype(o_ref.dtype)


def _gla(p_big, small, w_gate_up, b_gate, norm_g, B, T, dk, dv, tb):
    M = B * T
    H = GLA_HEADS
    nb = T // tb
    row = lambda b, h, n: b * nb + n
    kern = functools.partial(_gla_kernel, chunk=GLA_CHUNK, rank=GLA_GATE_RANK)
    koff = (H * dk) // dk
    voff = (2 * H * dk) // dv
    roff = (2 * H * dk + H * dv) // dv
    return pl.pallas_call(
        kern,
        out_shape=jax.ShapeDtypeStruct((M, H * dv), BF16),
        grid=(B, H, nb),
        in_specs=[
            pl.BlockSpec((tb, dk), lambda b, h, n: (row(b, h, n), h)),
            pl.BlockSpec((tb, dk), lambda b, h, n: (row(b, h, n), koff + h)),
            pl.BlockSpec((tb, dv), lambda b, h, n: (row(b, h, n), voff + h)),
            pl.BlockSpec((tb, dv), lambda b, h, n: (row(b, h, n), roff + h)),
            pl.BlockSpec((tb, LANES), lambda b, h, n: (row(b, h, n), 0)),
            pl.BlockSpec((GLA_GATE_RANK, dk), lambda b, h, n: (0, h)),
            pl.BlockSpec((1, dk), lambda b, h, n: (0, h)),
            pl.BlockSpec((1, dv), lambda b, h, n: (0, h)),
        ],
        out_specs=pl.BlockSpec((tb, dv), lambda b, h, n: (row(b, h, n), h)),
        scratch_shapes=[pltpu.VMEM((dv, dk), F32)],
        compiler_params=_cparams(("parallel", "parallel", "arbitrary")),
        name="gla",
    )(p_big, p_big, p_big, p_big, small, w_gate_up, b_gate, norm_g)


def _idx_kernel(iq_ref, sm_ref, ikt_ref, bias_ref, keys_ref, iqs_ref, wb_ref, gmax_ref, *, topk, w_off, idim):
    i = pl.program_id(1)
    nT, tq, tk = keys_ref.shape
    G = iqs_ref.shape[0]
    n_lane = tk // LANES
    ntile = ((i + 1) * tq + tk - 1) // tk
    kf = float(topk)

    w = sm_ref[:, w_off:w_off + G] * ((G ** -0.5) * (idim ** -0.5))
    for g in range(G):
        iqs_ref[g] = iq_ref[:, g * idim:(g + 1) * idim]
        wb_ref[g] = jnp.broadcast_to(w[:, g:g + 1], (tq, LANES))
    row_pos = i * tq + lax.broadcasted_iota(jnp.int32, (tq, 1), 0)
    gmax_ref[...] = jnp.full(gmax_ref.shape, -jnp.inf, F32)

    def to_key(v):
        bits = pltpu.bitcast(v, jnp.int32)
        return bits ^ ((bits >> 31) & 0x7FFFFFFF)

    def score_tile(j, carry):
        ik = ikt_ref[j]
        acc = jnp.zeros((tq, tk), F32)
        for g in range(G):
            a = jnp.dot(iqs_ref[g], ik, preferred_element_type=F32)
            wg = jnp.concatenate([wb_ref[g]] * n_lane, axis=1)
            acc = acc + wg * jnp.maximum(a, 0.0)
        col_pos = j * tk + lax.broadcasted_iota(jnp.int32, (1, tk), 1)
        causal = col_pos <= row_pos
        keys_ref[j] = jnp.where(causal, to_key(acc), INT_MIN)
        sc = jnp.where(causal, acc, -jnp.inf)
        for c in range(n_lane):
            gmax_ref[c % 2] = jnp.maximum(gmax_ref[c % 2], sc[:, c * LANES:(c + 1) * LANES])
        return carry

    lax.fori_loop(0, ntile, score_tile, 0)

    RB = min(tq, 128)

    def count_ge(cand):
        outs = []
        for r in range(tq // RB):
            rows = slice(r * RB, (r + 1) * RB)
            cb = jnp.broadcast_to(cand[rows], (RB, LANES))

            def body(j, cnt, rows=rows, cb=cb):
                for c in range(n_lane):
                    kt = keys_ref[j, rows, c * LANES:(c + 1) * LANES]
                    cnt = cnt + jnp.where(kt >= cb, 1.0, 0.0)
                return cnt

            cnt = lax.fori_loop(0, ntile, body, jnp.zeros((RB, LANES), F32))
            outs.append(jnp.sum(cnt, axis=1, keepdims=True))
        return jnp.concatenate(outs, axis=0)

    g0, g1 = gmax_ref[0], gmax_ref[1]
    lo0 = to_key(jnp.minimum(jnp.min(g0, axis=1, keepdims=True), jnp.min(g1, axis=1, keepdims=True)))
    hi0 = to_key(jnp.maximum(jnp.max(g0, axis=1, keepdims=True), jnp.max(g1, axis=1, keepdims=True)))
    few = row_pos < topk
    lo0 = jnp.where(few, INT_MIN + 1, lo0)
    hi0 = jnp.where(few, INT_MIN + 1, hi0)

    def n_open(lo, hi):
        return jnp.sum(jnp.where(lo < hi, 1.0, 0.0)).astype(jnp.int32)

    def bisect(state):
        lo, hi, exact, _ = state
        x = lo ^ hi
        mid = (lo & hi) + (x >> 1) + (x & 1)
        cnt = count_ge(mid)
        ge, eq = cnt >= kf, cnt == kf
        lo = jnp.where(ge, mid, lo)
        hi = jnp.where(eq, mid, jnp.where(ge, hi, mid - 1))
        return lo, hi, jnp.where(eq, 1, exact), n_open(lo, hi)

    lo, hi, exact, _ = lax.while_loop(lambda s: s[3] > 0, bisect, (lo0, hi0, jnp.where(few, 1, 0), n_open(lo0, hi0)))
    thr = lo
    n_tied = jnp.sum(jnp.where(exact > 0, 0.0, 1.0)).astype(jnp.int32)

    @pl.when(n_tied == 0)
    def _():
        thr_b = jnp.broadcast_to(thr, (tq, LANES))

        def write_tile(j, carry):
            kt = keys_ref[j]
            parts = [jnp.where(kt[:, c * LANES:(c + 1) * LANES] >= thr_b, 0.0, MASK_NEG) for c in range(n_lane)]
            bias_ref[j] = jnp.concatenate(parts, axis=1).astype(bias_ref.dtype)
            return carry

        lax.fori_loop(0, ntile, write_tile, 0)

    @pl.when(n_tied > 0)
    def _():
        need = kf - count_ge(thr + 1)
        before = (lax.broadcasted_iota(jnp.int32, (tk, tk), 0) < lax.broadcasted_iota(jnp.int32, (tk, tk), 1))
        before = jnp.where(before, 1.0, 0.0).astype(BF16)

        def write_tile(j, run):
            kt = keys_ref[j]
            tied = jnp.where(kt == thr, 1.0, 0.0)
            rank = jnp.dot(tied.astype(BF16), before, preferred_element_type=F32) + run
            sel = (kt > thr) | ((kt == thr) & (rank < need))
            bias_ref[j] = jnp.where(sel, 0.0, MASK_NEG).astype(bias_ref.dtype)
            return run + jnp.sum(tied, axis=1, keepdims=True)

        lax.fori_loop(0, ntile, write_tile, jnp.zeros((tq, 1), F32))

    def fill_tile(j, carry):
        bias_ref[j] = jnp.full((tq, tk), MASK_NEG, bias_ref.dtype)
        return carry

    lax.fori_loop(ntile, nT, fill_tile, 0)


def _dsa_mask(p_big, small, ikt, B, T, tq, tk, iq_off, w_off, topk):
    G = IDX_HEADS
    idim = ikt.shape[2]
    nQ, nT = T // tq, T // tk
    kern = functools.partial(_idx_kernel, topk=topk, w_off=w_off, idim=idim)
    return pl.pallas_call(
        kern,
        out_shape=jax.ShapeDtypeStruct((B, nQ, nT, tq, tk), BF16),
        grid=(B, nQ),
        in_specs=[
            pl.BlockSpec((tq, G * idim), lambda b, i: (b * nQ + i, iq_off // (G * idim))),
            pl.BlockSpec((tq, LANES), lambda b, i: (b * nQ + i, 0)),
            pl.BlockSpec((None, nT, idim, tk), lambda b, i: (b, 0, 0, 0)),
        ],
        out_specs=pl.BlockSpec((None, None, nT, tq, tk), lambda b, i: (b, i, 0, 0, 0)),
        scratch_shapes=[pltpu.VMEM((nT, tq, tk), jnp.int32),
                        pltpu.VMEM((G, tq, idim), BF16),
                        pltpu.VMEM((G, tq, LANES), F32),
                        pltpu.VMEM((2, tq, LANES), F32)],
        compiler_params=_cparams(("parallel", "arbitrary")),
        name="dsa_index_topk",
    )(p_big, small, ikt)


def _dsa_attn_kernel(qi_ref, kj_ref, q_ref, k_ref, v_ref, bias_ref, sl_ref, o_ref, m_ref, l_ref, acc_ref, *, heads):
    p = pl.program_id(1)
    qi, kj = qi_ref[p], kj_ref[p]
    tq, tk = bias_ref.shape
    d = q_ref.shape[1] // heads
    n_lane = tk // LANES

    @pl.when(kj == 0)
    def _():
        m_ref[...] = jnp.full_like(m_ref, MASK_NEG)
        l_ref[...] = jnp.zeros_like(l_ref)
        acc_ref[...] = jnp.zeros_like(acc_ref)

    rel = (kj * tk - qi * tq) + lax.broadcasted_iota(jnp.int32, (tk, LANES), 0)
    lane = lax.broadcasted_iota(jnp.int32, (tk, LANES), 1)
    pos = jnp.where(lane < 2 * ALIBI_SPLIT, jnp.where(lane % 2 == 0, rel >> 8, rel & 255), 0)
    pos = pos.astype(F32).astype(BF16)
    bias = bias_ref[...].astype(F32)
    for h in range(heads):
        cols = slice(h * d, (h + 1) * d)
        qa = jnp.concatenate([q_ref[:, cols], sl_ref[h]], axis=1)
        ka = jnp.concatenate([k_ref[:, cols], pos], axis=1)
        s = lax.dot_general(qa, ka, (((1,), (1,)), ((), ())), preferred_element_type=F32)
        sb = [s[:, c * LANES:(c + 1) * LANES] + bias[:, c * LANES:(c + 1) * LANES] for c in range(n_lane)]
        mx = sb[0]
        for c in range(1, n_lane):
            mx = jnp.maximum(mx, sb[c])
        m_old = m_ref[h]
        m_new = jnp.maximum(m_old, jnp.max(mx, axis=1, keepdims=True))
        corr = jnp.exp2(m_old - m_new)
        ps = [jnp.exp2(sb[c] - m_new) for c in range(n_lane)]
        psum = ps[0]
        for c in range(1, n_lane):
            psum = psum + ps[c]
        l_ref[h] = corr * l_ref[h] + psum
        pv = jnp.dot(jnp.concatenate(ps, axis=1).astype(BF16), v_ref[:, cols], preferred_element_type=F32)
        acc_ref[:, cols] = corr * acc_ref[:, cols] + pv
        m_ref[h] = m_new

    @pl.when((kj + 1) * tk >= (qi + 1) * tq)
    def _():
        for h in range(heads):
            cols = slice(h * d, (h + 1) * d)
            l = jnp.sum(l_ref[h], axis=1, keepdims=True)
            o_ref[:, cols] = (acc_ref[:, cols] / l).astype(o_ref.dtype)


def _alibi_columns(heads, tq):
    out = np.zeros((heads, LANES), np.float32)
    for h in range(heads):
        rest = np.float64(2.0 ** (-8.0 * (h + 1) / heads) * np.log2(np.e))
        for i in range(ALIBI_SPLIT):
            piece = np.float64(np.float32(rest).astype(BF16))
            out[h, 2 * i], out[h, 2 * i + 1] = 256.0 * piece, piece
            rest = rest - piece
    return jnp.asarray(np.broadcast_to(out.astype(BF16)[:, None, :], (heads, tq, LANES)))


def _dsa_attn(p_big, bias, B, T, tq, tk, q_off, width):
    nQ, nT = T // tq, T // tk
    pairs = [(i, j) for i in range(nQ) for j in range(((i + 1) * tq + tk - 1) // tk)]
    qi_tbl = jnp.asarray(np.array([p[0] for p in pairs], np.int32))
    kj_tbl = jnp.asarray(np.array([p[1] for p in pairs], np.int32))
    qb = q_off // width
    heads = DSA_HEADS
    kern = functools.partial(_dsa_attn_kernel, heads=heads)
    grid_spec = pltpu.PrefetchScalarGridSpec(
        num_scalar_prefetch=2,
        grid=(B, len(pairs)),
        in_specs=[
            pl.BlockSpec((tq, width), lambda b, p, qi, kj: (b * nQ + qi[p], qb)),
            pl.BlockSpec((tk, width), lambda b, p, qi, kj: (b * nT + kj[p], qb + 1)),
            pl.BlockSpec((tk, width), lambda b, p, qi, kj: (b * nT + kj[p], qb + 2)),
            pl.BlockSpec((None, None, None, tq, tk), lambda b, p, qi, kj: (b, qi[p], kj[p], 0, 0)),
            pl.BlockSpec((heads, tq, LANES), lambda b, p, qi, kj: (0, 0, 0)),
        ],
        out_specs=pl.BlockSpec((tq, width), lambda b, p, qi, kj: (b * nQ + qi[p], 0)),
        scratch_shapes=[pltpu.VMEM((heads, tq, LANES), F32),
                        pltpu.VMEM((heads, tq, LANES), F32),
                        pltpu.VMEM((tq, width), F32)],
    )
    return pl.pallas_call(
        kern,
        out_shape=jax.ShapeDtypeStruct((B * T, width), BF16),
        grid_spec=grid_spec,
        compiler_params=_cparams(("arbitrary", "arbitrary")),
        name="dsa_attention",
    )(qi_tbl, kj_tbl, p_big, p_big, p_big, bias, _alibi_columns(heads, tq))


def _mem_attn_kernel(q_ref, k_ref, v_ref, o_ref):
    s = lax.dot_general(q_ref[...], k_ref[...], (((1,), (1,)), ((), ())), preferred_element_type=F32)
    m = jnp.max(s, axis=1, keepdims=True)
    p = jnp.exp(s - m)
    l = jnp.sum(p, axis=1, keepdims=True)
    o = jnp.dot(p.astype(BF16), v_ref[...], preferred_element_type=F32)
    o_ref[...] = (o / l).astype(o_ref.dtype)


def _mem_attn(p_big, kv, B, T, Mt, dm, q_off, tq):
    Hm = MEM_HEADS
    nq = T // tq
    qb = q_off // dm
    return pl.pallas_call(
        _mem_attn_kernel,
        out_shape=jax.ShapeDtypeStruct((B * T, Hm * dm), BF16),
        grid=(B, nq, Hm),
        in_specs=[
            pl.BlockSpec((tq, dm), lambda b, i, h: (b * nq + i, qb + h)),
            pl.BlockSpec((Mt, dm), lambda b, i, h: (b, h)),
            pl.BlockSpec((Mt, dm), lambda b, i, h: (b, Hm + h)),
        ],
        out_specs=pl.BlockSpec((tq, dm), lambda b, i, h: (b * nq + i, h)),
        compiler_params=_cparams(("parallel", "parallel", "arbitrary")),
        name="memory_attention",
    )(p_big, kv, kv)


def _merge_kernel(a_ref, d_ref, m_ref, wa_ref, wd_ref, wm_ref, g0_ref, g1_ref, g2_ref,
                  b0_ref, b1_ref, b2_ref, o_ref):
    def gate(g_ref, b_ref):
        return jax.nn.sigmoid(g_ref[...].astype(F32) + b_ref[...])

    acc = gate(g0_ref, b0_ref) * jnp.dot(a_ref[...], wa_ref[...], preferred_element_type=F32)
    acc = acc + gate(g1_ref, b1_ref) * jnp.dot(d_ref[...], wd_ref[...], preferred_element_type=F32)
    acc = acc + gate(g2_ref, b2_ref) * jnp.dot(m_ref[...], wm_ref[...], preferred_element_type=F32)
    o_ref[...] = acc.astype(o_ref.dtype)


def _merge(o_gla, o_dsa, o_mem, wa, wd, wm, p_big, b_merge, D, gate_off, tm, tn):
    M = o_gla.shape[0]
    gb = gate_off // tn
    nd = D // tn
    row = lambda w: pl.BlockSpec((tm, w), lambda i, j: (i, 0))
    col = lambda k: pl.BlockSpec((k, tn), lambda i, j: (0, j))
    gspec = lambda r: pl.BlockSpec((tm, tn), lambda i, j: (i, gb + r * nd + j))
    bspec = lambda r: pl.BlockSpec((1, tn), lambda i, j: (0, r * nd + j))
    return pl.pallas_call(
        _merge_kernel,
        out_shape=jax.ShapeDtypeStruct((M, D), BF16),
        grid=(M // tm, D // tn),
        in_specs=[row(o_gla.shape[1]), row(o_dsa.shape[1]), row(o_mem.shape[1]),
                  col(wa.shape[0]), col(wd.shape[0]), col(wm.shape[0]),
                  gspec(0), gspec(1), gspec(2), bspec(0), bspec(1), bspec(2)],
        out_specs=pl.BlockSpec((tm, tn), lambda i, j: (i, j)),
        compiler_params=_cparams(("parallel", "arbitrary")),
        name="gated_merge",
    )(o_gla, o_dsa, o_mem, wa, wd, wm, p_big, p_big, p_big, b_merge, b_merge, b_merge)


def _layer_norm(y, g, b):
    mu = jnp.mean(y, axis=-1, keepdims=True)
    yc = y - mu
    var = jnp.mean(yc * yc, axis=-1, keepdims=True)
    return yc * lax.rsqrt(var + LN_EPS) * g + b


def _proj_ln_kernel(mg_ref, wo_ref, x_ref, g_ref, b_ref, o_ref, *, alpha):
    y = jnp.dot(mg_ref[...], wo_ref[...], preferred_element_type=F32)
    o_ref[...] = _layer_norm(alpha * x_ref[...] + y, g_ref[...], b_ref[...])


def _proj_ln(merged, w_o, x2, g, b, alpha, tm):
    M, D = x2.shape
    vec = pl.BlockSpec((1, D), lambda i: (0, 0))
    return pl.pallas_call(
        functools.partial(_proj_ln_kernel, alpha=alpha),
        out_shape=jax.ShapeDtypeStruct((M, D), F32),
        grid=(M // tm,),
        in_specs=[pl.BlockSpec((tm, D), lambda i: (i, 0)),
                  pl.BlockSpec((D, D), lambda i: (0, 0)),
                  pl.BlockSpec((tm, D), lambda i: (i, 0)), vec, vec],
        out_specs=pl.BlockSpec((tm, D), lambda i: (i, 0)),
        compiler_params=_cparams(("parallel",)),
        name="out_proj_ln",
    )(merged, w_o, x2, g, b)


def _mlp_kernel(x_ref, wu_ref, bu_ref, wd_ref, bd_ref, g_ref, b_ref, o_ref, xb_ref, acc_ref, *, alpha):
    f = pl.program_id(1)

    @pl.when(f == 0)
    def _():
        xb_ref[...] = x_ref[...].astype(BF16)
        acc_ref[...] = jnp.zeros_like(acc_ref)

    h = jnp.dot(xb_ref[...], wu_ref[...], preferred_element_type=F32) + bu_ref[...]
    h = jnp.square(jnp.maximum(h, 0.0))
    acc_ref[...] += jnp.dot(h.astype(BF16), wd_ref[...], preferred_element_type=F32)

    @pl.when(f == pl.num_programs(1) - 1)
    def _():
        y = alpha * x_ref[...] + acc_ref[...] + bd_ref[...]
        o_ref[...] = _layer_norm(y, g_ref[...], b_ref[...])


def _mlp(x1, w_up, b_up, w_down, b_down, g, b, alpha, tm, tf):
    M, D = x1.shape
    F = w_up.shape[1]
    vec = pl.BlockSpec((1, D), lambda i, f: (0, 0))
    return pl.pallas_call(
        functools.partial(_mlp_kernel, alpha=alpha),
        out_shape=jax.ShapeDtypeStruct((M, D), F32),
        grid=(M // tm, F // tf),
        in_specs=[pl.BlockSpec((tm, D), lambda i, f: (i, 0)),
                  pl.BlockSpec((D, tf), lambda i, f: (0, f)),
                  pl.BlockSpec((1, tf), lambda i, f: (0, f)),
                  pl.BlockSpec((tf, D), lambda i, f: (f, 0)),
                  vec, vec, vec],
        out_specs=pl.BlockSpec((tm, D), lambda i, f: (i, 0)),
        scratch_shapes=[pltpu.VMEM((tm, D), BF16), pltpu.VMEM((tm, D), F32)],
        compiler_params=_cparams(("parallel", "arbitrary")),
        name="mlp_ln",
    )(x1, w_up, b_up, w_down, b_down, g, b)


def _tile(n, pref):
    t = min(n, pref)
    assert n % t == 0
    return t


def _layer(x2, mem2, B, T, w_in, w_gate_up, b_gate, norm_g, w_mem_kv, w_br_gla, w_br_dsa, w_br_mem,
           b_merge, w_o, ln1_g, ln1_b, w_up, b_up, w_down, b_down, ln2_g, ln2_b, alpha):
    M, D = x2.shape
    Mt = mem2.shape[0] // B
    dk, dv = D // 8, D // 4
    dh = D // 16
    idim = D // 32
    dm = D // 8
    H, G = GLA_HEADS, IDX_HEADS
    widths = [H * dk, H * dk, H * dv, H * dv, GLA_GATE_RANK, DSA_HEADS * dh, DSA_HEADS * dh, DSA_HEADS * dh,
              G * idim, idim, G, MEM_HEADS * dm, N_BRANCH * D]
    names = ['gla_q', 'gla_k', 'gla_v', 'gla_r', 'glr', 'dsa_q', 'dsa_k', 'dsa_v', 'idx_q', 'idx_k', 'idx_w',
             'mem_q', 'gate']
    offs = dict(zip(names, np.cumsum([0] + widths[:-1]).tolist()))
    wd = dict(zip(names, widths))
    col = lambda n, scale=None: (w_in[:, offs[n]:offs[n] + wd[n]] if scale is None
                                 else w_in[:, offs[n]:offs[n] + wd[n]] * scale)
    big_parts = [('gla_q', dk ** -0.5), ('gla_k', None), ('gla_v', None), ('gla_r', None),
                 ('dsa_q', dh ** -0.5 * float(np.log2(np.e))), ('dsa_k', None), ('dsa_v', None), ('idx_q', None),
                 ('mem_q', dm ** -0.5), ('gate', None)]
    w_big = jnp.concatenate([col(n, s) for n, s in big_parts], axis=1).astype(BF16)
    boff = dict(zip([n for n, _ in big_parts], np.cumsum([0] + [wd[n] for n, _ in big_parts][:-1]).tolist()))
    n_small = wd['glr'] + wd['idx_k'] + wd['idx_w']
    w_small = jnp.concatenate([col('glr'), col('idx_k'), col('idx_w'),
                               jnp.zeros((D, LANES - n_small), w_in.dtype)], axis=1).astype(BF16)

    p_big = _matmul(x2, w_big, BF16, 1024, 512, "in_proj")
    small = _matmul(x2, w_small, F32, 1024, LANES, "in_proj_small")

    o_gla = _gla(p_big, small, w_gate_up, b_gate.reshape(1, -1), norm_g.reshape(1, -1), B, T, dk, dv,
                 _tile(T, 256))

    tq, tk = _tile(T, 256), _tile(T, 512)
    topk = min(DSA_TOPK_MAX, T // 4)
    ik = small[:, wd['glr']:wd['glr'] + idim].astype(BF16)
    ikt = ik.reshape(B, T // tk, tk, idim).transpose(0, 1, 3, 2)
    bias = _dsa_mask(p_big, small, ikt, B, T, tq, tk, boff['idx_q'], wd['glr'] + idim, topk)
    o_dsa = _dsa_attn(p_big, bias, B, T, tq, tk, boff['dsa_q'], DSA_HEADS * dh)

    kv = _matmul(mem2, w_mem_kv.astype(BF16), BF16, 512, 512, "mem_kv")
    o_mem = _mem_attn(p_big, kv, B, T, Mt, dm, boff['mem_q'], _tile(T, 1024))

    merged = _merge(o_gla, o_dsa, o_mem, w_br_gla.astype(BF16), w_br_dsa.astype(BF16), w_br_mem.astype(BF16),
                    p_big, b_merge.reshape(1, -1), D, boff['gate'], _tile(M, 1024), 512)
    x1 = _proj_ln(merged, w_o.astype(BF16), x2, ln1_g.reshape(1, -1), ln1_b.reshape(1, -1), alpha, _tile(M, 256))
    return _mlp(x1, w_up.astype(BF16), b_up.reshape(1, -1), w_down.astype(BF16), b_down.reshape(1, -1),
                ln2_g.reshape(1, -1), ln2_b.reshape(1, -1), alpha, _tile(M, 512), 512)


def kernel(x, mem, w_in, w_gla_gate_up, b_gla_gate, gla_norm_g, w_mem_kv, w_br_gla, w_br_dsa, w_br_mem,
           b_merge, w_o, ln1_g, ln1_b, w_up, b_up, w_down, b_down, ln2_g, ln2_b):
    B, T, D = x.shape
    depth = w_in.shape[0]
    alpha = (2 * depth) ** 0.25
    x2 = x.reshape(B * T, D)
    mem2 = mem.reshape(-1, D)
    for l in range(depth):
        x2 = _layer(x2, mem2, B, T, w_in[l], w_gla_gate_up[l], b_gla_gate[l], gla_norm_g[l], w_mem_kv[l],
                    w_br_gla[l], w_br_dsa[l], w_br_mem[l], b_merge[l], w_o[l], ln1_g[l], ln1_b[l],
                    w_up[l], b_up[l], w_down[l], b_down[l], ln2_g[l], ln2_b[l], alpha)
    return x2.reshape(B, T, D)
```

```python
import functools

import numpy as np
import jax
import jax.numpy as jnp
from jax import lax
from jax.experimental import pallas as pl
from jax.experimental.pallas import tpu as pltpu

F32 = jnp.float32
BF16 = jnp.bfloat16

GLA_HEADS = 4
GLA_GATE_RANK = 16
GLA_GATE_TEMP = 16.0
GLA_CHUNK = 64
DSA_HEADS = 8
IDX_HEADS = 16
DSA_TOPK_MAX = 256
MEM_HEADS = 4
N_BRANCH = 3
LN_EPS = 1e-5
RMS_EPS = 1e-6

V7X_VMEM_LIMIT_BYTES = 56 * 1024 * 1024
LANES = 128
MASK_NEG = -1e30
ALIBI_SPLIT = 3
INT_MIN = -(2 ** 31)


def _cparams(sem):
    return pltpu.CompilerParams(dimension_semantics=sem, vmem_limit_bytes=V7X_VMEM_LIMIT_BYTES)


def _mm_cast_kernel(x_ref, w_ref, o_ref, xb_ref):
    @pl.when(pl.program_id(1) == 0)
    def _():
        xb_ref[...] = x_ref[...].astype(BF16)

    o_ref[...] = jnp.dot(xb_ref[...], w_ref[...], preferred_element_type=F32).astype(o_ref.dtype)


def _matmul(x, w, out_dtype, tm, tn, name):
    M, K = x.shape
    N = w.shape[1]
    tm, tn = min(tm, M), min(tn, N)
    assert M % tm == 0 and N % tn == 0
    return pl.pallas_call(
        _mm_cast_kernel,
        out_shape=jax.ShapeDtypeStruct((M, N), out_dtype),
        grid=(M // tm, N // tn),
        in_specs=[pl.BlockSpec((tm, K), lambda i, j: (i, 0)),
                  pl.BlockSpec((K, tn), lambda i, j: (0, j))],
        out_specs=pl.BlockSpec((tm, tn), lambda i, j: (i, j)),
        scratch_shapes=[pltpu.VMEM((tm, K), BF16)],
        compiler_params=_cparams(("parallel", "arbitrary")),
        name=name,
    )(x, w)


def _log_sigmoid(z):
    return jnp.minimum(z, 0.0) - jnp.log(1.0 + jnp.exp(-jnp.abs(z)))


def _gla_kernel(q_ref, k_ref, v_ref, r_ref, sm_ref, wg_ref, bg_ref, ng_ref, o_ref, st_ref, *, chunk, rank):
    @pl.when(pl.program_id(2) == 0)
    def _():
        st_ref[...] = jnp.zeros_like(st_ref)

    tb, dk = q_ref.shape
    C = chunk
    glr = sm_ref[:, 0:rank]
    z = jnp.dot(glr, wg_ref[...], preferred_element_type=F32) + bg_ref[...]
    la = _log_sigmoid(z) * (1.0 / GLA_GATE_TEMP)
    ri = lax.broadcasted_iota(jnp.int32, (tb, tb), 0)
    ci = lax.broadcasted_iota(jnp.int32, (tb, tb), 1)
    tri = jnp.where((ci <= ri) & ((ri // C) == (ci // C)), 1.0, 0.0).astype(F32)
    b = jnp.dot(tri, la, preferred_element_type=F32, precision=lax.Precision.HIGHEST)
    causal = (lax.broadcasted_iota(jnp.int32, (C, C), 1) <= lax.broadcasted_iota(jnp.int32, (C, C), 0))

    for c in range(tb // C):
        rows = slice(c * C, (c + 1) * C)
        bc = b[rows]
        br = bc[C // 2:C // 2 + 1]
        bl = bc[C - 1:C]
        qc = q_ref[rows, :].astype(F32)
        kc = k_ref[rows, :].astype(F32)
        vc = v_ref[rows, :]
        q_in = (qc * jnp.exp(bc - br)).astype(BF16)
        k_in = (kc * jnp.exp(br - bc)).astype(BF16)
        sc = lax.dot_general(q_in, k_in, (((1,), (1,)), ((), ())), preferred_element_type=F32)
        sc = jnp.where(causal, sc, 0.0)
        o = jnp.dot(sc.astype(BF16), vc, preferred_element_type=F32)
        q_it = (qc * jnp.exp(bc)).astype(BF16)
        st = st_ref[...]
        o = o + lax.dot_general(q_it, st.astype(BF16), (((1,), (1,)), ((), ())), preferred_element_type=F32)
        k_st = (kc * jnp.exp(bl - bc)).astype(BF16)
        upd = lax.dot_general(vc, k_st, (((0,), (0,)), ((), ())), preferred_element_type=F32)
        st_ref[...] = st * jnp.exp(bl) + upd
        o = o * lax.rsqrt(jnp.mean(o * o, axis=-1, keepdims=True) + RMS_EPS)
        r = r_ref[rows, :].astype(F32)
        o = o * ng_ref[...] * (r / (1.0 + jnp.exp(-r)))
        o_ref[rows, :] = o.astype(o_ref.dtype)


def _gla(p_big, small, w_gate_up, b_gate, norm_g, B, T, dk, dv, tb):
    M = B * T
    H = GLA_HEADS
    nb = T // tb
    row = lambda b, h, n: b * nb + n
    kern = functools.partial(_gla_kernel, chunk=GLA_CHUNK, rank=GLA_GATE_RANK)
    koff = (H * dk) // dk
    voff = (2 * H * dk) // dv
    roff = (2 * H * dk + H * dv) // dv
    return pl.pallas_call(
        kern,
        out_shape=jax.ShapeDtypeStruct((M, H * dv), BF16),
        grid=(B, H, nb),
        in_specs=[
            pl.BlockSpec((tb, dk), lambda b, h, n: (row(b, h, n), h)),
            pl.BlockSpec((tb, dk), lambda b, h, n: (row(b, h, n), koff + h)),
            pl.BlockSpec((tb, dv), lambda b, h, n: (row(b, h, n), voff + h)),
            pl.BlockSpec((tb, dv), lambda b, h, n: (row(b, h, n), roff + h)),
            pl.BlockSpec((tb, LANES), lambda b, h, n: (row(b, h, n), 0)),
            pl.BlockSpec((GLA_GATE_RANK, dk), lambda b, h, n: (0, h)),
            pl.BlockSpec((1, dk), lambda b, h, n: (0, h)),
            pl.BlockSpec((1, dv), lambda b, h, n: (0, h)),
        ],
        out_specs=pl.BlockSpec((tb, dv), lambda b, h, n: (row(b, h, n), h)),
        scratch_shapes=[pltpu.VMEM((dv, dk), F32)],
        compiler_params=_cparams(("parallel", "parallel", "arbitrary")),
        name="gla",
    )(p_big, p_big, p_big, p_big, small, w_gate_up, b_gate, norm_g)


def _idx_kernel(iq_ref, sm_ref, ikt_ref, bias_ref, keys_ref, iqs_ref, wb_ref, gmax_ref, *, topk, w_off, idim):
    i = pl.program_id(1)
    nT, tq, tk = keys_ref.shape
    G = iqs_ref.shape[0]
    n_lane = tk // LANES
    ntile = ((i + 1) * tq + tk - 1) // tk
    kf = float(topk)

    w = sm_ref[:, w_off:w_off + G] * ((G ** -0.5) * (idim ** -0.5))
    for g in range(G):
        iqs_ref[g] = iq_ref[:, g * idim:(g + 1) * idim]
        wb_ref[g] = jnp.broadcast_to(w[:, g:g + 1], (tq, LANES))
    row_pos = i * tq + lax.broadcasted_iota(jnp.int32, (tq, 1), 0)
    gmax_ref[...] = jnp.full(gmax_ref.shape, -jnp.inf, F32)

    def to_key(v):
        bits = pltpu.bitcast(v, jnp.int32)
        return bits ^ ((bits >> 31) & 0x7FFFFFFF)

    def score_tile(j, carry):
        ik = ikt_ref[j]
        acc = jnp.zeros((tq, tk), F32)
        for g in range(G):
            a = jnp.dot(iqs_ref[g], ik, preferred_element_type=F32)
            wg = jnp.concatenate([wb_ref[g]] * n_lane, axis=1)
            acc = acc + wg * jnp.maximum(a, 0.0)
        col_pos = j * tk + lax.broadcasted_iota(jnp.int32, (1, tk), 1)
        causal = col_pos <= row_pos
        keys_ref[j] = jnp.where(causal, to_key(acc), INT_MIN)
        sc = jnp.where(causal, acc, -jnp.inf)
        for c in range(n_lane):
            gmax_ref[c % 2] = jnp.maximum(gmax_ref[c % 2], sc[:, c * LANES:(c + 1) * LANES])
        return carry

    lax.fori_loop(0, ntile, score_tile, 0)

    RB = min(tq, 128)

    def count_ge(cand):
        outs = []
        for r in range(tq // RB):
            rows = slice(r * RB, (r + 1) * RB)
            cb = jnp.broadcast_to(cand[rows], (RB, LANES))

            def body(j, cnt, rows=rows, cb=cb):
                for c in range(n_lane):
                    kt = keys_ref[j, rows, c * LANES:(c + 1) * LANES]
                    cnt = cnt + jnp.where(kt >= cb, 1.0, 0.0)
                return cnt

            cnt = lax.fori_loop(0, ntile, body, jnp.zeros((RB, LANES), F32))
            outs.append(jnp.sum(cnt, axis=1, keepdims=True))
        return jnp.concatenate(outs, axis=0)

    g0, g1 = gmax_ref[0], gmax_ref[1]
    lo0 = to_key(jnp.minimum(jnp.min(g0, axis=1, keepdims=True), jnp.min(g1, axis=1, keepdims=True)))
    hi0 = to_key(jnp.maximum(jnp.max(g0, axis=1, keepdims=True), jnp.max(g1, axis=1, keepdims=True)))
    few = row_pos < topk
    lo0 = jnp.where(few, INT_MIN + 1, lo0)
    hi0 = jnp.where(few, INT_MIN + 1, hi0)

    def n_open(lo, hi):
        return jnp.sum(jnp.where(lo < hi, 1.0, 0.0)).astype(jnp.int32)

    def bisect(state):
        lo, hi, exact, _ = state
        x = lo ^ hi
        mid = (lo & hi) + (x >> 1) + (x & 1)
        cnt = count_ge(mid)
        ge, eq = cnt >= kf, cnt == kf
        lo = jnp.where(ge, mid, lo)
        hi = jnp.where(eq, mid, jnp.where(ge, hi, mid - 1))
        return lo, hi, jnp.where(eq, 1, exact), n_open(lo, hi)

    lo, hi, exact, _ = lax.while_loop(lambda s: s[3] > 0, bisect, (lo0, hi0, jnp.where(few, 1, 0), n_open(lo0, hi0)))
    thr = lo
    n_tied = jnp.sum(jnp.where(exact > 0, 0.0, 1.0)).astype(jnp.int32)

    @pl.when(n_tied == 0)
    def _():
        thr_b = jnp.broadcast_to(thr, (tq, LANES))

        def write_tile(j, carry):
            kt = keys_ref[j]
            parts = [jnp.where(kt[:, c * LANES:(c + 1) * LANES] >= thr_b, 0.0, MASK_NEG) for c in range(n_lane)]
            bias_ref[j] = jnp.concatenate(parts, axis=1).astype(bias_ref.dtype)
            return carry

        lax.fori_loop(0, ntile, write_tile, 0)

    @pl.when(n_tied > 0)
    def _():
        need = kf - count_ge(thr + 1)
        before = (lax.broadcasted_iota(jnp.int32, (tk, tk), 0) < lax.broadcasted_iota(jnp.int32, (tk, tk), 1))
        before = jnp.where(before, 1.0, 0.0).astype(BF16)

        def write_tile(j, run):
            kt = keys_ref[j]
            tied = jnp.where(kt == thr, 1.0, 0.0)
            rank = jnp.dot(tied.astype(BF16), before, preferred_element_type=F32) + run
            sel = (kt > thr) | ((kt == thr) & (rank < need))
            bias_ref[j] = jnp.where(sel, 0.0, MASK_NEG).astype(bias_ref.dtype)
            return run + jnp.sum(tied, axis=1, keepdims=True)

        lax.fori_loop(0, ntile, write_tile, jnp.zeros((tq, 1), F32))

    def fill_tile(j, carry):
        bias_ref[j] = jnp.full((tq, tk), MASK_NEG, bias_ref.dtype)
        return carry

    lax.fori_loop(ntile, nT, fill_tile, 0)


def _dsa_mask(p_big, small, ikt, B, T, tq, tk, iq_off, w_off, topk):
    G = IDX_HEADS
    idim = ikt.shape[2]
    nQ, nT = T // tq, T // tk
    kern = functools.partial(_idx_kernel, topk=topk, w_off=w_off, idim=idim)
    return pl.pallas_call(
        kern,
        out_shape=jax.ShapeDtypeStruct((B, nQ, nT, tq, tk), BF16),
        grid=(B, nQ),
        in_specs=[
            pl.BlockSpec((tq, G * idim), lambda b, i: (b * nQ + i, iq_off // (G * idim))),
            pl.BlockSpec((tq, LANES), lambda b, i: (b * nQ + i, 0)),
            pl.BlockSpec((None, nT, idim, tk), lambda b, i: (b, 0, 0, 0)),
        ],
        out_specs=pl.BlockSpec((None, None, nT, tq, tk), lambda b, i: (b, i, 0, 0, 0)),
        scratch_shapes=[pltpu.VMEM((nT, tq, tk), jnp.int32),
                        pltpu.VMEM((G, tq, idim), BF16),
                        pltpu.VMEM((G, tq, LANES), F32),
                        pltpu.VMEM((2, tq, LANES), F32)],
        compiler_params=_cparams(("parallel", "arbitrary")),
        name="dsa_index_topk",
    )(p_big, small, ikt)


def _dsa_attn_kernel(qi_ref, kj_ref, q_ref, k_ref, v_ref, bias_ref, sl_ref, o_ref, m_ref, l_ref, acc_ref, *, heads):
    p = pl.program_id(1)
    qi, kj = qi_ref[p], kj_ref[p]
    tq, tk = bias_ref.shape
    d = q_ref.shape[1] // heads
    n_lane = tk // LANES

    @pl.when(kj == 0)
    def _():
        m_ref[...] = jnp.full_like(m_ref, MASK_NEG)
        l_ref[...] = jnp.zeros_like(l_ref)
        acc_ref[...] = jnp.zeros_like(acc_ref)

    rel = (kj * tk - qi * tq) + lax.broadcasted_iota(jnp.int32, (tk, LANES), 0)
    lane = lax.broadcasted_iota(jnp.int32, (tk, LANES), 1)
    pos = jnp.where(lane < 2 * ALIBI_SPLIT, jnp.where(lane % 2 == 0, rel >> 8, rel & 255), 0)
    pos = pos.astype(F32).astype(BF16)
    bias = bias_ref[...].astype(F32)
    ones_col = jnp.where(lane == 0, 1.0, 0.0).astype(BF16)

    def masked_logits(h):
        cols = slice(h * d, (h + 1) * d)
        qa = jnp.concatenate([q_ref[:, cols], sl_ref[h]], axis=1)
        ka = jnp.concatenate([k_ref[:, cols], pos], axis=1)
        s = lax.dot_general(qa, ka, (((1,), (1,)), ((), ())), preferred_element_type=F32)
        return [s[:, c * LANES:(c + 1) * LANES] + bias[:, c * LANES:(c + 1) * LANES] for c in range(n_lane)]

    sb_next = masked_logits(0)
    for h in range(heads):
        cols = slice(h * d, (h + 1) * d)
        sb = sb_next
        if h + 1 < heads:
            sb_next = masked_logits(h + 1)
        mx = sb[0]
        for c in range(1, n_lane):
            mx = jnp.maximum(mx, sb[c])
        m_old = m_ref[h]
        m_new = jnp.maximum(m_old, jnp.max(mx, axis=1, keepdims=True))
        corr = jnp.exp2(m_old - m_new)
        ps = [jnp.exp2((sb[c] - m_new).astype(BF16)) for c in range(n_lane)]
        va = jnp.concatenate([v_ref[:, cols], ones_col], axis=1)
        pv = jnp.dot(jnp.concatenate(ps, axis=1), va, preferred_element_type=F32)
        l_ref[h] = corr * l_ref[h] + pv[:, d:]
        acc_ref[:, cols] = corr * acc_ref[:, cols] + pv[:, :d]
        m_ref[h] = m_new

    @pl.when((kj + 1) * tk >= (qi + 1) * tq)
    def _():
        for h in range(heads):
            cols = slice(h * d, (h + 1) * d)
            l = jnp.sum(l_ref[h], axis=1, keepdims=True)
            o_ref[:, cols] = (acc_ref[:, cols] / l).astype(o_ref.dtype)


def _alibi_columns(heads, tq):
    out = np.zeros((heads, LANES), np.float32)
    for h in range(heads):
        rest = np.float64(2.0 ** (-8.0 * (h + 1) / heads) * np.log2(np.e))
        for i in range(ALIBI_SPLIT):
            piece = np.float64(np.float32(rest).astype(BF16))
            out[h, 2 * i], out[h, 2 * i + 1] = 256.0 * piece, piece
            rest = rest - piece
    return jnp.asarray(np.broadcast_to(out.astype(BF16)[:, None, :], (heads, tq, LANES)))


def _dsa_attn(p_big, bias, B, T, tq, tk, q_off, width):
    nQ, nT = T // tq, T // tk
    pairs = [(i, j) for i in range(nQ) for j in range(((i + 1) * tq + tk - 1) // tk)]
    qi_tbl = jnp.asarray(np.array([p[0] for p in pairs], np.int32))
    kj_tbl = jnp.asarray(np.array([p[1] for p in pairs], np.int32))
    qb = q_off // width
    heads = DSA_HEADS
    kern = functools.partial(_dsa_attn_kernel, heads=heads)
    grid_spec = pltpu.PrefetchScalarGridSpec(
        num_scalar_prefetch=2,
        grid=(B, len(pairs)),
        in_specs=[
            pl.BlockSpec((tq, width), lambda b, p, qi, kj: (b * nQ + qi[p], qb)),
            pl.BlockSpec((tk, width), lambda b, p, qi, kj: (b * nT + kj[p], qb + 1)),
            pl.BlockSpec((tk, width), lambda b, p, qi, kj: (b * nT + kj[p], qb + 2)),
            pl.BlockSpec((None, None, None, tq, tk), lambda b, p, qi, kj: (b, qi[p], kj[p], 0, 0)),
            pl.BlockSpec((heads, tq, LANES), lambda b, p, qi, kj: (0, 0, 0)),
        ],
        out_specs=pl.BlockSpec((tq, width), lambda b, p, qi, kj: (b * nQ + qi[p], 0)),
        scratch_shapes=[pltpu.VMEM((heads, tq, LANES), F32),
                        pltpu.VMEM((heads, tq, LANES), F32),
                        pltpu.VMEM((tq, width), F32)],
    )
    return pl.pallas_call(
        kern,
        out_shape=jax.ShapeDtypeStruct((B * T, width), BF16),
        grid_spec=grid_spec,
        compiler_params=_cparams(("arbitrary", "arbitrary")),
        name="dsa_attention",
    )(qi_tbl, kj_tbl, p_big, p_big, p_big, bias, _alibi_columns(heads, tq))


def _mem_attn_kernel(q_ref, k_ref, v_ref, o_ref):
    s = lax.dot_general(q_ref[...], k_ref[...], (((1,), (1,)), ((), ())), preferred_element_type=F32)
    m = jnp.max(s, axis=1, keepdims=True)
    p = jnp.exp(s - m)
    l = jnp.sum(p, axis=1, keepdims=True)
    o = jnp.dot(p.astype(BF16), v_ref[...], preferred_element_type=F32)
    o_ref[...] = (o / l).astype(o_ref.dtype)


def _mem_attn(p_big, kv, B, T, Mt, dm, q_off, tq):
    Hm = MEM_HEADS
    nq = T // tq
    qb = q_off // dm
    return pl.pallas_call(
        _mem_attn_kernel,
        out_shape=jax.ShapeDtypeStruct((B * T, Hm * dm), BF16),
        grid=(B, nq, Hm),
        in_specs=[
            pl.BlockSpec((tq, dm), lambda b, i, h: (b * nq + i, qb + h)),
            pl.BlockSpec((Mt, dm), lambda b, i, h: (b, h)),
            pl.BlockSpec((Mt, dm), lambda b, i, h: (b, Hm + h)),
        ],
        out_specs=pl.BlockSpec((tq, dm), lambda b, i, h: (b * nq + i, h)),
        compiler_params=_cparams(("parallel", "parallel", "arbitrary")),
        name="memory_attention",
    )(p_big, kv, kv)


def _merge_kernel(a_ref, d_ref, m_ref, wa_ref, wd_ref, wm_ref, g0_ref, g1_ref, g2_ref,
                  b0_ref, b1_ref, b2_ref, o_ref):
    def gate(g_ref, b_ref):
        return jax.nn.sigmoid(g_ref[...].astype(F32) + b_ref[...])

    acc = gate(g0_ref, b0_ref) * jnp.dot(a_ref[...], wa_ref[...], preferred_element_type=F32)
    acc = acc + gate(g1_ref, b1_ref) * jnp.dot(d_ref[...], wd_ref[...], preferred_element_type=F32)
    acc = acc + gate(g2_ref, b2_ref) * jnp.dot(m_ref[...], wm_ref[...], preferred_element_type=F32)
    o_ref[...] = acc.astype(o_ref.dtype)


def _merge(o_gla, o_dsa, o_mem, wa, wd, wm, p_big, b_merge, D, gate_off, tm, tn):
    M = o_gla.shape[0]
    gb = gate_off // tn
    nd = D // tn
    row = lambda w: pl.BlockSpec((tm, w), lambda i, j: (i, 0))
    col = lambda k: pl.BlockSpec((k, tn), lambda i, j: (0, j))
    gspec = lambda r: pl.BlockSpec((tm, tn), lambda i, j: (i, gb + r * nd + j))
    bspec = lambda r: pl.BlockSpec((1, tn), lambda i, j: (0, r * nd + j))
    return pl.pallas_call(
        _merge_kernel,
        out_shape=jax.ShapeDtypeStruct((M, D), BF16),
        grid=(M // tm, D // tn),
        in_specs=[row(o_gla.shape[1]), row(o_dsa.shape[1]), row(o_mem.shape[1]),
                  col(wa.shape[0]), col(wd.shape[0]), col(wm.shape[0]),
                  gspec(0), gspec(1), gspec(2), bspec(0), bspec(1), bspec(2)],
        out_specs=pl.BlockSpec((tm, tn), lambda i, j: (i, j)),
        compiler_params=_cparams(("parallel", "arbitrary")),
        name="gated_merge",
    )(o_gla, o_dsa, o_mem, wa, wd, wm, p_big, p_big, p_big, b_merge, b_merge, b_merge)


def _layer_norm(y, g, b):
    mu = jnp.mean(y, axis=-1, keepdims=True)
    yc = y - mu
    var = jnp.mean(yc * yc, axis=-1, keepdims=True)
    return yc * lax.rsqrt(var + LN_EPS) * g + b


def _proj_ln_kernel(mg_ref, wo_ref, x_ref, g_ref, b_ref, o_ref, *, alpha):
    y = jnp.dot(mg_ref[...], wo_ref[...], preferred_element_type=F32)
    o_ref[...] = _layer_norm(alpha * x_ref[...] + y, g_ref[...], b_ref[...])


def _proj_ln(merged, w_o, x2, g, b, alpha, tm):
    M, D = x2.shape
    vec = pl.BlockSpec((1, D), lambda i: (0, 0))
    return pl.pallas_call(
        functools.partial(_proj_ln_kernel, alpha=alpha),
        out_shape=jax.ShapeDtypeStruct((M, D), F32),
        grid=(M // tm,),
        in_specs=[pl.BlockSpec((tm, D), lambda i: (i, 0)),
                  pl.BlockSpec((D, D), lambda i: (0, 0)),
                  pl.BlockSpec((tm, D), lambda i: (i, 0)), vec, vec],
        out_specs=pl.BlockSpec((tm, D), lambda i: (i, 0)),
        compiler_params=_cparams(("parallel",)),
        name="out_proj_ln",
    )(merged, w_o, x2, g, b)


def _mlp_kernel(x_ref, wu_ref, bu_ref, wd_ref, bd_ref, g_ref, b_ref, o_ref, xb_ref, acc_ref, *, alpha):
    f = pl.program_id(1)

    @pl.when(f == 0)
    def _():
        xb_ref[...] = x_ref[...].astype(BF16)
        acc_ref[...] = jnp.zeros_like(acc_ref)

    h = jnp.dot(xb_ref[...], wu_ref[...], preferred_element_type=F32) + bu_ref[...]
    h = jnp.square(jnp.maximum(h, 0.0))
    acc_ref[...] += jnp.dot(h.astype(BF16), wd_ref[...], preferred_element_type=F32)

    @pl.when(f == pl.num_programs(1) - 1)
    def _():
        y = alpha * x_ref[...] + acc_ref[...] + bd_ref[...]
        o_ref[...] = _layer_norm(y, g_ref[...], b_ref[...])


def _mlp(x1, w_up, b_up, w_down, b_down, g, b, alpha, tm, tf):
    M, D = x1.shape
    F = w_up.shape[1]
    vec = pl.BlockSpec((1, D), lambda i, f: (0, 0))
    return pl.pallas_call(
        functools.partial(_mlp_kernel, alpha=alpha),
        out_shape=jax.ShapeDtypeStruct((M, D), F32),
        grid=(M // tm, F // tf),
        in_specs=[pl.BlockSpec((tm, D), lambda i, f: (i, 0)),
                  pl.BlockSpec((D, tf), lambda i, f: (0, f)),
                  pl.BlockSpec((1, tf), lambda i, f: (0, f)),
                  pl.BlockSpec((tf, D), lambda i, f: (f, 0)),
                  vec, vec, vec],
        out_specs=pl.BlockSpec((tm, D), lambda i, f: (i, 0)),
        scratch_shapes=[pltpu.VMEM((tm, D), BF16), pltpu.VMEM((tm, D), F32)],
        compiler_params=_cparams(("parallel", "arbitrary")),
        name="mlp_ln",
    )(x1, w_up, b_up, w_down, b_down, g, b)


def _tile(n, pref):
    t = min(n, pref)
    assert n % t == 0
    return t


def _layer(x2, mem2, B, T, w_in, w_gate_up, b_gate, norm_g, w_mem_kv, w_br_gla, w_br_dsa, w_br_mem,
           b_merge, w_o, ln1_g, ln1_b, w_up, b_up, w_down, b_down, ln2_g, ln2_b, alpha):
    M, D = x2.shape
    Mt = mem2.shape[0] // B
    dk, dv = D // 8, D // 4
    dh = D // 16
    idim = D // 32
    dm = D // 8
    H, G = GLA_HEADS, IDX_HEADS
    widths = [H * dk, H * dk, H * dv, H * dv, GLA_GATE_RANK, DSA_HEADS * dh, DSA_HEADS * dh, DSA_HEADS * dh,
              G * idim, idim, G, MEM_HEADS * dm, N_BRANCH * D]
    names = ['gla_q', 'gla_k', 'gla_v', 'gla_r', 'glr', 'dsa_q', 'dsa_k', 'dsa_v', 'idx_q', 'idx_k', 'idx_w',
             'mem_q', 'gate']
    offs = dict(zip(names, np.cumsum([0] + widths[:-1]).tolist()))
    wd = dict(zip(names, widths))
    col = lambda n, scale=None: (w_in[:, offs[n]:offs[n] + wd[n]] if scale is None
                                 else w_in[:, offs[n]:offs[n] + wd[n]] * scale)
    big_parts = [('gla_q', dk ** -0.5), ('gla_k', None), ('gla_v', None), ('gla_r', None),
                 ('dsa_q', dh ** -0.5 * float(np.log2(np.e))), ('dsa_k', None), ('dsa_v', None), ('idx_q', None),
                 ('mem_q', dm ** -0.5), ('gate', None)]
    w_big = jnp.concatenate([col(n, s) for n, s in big_parts], axis=1).astype(BF16)
    boff = dict(zip([n for n, _ in big_parts], np.cumsum([0] + [wd[n] for n, _ in big_parts][:-1]).tolist()))
    n_small = wd['glr'] + wd['idx_k'] + wd['idx_w']
    w_small = jnp.concatenate([col('glr'), col('idx_k'), col('idx_w'),
                               jnp.zeros((D, LANES - n_small), w_in.dtype)], axis=1).astype(BF16)

    p_big = _matmul(x2, w_big, BF16, 1024, 512, "in_proj")
    small = _matmul(x2, w_small, F32, 1024, LANES, "in_proj_small")

    o_gla = _gla(p_big, small, w_gate_up, b_gate.reshape(1, -1), norm_g.reshape(1, -1), B, T, dk, dv,
                 _tile(T, 256))

    tq, tk = _tile(T, 256), _tile(T, 512)
    topk = min(DSA_TOPK_MAX, T // 4)
    ik = small[:, wd['glr']:wd['glr'] + idim].astype(BF16)
    ikt = ik.reshape(B, T // tk, tk, idim).transpose(0, 1, 3, 2)
    bias = _dsa_mask(p_big, small, ikt, B, T, tq, tk, boff['idx_q'], wd['glr'] + idim, topk)
    o_dsa = _dsa_attn(p_big, bias, B, T, tq, tk, boff['dsa_q'], DSA_HEADS * dh)

    kv = _matmul(mem2, w_mem_kv.astype(BF16), BF16, 512, 512, "mem_kv")
    o_mem = _mem_attn(p_big, kv, B, T, Mt, dm, boff['mem_q'], _tile(T, 1024))

    merged = _merge(o_gla, o_dsa, o_mem, w_br_gla.astype(BF16), w_br_dsa.astype(BF16), w_br_mem.astype(BF16),
                    p_big, b_merge.reshape(1, -1), D, boff['gate'], _tile(M, 1024), 512)
    x1 = _proj_ln(merged, w_o.astype(BF16), x2, ln1_g.reshape(1, -1), ln1_b.reshape(1, -1), alpha, _tile(M, 256))
    return _mlp(x1, w_up.astype(BF16), b_up.reshape(1, -1), w_down.astype(BF16), b_down.reshape(1, -1),
                ln2_g.reshape(1, -1), ln2_b.reshape(1, -1), alpha, _tile(M, 512), 512)


def kernel(x, mem, w_in, w_gla_gate_up, b_gla_gate, gla_norm_g, w_mem_kv, w_br_gla, w_br_dsa, w_br_mem,
           b_merge, w_o, ln1_g, ln1_b, w_up, b_up, w_down, b_down, ln2_g, ln2_b):
    B, T, D = x.shape
    depth = w_in.shape[0]
    alpha = (2 * depth) ** 0.25
    x2 = x.reshape(B * T, D)
    mem2 = mem.reshape(-1, D)
    for l in range(depth):
        x2 = _layer(x2, mem2, B, T, w_in[l], w_gla_gate_up[l], b_gla_gate[l], gla_norm_g[l], w_mem_kv[l],
                    w_br_gla[l], w_br_dsa[l], w_br_mem[l], b_merge[l], w_o[l], ln1_g[l], ln1_b[l],
                    w_up[l], b_up[l], w_down[l], b_down[l], ln2_g[l], ln2_b[l], alpha)
    return x2.reshape(B, T, D)
```

```python
import functools

import numpy as np
import jax
import jax.numpy as jnp
from jax import lax
from jax.experimental import pallas as pl
from jax.experimental.pallas import tpu as pltpu

F32 = jnp.float32
BF16 = jnp.bfloat16

GLA_HEADS = 4
GLA_GATE_RANK = 16
GLA_GATE_TEMP = 16.0
GLA_CHUNK = 64
DSA_HEADS = 8
IDX_HEADS = 16
DSA_TOPK_MAX = 256
MEM_HEADS = 4
N_BRANCH = 3
LN_EPS = 1e-5
RMS_EPS = 1e-6

V7X_VMEM_LIMIT_BYTES = 56 * 1024 * 1024
LANES = 128
MASK_NEG = -1e30
CAND_PER_GROUP = 7
ALIBI_SPLIT = 3


def _cparams(sem):
    return pltpu.CompilerParams(dimension_semantics=sem, vmem_limit_bytes=V7X_VMEM_LIMIT_BYTES)


def _mm_cast_kernel(x_ref, w_ref, o_ref, xb_ref):
    @pl.when(pl.program_id(1) == 0)
    def _():
        xb_ref[...] = x_ref[...].astype(BF16)

    o_ref[...] = jnp.dot(xb_ref[...], w_ref[...], preferred_element_type=F32).astype(o_ref.dtype)


def _matmul(x, w, out_dtype, tm, tn, name):
    M, K = x.shape
    N = w.shape[1]
    tm, tn = min(tm, M), min(tn, N)
    assert M % tm == 0 and N % tn == 0
    return pl.pallas_call(
        _mm_cast_kernel,
        out_shape=jax.ShapeDtypeStruct((M, N), out_dtype),
        grid=(M // tm, N // tn),
        in_specs=[pl.BlockSpec((tm, K), lambda i, j: (i, 0)),
                  pl.BlockSpec((K, tn), lambda i, j: (0, j))],
        out_specs=pl.BlockSpec((tm, tn), lambda i, j: (i, j)),
        scratch_shapes=[pltpu.VMEM((tm, K), BF16)],
        compiler_params=_cparams(("parallel", "arbitrary")),
        name=name,
    )(x, w)


def _log_sigmoid(z):
    return jnp.minimum(z, 0.0) - jnp.log(1.0 + jnp.exp(-jnp.abs(z)))


def _gla_kernel(q_ref, k_ref, v_ref, r_ref, sm_ref, wg_ref, bg_ref, ng_ref, o_ref, st_ref, *, chunk, rank):
    @pl.when(pl.program_id(2) == 0)
    def _():
        st_ref[...] = jnp.zeros_like(st_ref)

    tb, dk = q_ref.shape
    C = chunk
    glr = sm_ref[:, 0:rank]
    z = jnp.dot(glr, wg_ref[...], preferred_element_type=F32) + bg_ref[...]
    la = _log_sigmoid(z) * (1.0 / GLA_GATE_TEMP)
    ri = lax.broadcasted_iota(jnp.int32, (tb, tb), 0)
    ci = lax.broadcasted_iota(jnp.int32, (tb, tb), 1)
    tri = jnp.where((ci <= ri) & ((ri // C) == (ci // C)), 1.0, 0.0).astype(F32)
    b = jnp.dot(tri, la, preferred_element_type=F32, precision=lax.Precision.HIGHEST)
    causal = (lax.broadcasted_iota(jnp.int32, (C, C), 1) <= lax.broadcasted_iota(jnp.int32, (C, C), 0))

    for c in range(tb // C):
        rows = slice(c * C, (c + 1) * C)
        bc = b[rows]
        br = bc[C // 2:C // 2 + 1]
        bl = bc[C - 1:C]
        qc = q_ref[rows, :].astype(F32)
        kc = k_ref[rows, :].astype(F32)
        vc = v_ref[rows, :]
        q_in = (qc * jnp.exp(bc - br)).astype(BF16)
        k_in = (kc * jnp.exp(br - bc)).astype(BF16)
        sc = lax.dot_general(q_in, k_in, (((1,), (1,)), ((), ())), preferred_element_type=F32)
        sc = jnp.where(causal, sc, 0.0)
        o = jnp.dot(sc.astype(BF16), vc, preferred_element_type=F32)
        q_it = (qc * jnp.exp(bc)).astype(BF16)
        st = st_ref[...]
        o = o + lax.dot_general(q_it, st.astype(BF16), (((1,), (1,)), ((), ())), preferred_element_type=F32)
        k_st = (kc * jnp.exp(bl - bc)).astype(BF16)
        upd = lax.dot_general(vc, k_st, (((0,), (0,)), ((), ())), preferred_element_type=F32)
        st_ref[...] = st * jnp.exp(bl) + upd
        o = o * lax.rsqrt(jnp.mean(o * o, axis=-1, keepdims=True) + RMS_EPS)
        r = r_ref[rows, :].astype(F32)
        o = o * ng_ref[...] * (r / (1.0 + jnp.exp(-r)))
        o_ref[rows, :] = o.astype(o_ref.dtype)


def _gla(p_big, small, w_gate_up, b_gate, norm_g, B, T, dk, dv, tb):
    M = B * T
    H = GLA_HEADS
    nb = T // tb
    row = lambda b, h, n: b * nb + n
    kern = functools.partial(_gla_kernel, chunk=GLA_CHUNK, rank=GLA_GATE_RANK)
    koff = (H * dk) // dk
    voff = (2 * H * dk) // dv
    roff = (2 * H * dk + H * dv) // dv
    return pl.pallas_call(
        kern,
        out_shape=jax.ShapeDtypeStruct((M, H * dv), BF16),
        grid=(B, H, nb),
        in_specs=[
            pl.BlockSpec((tb, dk), lambda b, h, n: (row(b, h, n), h)),
            pl.BlockSpec((tb, dk), lambda b, h, n: (row(b, h, n), koff + h)),
            pl.BlockSpec((tb, dv), lambda b, h, n: (row(b, h, n), voff + h)),
            pl.BlockSpec((tb, dv), lambda b, h, n: (row(b, h, n), roff + h)),
            pl.BlockSpec((tb, LANES), lambda b, h, n: (row(b, h, n), 0)),
            pl.BlockSpec((GLA_GATE_RANK, dk), lambda b, h, n: (0, h)),
            pl.BlockSpec((1, dk), lambda b, h, n: (0, h)),
            pl.BlockSpec((1, dv), lambda b, h, n: (0, h)),
        ],
        out_specs=pl.BlockSpec((tb, dv), lambda b, h, n: (row(b, h, n), h)),
        scratch_shapes=[pltpu.VMEM((dv, dk), F32)],
        compiler_params=_cparams(("parallel", "parallel", "arbitrary")),
        name="gla",
    )(p_big, p_big, p_big, p_big, small, w_gate_up, b_gate, norm_g)


def _idx_kernel(iq_ref, sm_ref, ikt_ref, bias_ref, sc_ref, iqs_ref, wb_ref, cand_ref, thr_ref, *, topk, w_off, idim):
    i = pl.program_id(1)
    nT, tq, tk = sc_ref.shape
    G = iqs_ref.shape[0]
    n_lane = tk // LANES
    n_cand = cand_ref.shape[0]
    R = n_cand // n_lane
    ntile = ((i + 1) * tq + tk - 1) // tk
    kf = float(topk)
    RB = min(tq, 128)
    n_rb = tq // RB

    w = sm_ref[:, w_off:w_off + G] * ((G ** -0.5) * (idim ** -0.5))
    for g in range(G):
        iqs_ref[g] = iq_ref[:, g * idim:(g + 1) * idim]
        wb_ref[g] = jnp.broadcast_to(w[:, g:g + 1], (tq, LANES))
    row_pos = i * tq + lax.broadcasted_iota(jnp.int32, (tq, 1), 0)
    cand_ref[...] = jnp.full(cand_ref.shape, -jnp.inf, F32)

    def score_tile(j, carry):
        ik = ikt_ref[j]
        acc = jnp.zeros((tq, tk), F32)
        for g in range(G):
            a = jnp.dot(iqs_ref[g], ik, preferred_element_type=F32)
            wg = jnp.concatenate([wb_ref[g]] * n_lane, axis=1)
            acc = acc + wg * jnp.maximum(a, 0.0)
        col_pos = j * tk + lax.broadcasted_iota(jnp.int32, (1, tk), 1)
        sc = jnp.where(col_pos <= row_pos, acc, -jnp.inf)
        sc_ref[j] = sc
        for c in range(n_lane):
            x = sc[:, c * LANES:(c + 1) * LANES]
            for r in range(R):
                cur = cand_ref[c * R + r]
                cand_ref[c * R + r] = jnp.maximum(cur, x)
                x = jnp.minimum(cur, x)
        return carry

    lax.fori_loop(0, ntile, score_tile, 0)

    def to_key(v):
        bits = pltpu.bitcast(v, jnp.int32)
        return bits ^ ((bits >> 31) & 0x7FFFFFFF)

    def from_key(k):
        return pltpu.bitcast(k ^ ((k >> 31) & 0x7FFFFFFF), F32)

    gmax, gmin, last_kept = cand_ref[0], cand_ref[0], cand_ref[R - 1]
    for c in range(1, n_lane):
        gmax = jnp.maximum(gmax, cand_ref[c * R])
        gmin = jnp.minimum(gmin, cand_ref[c * R])
        last_kept = jnp.maximum(last_kept, cand_ref[c * R + R - 1])
    hi0 = to_key(jnp.max(gmax, axis=1, keepdims=True))
    lo0 = to_key(jnp.min(gmin, axis=1, keepdims=True))
    last_kept = jnp.max(last_kept, axis=1, keepdims=True)
    few = row_pos < topk
    all_finite = to_key(jnp.full((tq, 1), -jnp.inf, F32)) + 1
    lo0 = jnp.where(few, all_finite, lo0)
    hi0 = jnp.where(few, all_finite, hi0)

    def count_ge(thr, strict=False):
        outs = []
        for r in range(n_rb):
            rows = slice(r * RB, (r + 1) * RB)
            tb = jnp.broadcast_to(thr[rows], (RB, LANES))

            def body(j, cnt, rows=rows, tb=tb):
                for c in range(n_lane):
                    st = sc_ref[j, rows, c * LANES:(c + 1) * LANES]
                    cnt = cnt + jnp.where((st > tb) if strict else (st >= tb), 1.0, 0.0)
                return cnt

            cnt = lax.fori_loop(0, ntile, body, jnp.zeros((RB, LANES), F32))
            outs.append(jnp.sum(cnt, axis=1, keepdims=True))
        return jnp.concatenate(outs, axis=0)

    def cand_count_ge(thr):
        outs = []
        for r in range(n_rb):
            rows = slice(r * RB, (r + 1) * RB)
            tb = jnp.broadcast_to(thr[rows], (RB, LANES))
            cnt = jnp.zeros((RB, LANES), F32)
            for q in range(n_cand):
                cnt = cnt + jnp.where(cand_ref[q, rows, :] >= tb, 1.0, 0.0)
            outs.append(jnp.sum(cnt, axis=1, keepdims=True))
        return jnp.concatenate(outs, axis=0)

    def cand_min_ge(thr):
        outs = []
        for r in range(n_rb):
            rows = slice(r * RB, (r + 1) * RB)
            tb = jnp.broadcast_to(thr[rows], (RB, LANES))
            cur = jnp.full((RB, LANES), jnp.inf, F32)
            for q in range(n_cand):
                ck = cand_ref[q, rows, :]
                cur = jnp.minimum(cur, jnp.where(ck >= tb, ck, jnp.inf))
            outs.append(jnp.min(cur, axis=1, keepdims=True))
        return jnp.concatenate(outs, axis=0)

    def n_open(lo, hi):
        return jnp.sum(jnp.where(lo < hi, 1.0, 0.0)).astype(jnp.int32)

    def search(count):
        def bisect(state):
            lo, hi, exact, _ = state
            x = lo ^ hi
            mid = (lo & hi) + (x >> 1) + (x & 1)
            cnt = count(from_key(mid))
            ge, eq = cnt >= kf, cnt == kf
            lo = jnp.where(ge, mid, lo)
            hi = jnp.where(eq, mid, jnp.where(ge, hi, mid - 1))
            return lo, hi, jnp.where(eq, 1, exact), n_open(lo, hi)

        lo, _, exact, _ = lax.while_loop(lambda s: s[3] > 0, bisect,
                                         (lo0, hi0, jnp.where(few, 1, 0), n_open(lo0, hi0)))
        return from_key(lo), exact

    thr, exact = search(cand_count_ge)
    thr = jnp.where(few, thr, cand_min_ge(thr))
    thr_ref[0] = thr
    thr_ref[1] = exact.astype(F32)
    n_unsure = jnp.sum(jnp.where((last_kept >= thr) & jnp.logical_not(few), 1.0, 0.0)).astype(jnp.int32)

    @pl.when(n_unsure > 0)
    def _():
        thr_full, exact_full = search(count_ge)
        thr_ref[0] = thr_full
        thr_ref[1] = exact_full.astype(F32)

    thr = thr_ref[0]
    n_tied = jnp.sum(jnp.where(thr_ref[1] > 0.0, 0.0, 1.0)).astype(jnp.int32)

    @pl.when(n_tied == 0)
    def _():
        thr_b = jnp.broadcast_to(thr, (tq, LANES))

        def write_tile(j, carry):
            st = sc_ref[j]
            parts = [jnp.where(st[:, c * LANES:(c + 1) * LANES] >= thr_b, 0.0, MASK_NEG) for c in range(n_lane)]
            bias_ref[j] = jnp.concatenate(parts, axis=1).astype(bias_ref.dtype)
            return carry

        lax.fori_loop(0, ntile, write_tile, 0)

    @pl.when(n_tied > 0)
    def _():
        need = kf - count_ge(thr, strict=True)
        before = (lax.broadcasted_iota(jnp.int32, (tk, tk), 0) < lax.broadcasted_iota(jnp.int32, (tk, tk), 1))
        before = jnp.where(before, 1.0, 0.0).astype(BF16)

        def write_tile(j, run):
            st = sc_ref[j]
            tied = jnp.where(st == thr, 1.0, 0.0)
            rank = jnp.dot(tied.astype(BF16), before, preferred_element_type=F32) + run
            sel = (st > thr) | ((st == thr) & (rank < need))
            bias_ref[j] = jnp.where(sel, 0.0, MASK_NEG).astype(bias_ref.dtype)
            return run + jnp.sum(tied, axis=1, keepdims=True)

        lax.fori_loop(0, ntile, write_tile, jnp.zeros((tq, 1), F32))

    def fill_tile(j, carry):
        bias_ref[j] = jnp.full((tq, tk), MASK_NEG, bias_ref.dtype)
        return carry

    lax.fori_loop(ntile, nT, fill_tile, 0)


def _dsa_mask(p_big, small, ikt, B, T, tq, tk, iq_off, w_off, topk):
    G = IDX_HEADS
    idim = ikt.shape[2]
    nQ, nT = T // tq, T // tk
    kern = functools.partial(_idx_kernel, topk=topk, w_off=w_off, idim=idim)
    return pl.pallas_call(
        kern,
        out_shape=jax.ShapeDtypeStruct((B, nQ, nT, tq, tk), BF16),
        grid=(B, nQ),
        in_specs=[
            pl.BlockSpec((tq, G * idim), lambda b, i: (b * nQ + i, iq_off // (G * idim))),
            pl.BlockSpec((tq, LANES), lambda b, i: (b * nQ + i, 0)),
            pl.BlockSpec((None, nT, idim, tk), lambda b, i: (b, 0, 0, 0)),
        ],
        out_specs=pl.BlockSpec((None, None, nT, tq, tk), lambda b, i: (b, i, 0, 0, 0)),
        scratch_shapes=[pltpu.VMEM((nT, tq, tk), F32),
                        pltpu.VMEM((G, tq, idim), BF16),
                        pltpu.VMEM((G, tq, LANES), F32),
                        pltpu.VMEM((tk // LANES * CAND_PER_GROUP, tq, LANES), F32),
                        pltpu.VMEM((2, tq, 1), F32)],
        compiler_params=_cparams(("parallel", "arbitrary")),
        name="dsa_index_topk",
    )(p_big, small, ikt)


def _dsa_attn_kernel(qi_ref, kj_ref, q_ref, k_ref, v_ref, bias_ref, sl_ref, o_ref, m_ref, l_ref, acc_ref, *, heads):
    p = pl.program_id(1)
    qi, kj = qi_ref[p], kj_ref[p]
    tq, tk = bias_ref.shape
    d = q_ref.shape[1] // heads
    n_lane = tk // LANES

    @pl.when(kj == 0)
    def _():
        m_ref[...] = jnp.full_like(m_ref, MASK_NEG)
        l_ref[...] = jnp.zeros_like(l_ref)
        acc_ref[...] = jnp.zeros_like(acc_ref)

    rel = (kj * tk - qi * tq) + lax.broadcasted_iota(jnp.int32, (tk, LANES), 0)
    lane = lax.broadcasted_iota(jnp.int32, (tk, LANES), 1)
    pos = jnp.where(lane < 2 * ALIBI_SPLIT, jnp.where(lane % 2 == 0, rel >> 8, rel & 255), 0)
    pos = pos.astype(F32).astype(BF16)
    bias = bias_ref[...].astype(F32)
    ones_col = jnp.where(lane == 0, 1.0, 0.0).astype(BF16)

    def masked_logits(h):
        cols = slice(h * d, (h + 1) * d)
        qa = jnp.concatenate([q_ref[:, cols], sl_ref[h]], axis=1)
        ka = jnp.concatenate([k_ref[:, cols], pos], axis=1)
        s = lax.dot_general(qa, ka, (((1,), (1,)), ((), ())), preferred_element_type=F32)
        return [s[:, c * LANES:(c + 1) * LANES] + bias[:, c * LANES:(c + 1) * LANES] for c in range(n_lane)]

    sb_next = masked_logits(0)
    for h in range(heads):
        cols = slice(h * d, (h + 1) * d)
        sb = sb_next
        if h + 1 < heads:
            sb_next = masked_logits(h + 1)
        mx = sb[0]
        for c in range(1, n_lane):
            mx = jnp.maximum(mx, sb[c])
        m_old = m_ref[h]
        m_new = jnp.maximum(m_old, jnp.max(mx, axis=1, keepdims=True))
        corr = jnp.exp2(m_old - m_new)
        ps = [jnp.exp2((sb[c] - m_new).astype(BF16)) for c in range(n_lane)]
        va = jnp.concatenate([v_ref[:, cols], ones_col], axis=1)
        pv = jnp.dot(jnp.concatenate(ps, axis=1), va, preferred_element_type=F32)
        l_ref[h] = corr * l_ref[h] + pv[:, d:]
        acc_ref[:, cols] = corr * acc_ref[:, cols] + pv[:, :d]
        m_ref[h] = m_new

    @pl.when((kj + 1) * tk >= (qi + 1) * tq)
    def _():
        for h in range(heads):
            cols = slice(h * d, (h + 1) * d)
            l = jnp.sum(l_ref[h], axis=1, keepdims=True)
            o_ref[:, cols] = (acc_ref[:, cols] / l).astype(o_ref.dtype)


def _alibi_columns(heads, tq):
    out = np.zeros((heads, LANES), np.float32)
    for h in range(heads):
        rest = np.float64(2.0 ** (-8.0 * (h + 1) / heads) * np.log2(np.e))
        for i in range(ALIBI_SPLIT):
            piece = np.float64(np.float32(rest).astype(BF16))
            out[h, 2 * i], out[h, 2 * i + 1] = 256.0 * piece, piece
            rest = rest - piece
    return jnp.asarray(np.broadcast_to(out.astype(BF16)[:, None, :], (heads, tq, LANES)))


def _dsa_attn(p_big, bias, B, T, tq, tk, q_off, width):
    nQ, nT = T // tq, T // tk
    pairs = [(i, j) for i in range(nQ) for j in range(((i + 1) * tq + tk - 1) // tk)]
    qi_tbl = jnp.asarray(np.array([p[0] for p in pairs], np.int32))
    kj_tbl = jnp.asarray(np.array([p[1] for p in pairs], np.int32))
    qb = q_off // width
    heads = DSA_HEADS
    kern = functools.partial(_dsa_attn_kernel, heads=heads)
    grid_spec = pltpu.PrefetchScalarGridSpec(
        num_scalar_prefetch=2,
        grid=(B, len(pairs)),
        in_specs=[
            pl.BlockSpec((tq, width), lambda b, p, qi, kj: (b * nQ + qi[p], qb)),
            pl.BlockSpec((tk, width), lambda b, p, qi, kj: (b * nT + kj[p], qb + 1)),
            pl.BlockSpec((tk, width), lambda b, p, qi, kj: (b * nT + kj[p], qb + 2)),
            pl.BlockSpec((None, None, None, tq, tk), lambda b, p, qi, kj: (b, qi[p], kj[p], 0, 0)),
            pl.BlockSpec((heads, tq, LANES), lambda b, p, qi, kj: (0, 0, 0)),
        ],
        out_specs=pl.BlockSpec((tq, width), lambda b, p, qi, kj: (b * nQ + qi[p], 0)),
        scratch_shapes=[pltpu.VMEM((heads, tq, LANES), F32),
                        pltpu.VMEM((heads, tq, LANES), F32),
                        pltpu.VMEM((tq, width), F32)],
    )
    return pl.pallas_call(
        kern,
        out_shape=jax.ShapeDtypeStruct((B * T, width), BF16),
        grid_spec=grid_spec,
        compiler_params=_cparams(("arbitrary", "arbitrary")),
        name="dsa_attention",
    )(qi_tbl, kj_tbl, p_big, p_big, p_big, bias, _alibi_columns(heads, tq))


def _mem_attn_kernel(q_ref, k_ref, v_ref, o_ref):
    s = lax.dot_general(q_ref[...], k_ref[...], (((1,), (1,)), ((), ())), preferred_element_type=F32)
    m = jnp.max(s, axis=1, keepdims=True)
    p = jnp.exp(s - m)
    l = jnp.sum(p, axis=1, keepdims=True)
    o = jnp.dot(p.astype(BF16), v_ref[...], preferred_element_type=F32)
    o_ref[...] = (o / l).astype(o_ref.dtype)


def _mem_attn(p_big, kv, B, T, Mt, dm, q_off, tq):
    Hm = MEM_HEADS
    nq = T // tq
    qb = q_off // dm
    return pl.pallas_call(
        _mem_attn_kernel,
        out_shape=jax.ShapeDtypeStruct((B * T, Hm * dm), BF16),
        grid=(B, nq, Hm),
        in_specs=[
            pl.BlockSpec((tq, dm), lambda b, i, h: (b * nq + i, qb + h)),
            pl.BlockSpec((Mt, dm), lambda b, i, h: (b, h)),
            pl.BlockSpec((Mt, dm), lambda b, i, h: (b, Hm + h)),
        ],
        out_specs=pl.BlockSpec((tq, dm), lambda b, i, h: (b * nq + i, h)),
        compiler_params=_cparams(("parallel", "parallel", "arbitrary")),
        name="memory_attention",
    )(p_big, kv, kv)


def _merge_kernel(a_ref, d_ref, m_ref, wa_ref, wd_ref, wm_ref, g0_ref, g1_ref, g2_ref,
                  b0_ref, b1_ref, b2_ref, o_ref):
    def gate(g_ref, b_ref):
        return jax.nn.sigmoid(g_ref[...].astype(F32) + b_ref[...])

    acc = gate(g0_ref, b0_ref) * jnp.dot(a_ref[...], wa_ref[...], preferred_element_type=F32)
    acc = acc + gate(g1_ref, b1_ref) * jnp.dot(d_ref[...], wd_ref[...], preferred_element_type=F32)
    acc = acc + gate(g2_ref, b2_ref) * jnp.dot(m_ref[...], wm_ref[...], preferred_element_type=F32)
    o_ref[...] = acc.astype(o_ref.dtype)


def _merge(o_gla, o_dsa, o_mem, wa, wd, wm, p_big, b_merge, D, gate_off, tm, tn):
    M = o_gla.shape[0]
    gb = gate_off // tn
    nd = D // tn
    row = lambda w: pl.BlockSpec((tm, w), lambda i, j: (i, 0))
    col = lambda k: pl.BlockSpec((k, tn), lambda i, j: (0, j))
    gspec = lambda r: pl.BlockSpec((tm, tn), lambda i, j: (i, gb + r * nd + j))
    bspec = lambda r: pl.BlockSpec((1, tn), lambda i, j: (0, r * nd + j))
    return pl.pallas_call(
        _merge_kernel,
        out_shape=jax.ShapeDtypeStruct((M, D), BF16),
        grid=(M // tm, D // tn),
        in_specs=[row(o_gla.shape[1]), row(o_dsa.shape[1]), row(o_mem.shape[1]),
                  col(wa.shape[0]), col(wd.shape[0]), col(wm.shape[0]),
                  gspec(0), gspec(1), gspec(2), bspec(0), bspec(1), bspec(2)],
        out_specs=pl.BlockSpec((tm, tn), lambda i, j: (i, j)),
        compiler_params=_cparams(("parallel", "arbitrary")),
        name="gated_merge",
    )(o_gla, o_dsa, o_mem, wa, wd, wm, p_big, p_big, p_big, b_merge, b_merge, b_merge)


def _layer_norm(y, g, b):
    mu = jnp.mean(y, axis=-1, keepdims=True)
    yc = y - mu
    var = jnp.mean(yc * yc, axis=-1, keepdims=True)
    return yc * lax.rsqrt(var + LN_EPS) * g + b


def _proj_ln_kernel(mg_ref, wo_ref, x_ref, g_ref, b_ref, o_ref, *, alpha):
    y = jnp.dot(mg_ref[...], wo_ref[...], preferred_element_type=F32)
    o_ref[...] = _layer_norm(alpha * x_ref[...] + y, g_ref[...], b_ref[...])


def _proj_ln(merged, w_o, x2, g, b, alpha, tm):
    M, D = x2.shape
    vec = pl.BlockSpec((1, D), lambda i: (0, 0))
    return pl.pallas_call(
        functools.partial(_proj_ln_kernel, alpha=alpha),
        out_shape=jax.ShapeDtypeStruct((M, D), F32),
        grid=(M // tm,),
        in_specs=[pl.BlockSpec((tm, D), lambda i: (i, 0)),
                  pl.BlockSpec((D, D), lambda i: (0, 0)),
                  pl.BlockSpec((tm, D), lambda i: (i, 0)), vec, vec],
        out_specs=pl.BlockSpec((tm, D), lambda i: (i, 0)),
        compiler_params=_cparams(("parallel",)),
        name="out_proj_ln",
    )(merged, w_o, x2, g, b)


def _mlp_kernel(x_ref, wu_ref, bu_ref, wd_ref, bd_ref, g_ref, b_ref, o_ref, xb_ref, acc_ref, *, alpha):
    f = pl.program_id(1)

    @pl.when(f == 0)
    def _():
        xb_ref[...] = x_ref[...].astype(BF16)
        acc_ref[...] = jnp.zeros_like(acc_ref)

    h = jnp.dot(xb_ref[...], wu_ref[...], preferred_element_type=F32) + bu_ref[...]
    h = jnp.square(jnp.maximum(h, 0.0))
    acc_ref[...] += jnp.dot(h.astype(BF16), wd_ref[...], preferred_element_type=F32)

    @pl.when(f == pl.num_programs(1) - 1)
    def _():
        y = alpha * x_ref[...] + acc_ref[...] + bd_ref[...]
        o_ref[...] = _layer_norm(y, g_ref[...], b_ref[...])


def _mlp(x1, w_up, b_up, w_down, b_down, g, b, alpha, tm, tf):
    M, D = x1.shape
    F = w_up.shape[1]
    vec = pl.BlockSpec((1, D), lambda i, f: (0, 0))
    return pl.pallas_call(
        functools.partial(_mlp_kernel, alpha=alpha),
        out_shape=jax.ShapeDtypeStruct((M, D), F32),
        grid=(M // tm, F // tf),
        in_specs=[pl.BlockSpec((tm, D), lambda i, f: (i, 0)),
                  pl.BlockSpec((D, tf), lambda i, f: (0, f)),
                  pl.BlockSpec((1, tf), lambda i, f: (0, f)),
                  pl.BlockSpec((tf, D), lambda i, f: (f, 0)),
                  vec, vec, vec],
        out_specs=pl.BlockSpec((tm, D), lambda i, f: (i, 0)),
        scratch_shapes=[pltpu.VMEM((tm, D), BF16), pltpu.VMEM((tm, D), F32)],
        compiler_params=_cparams(("parallel", "arbitrary")),
        name="mlp_ln",
    )(x1, w_up, b_up, w_down, b_down, g, b)


def _tile(n, pref):
    t = min(n, pref)
    assert n % t == 0
    return t


def _layer(x2, mem2, B, T, w_in, w_gate_up, b_gate, norm_g, w_mem_kv, w_br_gla, w_br_dsa, w_br_mem,
           b_merge, w_o, ln1_g, ln1_b, w_up, b_up, w_down, b_down, ln2_g, ln2_b, alpha):
    M, D = x2.shape
    Mt = mem2.shape[0] // B
    dk, dv = D // 8, D // 4
    dh = D // 16
    idim = D // 32
    dm = D // 8
    H, G = GLA_HEADS, IDX_HEADS
    widths = [H * dk, H * dk, H * dv, H * dv, GLA_GATE_RANK, DSA_HEADS * dh, DSA_HEADS * dh, DSA_HEADS * dh,
              G * idim, idim, G, MEM_HEADS * dm, N_BRANCH * D]
    names = ['gla_q', 'gla_k', 'gla_v', 'gla_r', 'glr', 'dsa_q', 'dsa_k', 'dsa_v', 'idx_q', 'idx_k', 'idx_w',
             'mem_q', 'gate']
    offs = dict(zip(names, np.cumsum([0] + widths[:-1]).tolist()))
    wd = dict(zip(names, widths))
    col = lambda n, scale=None: (w_in[:, offs[n]:offs[n] + wd[n]] if scale is None
                                 else w_in[:, offs[n]:offs[n] + wd[n]] * scale)
    big_parts = [('gla_q', dk ** -0.5), ('gla_k', None), ('gla_v', None), ('gla_r', None),
                 ('dsa_q', dh ** -0.5 * float(np.log2(np.e))), ('dsa_k', None), ('dsa_v', None), ('idx_q', None),
                 ('mem_q', dm ** -0.5), ('gate', None)]
    w_big = jnp.concatenate([col(n, s) for n, s in big_parts], axis=1).astype(BF16)
    boff = dict(zip([n for n, _ in big_parts], np.cumsum([0] + [wd[n] for n, _ in big_parts][:-1]).tolist()))
    n_small = wd['glr'] + wd['idx_k'] + wd['idx_w']
    w_small = jnp.concatenate([col('glr'), col('idx_k'), col('idx_w'),
                               jnp.zeros((D, LANES - n_small), w_in.dtype)], axis=1).astype(BF16)

    p_big = _matmul(x2, w_big, BF16, 1024, 512, "in_proj")
    small = _matmul(x2, w_small, F32, 1024, LANES, "in_proj_small")

    o_gla = _gla(p_big, small, w_gate_up, b_gate.reshape(1, -1), norm_g.reshape(1, -1), B, T, dk, dv,
                 _tile(T, 256))

    tq, tk = _tile(T, 256), _tile(T, 512)
    topk = min(DSA_TOPK_MAX, T // 4)
    ik = small[:, wd['glr']:wd['glr'] + idim].astype(BF16)
    ikt = ik.reshape(B, T // tk, tk, idim).transpose(0, 1, 3, 2)
    bias = _dsa_mask(p_big, small, ikt, B, T, tq, tk, boff['idx_q'], wd['glr'] + idim, topk)
    o_dsa = _dsa_attn(p_big, bias, B, T, tq, tk, boff['dsa_q'], DSA_HEADS * dh)

    kv = _matmul(mem2, w_mem_kv.astype(BF16), BF16, 512, 512, "mem_kv")
    o_mem = _mem_attn(p_big, kv, B, T, Mt, dm, boff['mem_q'], _tile(T, 1024))

    merged = _merge(o_gla, o_dsa, o_mem, w_br_gla.astype(BF16), w_br_dsa.astype(BF16), w_br_mem.astype(BF16),
                    p_big, b_merge.reshape(1, -1), D, boff['gate'], _tile(M, 1024), 512)
    x1 = _proj_ln(merged, w_o.astype(BF16), x2, ln1_g.reshape(1, -1), ln1_b.reshape(1, -1), alpha, _tile(M, 256))
    return _mlp(x1, w_up.astype(BF16), b_up.reshape(1, -1), w_down.astype(BF16), b_down.reshape(1, -1),
                ln2_g.reshape(1, -1), ln2_b.reshape(1, -1), alpha, _tile(M, 512), 512)


def kernel(x, mem, w_in, w_gla_gate_up, b_gla_gate, gla_norm_g, w_mem_kv, w_br_gla, w_br_dsa, w_br_mem,
           b_merge, w_o, ln1_g, ln1_b, w_up, b_up, w_down, b_down, ln2_g, ln2_b):
    B, T, D = x.shape
    depth = w_in.shape[0]
    alpha = (2 * depth) ** 0.25
    x2 = x.reshape(B * T, D)
    mem2 = mem.reshape(-1, D)
    for l in range(depth):
        x2 = _layer(x2, mem2, B, T, w_in[l], w_gla_gate_up[l], b_gla_gate[l], gla_norm_g[l], w_mem_kv[l],
                    w_br_gla[l], w_br_dsa[l], w_br_mem[l], b_merge[l], w_o[l], ln1_g[l], ln1_b[l],
                    w_up[l], b_up[l], w_down[l], b_down[l], ln2_g[l], ln2_b[l], alpha)
    return x2.reshape(B, T, D)
```

```python
import functools

import numpy as np
import jax
import jax.numpy as jnp
from jax import lax
from jax.experimental import pallas as pl
from jax.experimental.pallas import tpu as pltpu

F32 = jnp.float32
BF16 = jnp.bfloat16

GLA_HEADS = 4
GLA_GATE_RANK = 16
GLA_GATE_TEMP = 16.0
GLA_CHUNK = 64
DSA_HEADS = 8
IDX_HEADS = 16
DSA_TOPK_MAX = 256
MEM_HEADS = 4
N_BRANCH = 3
LN_EPS = 1e-5
RMS_EPS = 1e-6

V7X_VMEM_LIMIT_BYTES = 56 * 1024 * 1024
LANES = 128
MASK_NEG = -1e30
CAND_PER_GROUP = 7
ALIBI_SPLIT = 3


def _cparams(sem):
    return pltpu.CompilerParams(dimension_semantics=sem, vmem_limit_bytes=V7X_VMEM_LIMIT_BYTES)


def _mm_cast_kernel(x_ref, w_ref, o_ref, xb_ref):
    @pl.when(pl.program_id(1) == 0)
    def _():
        xb_ref[...] = x_ref[...].astype(BF16)

    o_ref[...] = jnp.dot(xb_ref[...], w_ref[...], preferred_element_type=F32).astype(o_ref.dtype)


def _matmul(x, w, out_dtype, tm, tn, name):
    M, K = x.shape
    N = w.shape[1]
    tm, tn = min(tm, M), min(tn, N)
    assert M % tm == 0 and N % tn == 0
    return pl.pallas_call(
        _mm_cast_kernel,
        out_shape=jax.ShapeDtypeStruct((M, N), out_dtype),
        grid=(M // tm, N // tn),
        in_specs=[pl.BlockSpec((tm, K), lambda i, j: (i, 0)),
                  pl.BlockSpec((K, tn), lambda i, j: (0, j))],
        out_specs=pl.BlockSpec((tm, tn), lambda i, j: (i, j)),
        scratch_shapes=[pltpu.VMEM((tm, K), BF16)],
        compiler_params=_cparams(("parallel", "arbitrary")),
        name=name,
    )(x, w)


def _in_proj_kernel(x_ref, w_ref, ws_ref, o_ref, os_ref, xb_ref):
    @pl.when(pl.program_id(1) == 0)
    def _():
        xb_ref[...] = x_ref[...].astype(BF16)
        os_ref[...] = jnp.dot(xb_ref[...], ws_ref[...], preferred_element_type=F32)

    o_ref[...] = jnp.dot(xb_ref[...], w_ref[...], preferred_element_type=F32).astype(o_ref.dtype)


def _in_proj(x, w_big, w_small, tm, tn):
    M, K = x.shape
    N, Ns = w_big.shape[1], w_small.shape[1]
    tm, tn = min(tm, M), min(tn, N)
    assert M % tm == 0 and N % tn == 0
    return pl.pallas_call(
        _in_proj_kernel,
        out_shape=(jax.ShapeDtypeStruct((M, N), BF16), jax.ShapeDtypeStruct((M, Ns), F32)),
        grid=(M // tm, N // tn),
        in_specs=[pl.BlockSpec((tm, K), lambda i, j: (i, 0)),
                  pl.BlockSpec((K, tn), lambda i, j: (0, j)),
                  pl.BlockSpec((K, Ns), lambda i, j: (0, 0))],
        out_specs=(pl.BlockSpec((tm, tn), lambda i, j: (i, j)),
                   pl.BlockSpec((tm, Ns), lambda i, j: (i, 0))),
        scratch_shapes=[pltpu.VMEM((tm, K), BF16)],
        compiler_params=_cparams(("parallel", "arbitrary")),
        name="in_proj",
    )(x, w_big, w_small)


def _log_sigmoid(z):
    return jnp.minimum(z, 0.0) - jnp.log(1.0 + jnp.exp(-jnp.abs(z)))


def _gla_kernel(q_ref, k_ref, v_ref, r_ref, sm_ref, wg_ref, bg_ref, ng_ref, o_ref, st_ref, *, chunk, rank):
    @pl.when(pl.program_id(2) == 0)
    def _():
        st_ref[...] = jnp.zeros_like(st_ref)

    tb, dk = q_ref.shape
    C = chunk
    glr = sm_ref[:, 0:rank]
    z = jnp.dot(glr, wg_ref[...], preferred_element_type=F32) + bg_ref[...]
    la = _log_sigmoid(z) * (1.0 / GLA_GATE_TEMP)
    ri = lax.broadcasted_iota(jnp.int32, (tb, tb), 0)
    ci = lax.broadcasted_iota(jnp.int32, (tb, tb), 1)
    tri = jnp.where((ci <= ri) & ((ri // C) == (ci // C)), 1.0, 0.0).astype(F32)
    b = jnp.dot(tri, la, preferred_element_type=F32, precision=lax.Precision.HIGHEST)
    causal = (lax.broadcasted_iota(jnp.int32, (C, C), 1) <= lax.broadcasted_iota(jnp.int32, (C, C), 0))

    for c in range(tb // C):
        rows = slice(c * C, (c + 1) * C)
        bc = b[rows]
        br = bc[C // 2:C // 2 + 1]
        bl = bc[C - 1:C]
        qc = q_ref[rows, :].astype(F32)
        kc = k_ref[rows, :].astype(F32)
        vc = v_ref[rows, :]
        q_in = (qc * jnp.exp(bc - br)).astype(BF16)
        k_in = (kc * jnp.exp(br - bc)).astype(BF16)
        sc = lax.dot_general(q_in, k_in, (((1,), (1,)), ((), ())), preferred_element_type=F32)
        sc = jnp.where(causal, sc, 0.0)
        o = jnp.dot(sc.astype(BF16), vc, preferred_element_type=F32)
        q_it = (qc * jnp.exp(bc)).astype(BF16)
        st = st_ref[...]
        o = o + lax.dot_general(q_it, st.astype(BF16), (((1,), (1,)), ((), ())), preferred_element_type=F32)
        k_st = (kc * jnp.exp(bl - bc)).astype(BF16)
        upd = lax.dot_general(vc, k_st, (((0,), (0,)), ((), ())), preferred_element_type=F32)
        st_ref[...] = st * jnp.exp(bl) + upd
        o = o * lax.rsqrt(jnp.mean(o * o, axis=-1, keepdims=True) + RMS_EPS)
        r = r_ref[rows, :].astype(F32)
        o = o * ng_ref[...] * (r / (1.0 + jnp.exp(-r)))
        o_ref[rows, :] = o.astype(o_ref.dtype)


def _gla(p_big, small, w_gate_up, b_gate, norm_g, B, T, dk, dv, tb):
    M = B * T
    H = GLA_HEADS
    nb = T // tb
    row = lambda b, h, n: b * nb + n
    kern = functools.partial(_gla_kernel, chunk=GLA_CHUNK, rank=GLA_GATE_RANK)
    koff = (H * dk) // dk
    voff = (2 * H * dk) // dv
    roff = (2 * H * dk + H * dv) // dv
    return pl.pallas_call(
        kern,
        out_shape=jax.ShapeDtypeStruct((M, H * dv), BF16),
        grid=(B, H, nb),
        in_specs=[
            pl.BlockSpec((tb, dk), lambda b, h, n: (row(b, h, n), h)),
            pl.BlockSpec((tb, dk), lambda b, h, n: (row(b, h, n), koff + h)),
            pl.BlockSpec((tb, dv), lambda b, h, n: (row(b, h, n), voff + h)),
            pl.BlockSpec((tb, dv), lambda b, h, n: (row(b, h, n), roff + h)),
            pl.BlockSpec((tb, LANES), lambda b, h, n: (row(b, h, n), 0)),
            pl.BlockSpec((GLA_GATE_RANK, dk), lambda b, h, n: (0, h)),
            pl.BlockSpec((1, dk), lambda b, h, n: (0, h)),
            pl.BlockSpec((1, dv), lambda b, h, n: (0, h)),
        ],
        out_specs=pl.BlockSpec((tb, dv), lambda b, h, n: (row(b, h, n), h)),
        scratch_shapes=[pltpu.VMEM((dv, dk), F32)],
        compiler_params=_cparams(("parallel", "parallel", "arbitrary")),
        name="gla",
    )(p_big, p_big, p_big, p_big, small, w_gate_up, b_gate, norm_g)


def _idx_kernel(iq_ref, sm_ref, ikt_ref, bias_ref, sc_ref, iqs_ref, wb_ref, cand_ref, thr_ref, *, topk, w_off, idim):
    i = pl.program_id(1)
    nT, tq, tk = sc_ref.shape
    G = iqs_ref.shape[0]
    n_lane = tk // LANES
    n_cand = cand_ref.shape[0]
    R = n_cand // n_lane
    ntile = ((i + 1) * tq + tk - 1) // tk
    kf = float(topk)
    RB = min(tq, 128)
    n_rb = tq // RB

    w = sm_ref[:, w_off:w_off + G] * ((G ** -0.5) * (idim ** -0.5))
    for g in range(G):
        iqs_ref[g] = iq_ref[:, g * idim:(g + 1) * idim]
        wb_ref[g] = jnp.broadcast_to(w[:, g:g + 1], (tq, LANES))
    row_pos = i * tq + lax.broadcasted_iota(jnp.int32, (tq, 1), 0)
    cand_ref[...] = jnp.full(cand_ref.shape, -jnp.inf, F32)

    def score_tile(j, carry):
        ik = ikt_ref[j]
        acc = jnp.zeros((tq, tk), F32)
        for g in range(G):
            a = jnp.dot(iqs_ref[g], ik, preferred_element_type=F32)
            wg = jnp.concatenate([wb_ref[g]] * n_lane, axis=1)
            acc = acc + wg * jnp.maximum(a, 0.0)
        col_pos = j * tk + lax.broadcasted_iota(jnp.int32, (1, tk), 1)
        sc = jnp.where(col_pos <= row_pos, acc, -jnp.inf)
        sc_ref[j] = sc
        for c in range(n_lane):
            x = sc[:, c * LANES:(c + 1) * LANES]
            for r in range(R):
                cur = cand_ref[c * R + r]
                cand_ref[c * R + r] = jnp.maximum(cur, x)
                x = jnp.minimum(cur, x)
        return carry

    lax.fori_loop(0, ntile, score_tile, 0)

    def to_key(v):
        bits = pltpu.bitcast(v, jnp.int32)
        return bits ^ ((bits >> 31) & 0x7FFFFFFF)

    def from_key(k):
        return pltpu.bitcast(k ^ ((k >> 31) & 0x7FFFFFFF), F32)

    gmax, gmin, last_kept = cand_ref[0], cand_ref[0], cand_ref[R - 1]
    for c in range(1, n_lane):
        gmax = jnp.maximum(gmax, cand_ref[c * R])
        gmin = jnp.minimum(gmin, cand_ref[c * R])
        last_kept = jnp.maximum(last_kept, cand_ref[c * R + R - 1])
    hi0 = to_key(jnp.max(gmax, axis=1, keepdims=True))
    lo0 = to_key(jnp.min(gmin, axis=1, keepdims=True))
    last_kept = jnp.max(last_kept, axis=1, keepdims=True)
    few = row_pos < topk
    all_finite = to_key(jnp.full((tq, 1), -jnp.inf, F32)) + 1
    lo0 = jnp.where(few, all_finite, lo0)
    hi0 = jnp.where(few, all_finite, hi0)

    def count_ge(thr, strict=False):
        outs = []
        for r in range(n_rb):
            rows = slice(r * RB, (r + 1) * RB)
            tb = jnp.broadcast_to(thr[rows], (RB, LANES))

            def body(j, cnt, rows=rows, tb=tb):
                for c in range(n_lane):
                    st = sc_ref[j, rows, c * LANES:(c + 1) * LANES]
                    cnt = cnt + jnp.where((st > tb) if strict else (st >= tb), 1.0, 0.0)
                return cnt

            cnt = lax.fori_loop(0, ntile, body, jnp.zeros((RB, LANES), F32))
            outs.append(jnp.sum(cnt, axis=1, keepdims=True))
        return jnp.concatenate(outs, axis=0)

    def cand_count_ge(thr):
        outs = []
        for r in range(n_rb):
            rows = slice(r * RB, (r + 1) * RB)
            tb = jnp.broadcast_to(thr[rows], (RB, LANES))
            cnt = jnp.zeros((RB, LANES), F32)
            for q in range(n_cand):
                cnt = cnt + jnp.where(cand_ref[q, rows, :] >= tb, 1.0, 0.0)
            outs.append(jnp.sum(cnt, axis=1, keepdims=True))
        return jnp.concatenate(outs, axis=0)

    def cand_min_ge(thr):
        outs = []
        for r in range(n_rb):
            rows = slice(r * RB, (r + 1) * RB)
            tb = jnp.broadcast_to(thr[rows], (RB, LANES))
            cur = jnp.full((RB, LANES), jnp.inf, F32)
            for q in range(n_cand):
                ck = cand_ref[q, rows, :]
                cur = jnp.minimum(cur, jnp.where(ck >= tb, ck, jnp.inf))
            outs.append(jnp.min(cur, axis=1, keepdims=True))
        return jnp.concatenate(outs, axis=0)

    def n_open(lo, hi):
        return jnp.sum(jnp.where(lo < hi, 1.0, 0.0)).astype(jnp.int32)

    def search(count):
        def bisect(state):
            lo, hi, exact, _ = state
            x = lo ^ hi
            mid = (lo & hi) + (x >> 1) + (x & 1)
            cnt = count(from_key(mid))
            ge, eq = cnt >= kf, cnt == kf
            lo = jnp.where(ge, mid, lo)
            hi = jnp.where(eq, mid, jnp.where(ge, hi, mid - 1))
            return lo, hi, jnp.where(eq, 1, exact), n_open(lo, hi)

        lo, _, exact, _ = lax.while_loop(lambda s: s[3] > 0, bisect,
                                         (lo0, hi0, jnp.where(few, 1, 0), n_open(lo0, hi0)))
        return from_key(lo), exact

    thr, exact = search(cand_count_ge)
    thr = jnp.where(few, thr, cand_min_ge(thr))
    thr_ref[0] = thr
    thr_ref[1] = exact.astype(F32)
    n_unsure = jnp.sum(jnp.where((last_kept >= thr) & jnp.logical_not(few), 1.0, 0.0)).astype(jnp.int32)

    @pl.when(n_unsure > 0)
    def _():
        thr_full, exact_full = search(count_ge)
        thr_ref[0] = thr_full
        thr_ref[1] = exact_full.astype(F32)

    thr = thr_ref[0]
    n_tied = jnp.sum(jnp.where(thr_ref[1] > 0.0, 0.0, 1.0)).astype(jnp.int32)

    @pl.when(n_tied == 0)
    def _():
        thr_b = jnp.broadcast_to(thr, (tq, LANES))

        def write_tile(j, carry):
            st = sc_ref[j]
            parts = [jnp.where(st[:, c * LANES:(c + 1) * LANES] >= thr_b, 0.0, MASK_NEG) for c in range(n_lane)]
            bias_ref[j] = jnp.concatenate(parts, axis=1).astype(bias_ref.dtype)
            return carry

        lax.fori_loop(0, ntile, write_tile, 0)

    @pl.when(n_tied > 0)
    def _():
        need = kf - count_ge(thr, strict=True)
        before = (lax.broadcasted_iota(jnp.int32, (tk, tk), 0) < lax.broadcasted_iota(jnp.int32, (tk, tk), 1))
        before = jnp.where(before, 1.0, 0.0).astype(BF16)

        def write_tile(j, run):
            st = sc_ref[j]
            tied = jnp.where(st == thr, 1.0, 0.0)
            rank = jnp.dot(tied.astype(BF16), before, preferred_element_type=F32) + run
            sel = (st > thr) | ((st == thr) & (rank < need))
            bias_ref[j] = jnp.where(sel, 0.0, MASK_NEG).astype(bias_ref.dtype)
            return run + jnp.sum(tied, axis=1, keepdims=True)

        lax.fori_loop(0, ntile, write_tile, jnp.zeros((tq, 1), F32))

    def fill_tile(j, carry):
        bias_ref[j] = jnp.full((tq, tk), MASK_NEG, bias_ref.dtype)
        return carry

    lax.fori_loop(ntile, nT, fill_tile, 0)


def _dsa_mask(p_big, small, ikt, B, T, tq, tk, iq_off, w_off, topk):
    G = IDX_HEADS
    idim = ikt.shape[2]
    nQ, nT = T // tq, T // tk
    kern = functools.partial(_idx_kernel, topk=topk, w_off=w_off, idim=idim)
    return pl.pallas_call(
        kern,
        out_shape=jax.ShapeDtypeStruct((B, nQ, nT, tq, tk), BF16),
        grid=(B, nQ),
        in_specs=[
            pl.BlockSpec((tq, G * idim), lambda b, i: (b * nQ + i, iq_off // (G * idim))),
            pl.BlockSpec((tq, LANES), lambda b, i: (b * nQ + i, 0)),
            pl.BlockSpec((None, nT, idim, tk), lambda b, i: (b, 0, 0, 0)),
        ],
        out_specs=pl.BlockSpec((None, None, nT, tq, tk), lambda b, i: (b, i, 0, 0, 0)),
        scratch_shapes=[pltpu.VMEM((nT, tq, tk), F32),
                        pltpu.VMEM((G, tq, idim), BF16),
                        pltpu.VMEM((G, tq, LANES), F32),
                        pltpu.VMEM((tk // LANES * CAND_PER_GROUP, tq, LANES), F32),
                        pltpu.VMEM((2, tq, 1), F32)],
        compiler_params=_cparams(("parallel", "arbitrary")),
        name="dsa_index_topk",
    )(p_big, small, ikt)


def _dsa_attn_kernel(qi_ref, kj_ref, q_ref, k_ref, v_ref, bias_ref, sl_ref, o_ref, m_ref, l_ref, acc_ref, *, heads):
    p = pl.program_id(1)
    qi, kj = qi_ref[p], kj_ref[p]
    tq, tk = bias_ref.shape
    d = q_ref.shape[1] // heads
    n_lane = tk // LANES

    @pl.when(kj == 0)
    def _():
        m_ref[...] = jnp.full_like(m_ref, MASK_NEG)
        l_ref[...] = jnp.zeros_like(l_ref)
        acc_ref[...] = jnp.zeros_like(acc_ref)

    rel = (kj * tk - qi * tq) + lax.broadcasted_iota(jnp.int32, (tk, LANES), 0)
    lane = lax.broadcasted_iota(jnp.int32, (tk, LANES), 1)
    pos = jnp.where(lane < 2 * ALIBI_SPLIT, jnp.where(lane % 2 == 0, rel >> 8, rel & 255), 0)
    pos = pos.astype(F32).astype(BF16)
    bias = bias_ref[...].astype(F32)
    ones_col = jnp.where(lane == 0, 1.0, 0.0).astype(BF16)

    def masked_logits(h):
        cols = slice(h * d, (h + 1) * d)
        qa = jnp.concatenate([q_ref[:, cols], sl_ref[h]], axis=1)
        ka = jnp.concatenate([k_ref[:, cols], pos], axis=1)
        s = lax.dot_general(qa, ka, (((1,), (1,)), ((), ())), preferred_element_type=F32)
        return [s[:, c * LANES:(c + 1) * LANES] + bias[:, c * LANES:(c + 1) * LANES] for c in range(n_lane)]

    sb_next = masked_logits(0)
    for h in range(heads):
        cols = slice(h * d, (h + 1) * d)
        sb = sb_next
        if h + 1 < heads:
            sb_next = masked_logits(h + 1)
        mx = sb[0]
        for c in range(1, n_lane):
            mx = jnp.maximum(mx, sb[c])
        m_old = m_ref[h]
        m_new = jnp.maximum(m_old, jnp.max(mx, axis=1, keepdims=True))
        corr = jnp.exp2(m_old - m_new)
        ps = [jnp.exp2((sb[c] - m_new).astype(BF16)) for c in range(n_lane)]
        va = jnp.concatenate([v_ref[:, cols], ones_col], axis=1)
        pv = jnp.dot(jnp.concatenate(ps, axis=1), va, preferred_element_type=F32)
        l_ref[h] = corr * l_ref[h] + pv[:, d:]
        acc_ref[:, cols] = corr * acc_ref[:, cols] + pv[:, :d]
        m_ref[h] = m_new

    @pl.when((kj + 1) * tk >= (qi + 1) * tq)
    def _():
        for h in range(heads):
            cols = slice(h * d, (h + 1) * d)
            l = jnp.sum(l_ref[h], axis=1, keepdims=True)
            o_ref[:, cols] = (acc_ref[:, cols] / l).astype(o_ref.dtype)


def _alibi_columns(heads, tq):
    out = np.zeros((heads, LANES), np.float32)
    for h in range(heads):
        rest = np.float64(2.0 ** (-8.0 * (h + 1) / heads) * np.log2(np.e))
        for i in range(ALIBI_SPLIT):
            piece = np.float64(np.float32(rest).astype(BF16))
            out[h, 2 * i], out[h, 2 * i + 1] = 256.0 * piece, piece
            rest = rest - piece
    return jnp.asarray(np.broadcast_to(out.astype(BF16)[:, None, :], (heads, tq, LANES)))


def _dsa_attn(p_big, bias, B, T, tq, tk, q_off, width):
    nQ, nT = T // tq, T // tk
    pairs = [(i, j) for i in range(nQ) for j in range(((i + 1) * tq + tk - 1) // tk)]
    qi_tbl = jnp.asarray(np.array([p[0] for p in pairs], np.int32))
    kj_tbl = jnp.asarray(np.array([p[1] for p in pairs], np.int32))
    qb = q_off // width
    heads = DSA_HEADS
    kern = functools.partial(_dsa_attn_kernel, heads=heads)
    grid_spec = pltpu.PrefetchScalarGridSpec(
        num_scalar_prefetch=2,
        grid=(B, len(pairs)),
        in_specs=[
            pl.BlockSpec((tq, width), lambda b, p, qi, kj: (b * nQ + qi[p], qb)),
            pl.BlockSpec((tk, width), lambda b, p, qi, kj: (b * nT + kj[p], qb + 1)),
            pl.BlockSpec((tk, width), lambda b, p, qi, kj: (b * nT + kj[p], qb + 2)),
            pl.BlockSpec((None, None, None, tq, tk), lambda b, p, qi, kj: (b, qi[p], kj[p], 0, 0)),
            pl.BlockSpec((heads, tq, LANES), lambda b, p, qi, kj: (0, 0, 0)),
        ],
        out_specs=pl.BlockSpec((tq, width), lambda b, p, qi, kj: (b * nQ + qi[p], 0)),
        scratch_shapes=[pltpu.VMEM((heads, tq, LANES), F32),
                        pltpu.VMEM((heads, tq, LANES), F32),
                        pltpu.VMEM((tq, width), F32)],
    )
    return pl.pallas_call(
        kern,
        out_shape=jax.ShapeDtypeStruct((B * T, width), BF16),
        grid_spec=grid_spec,
        compiler_params=_cparams(("arbitrary", "arbitrary")),
        name="dsa_attention",
    )(qi_tbl, kj_tbl, p_big, p_big, p_big, bias, _alibi_columns(heads, tq))


def _mem_attn_kernel(q_ref, k_ref, v_ref, o_ref):
    s = lax.dot_general(q_ref[...], k_ref[...], (((1,), (1,)), ((), ())), preferred_element_type=F32)
    m = jnp.max(s, axis=1, keepdims=True)
    p = jnp.exp(s - m)
    l = jnp.sum(p, axis=1, keepdims=True)
    o = jnp.dot(p.astype(BF16), v_ref[...], preferred_element_type=F32)
    o_ref[...] = (o / l).astype(o_ref.dtype)


def _mem_attn(p_big, kv, B, T, Mt, dm, q_off, tq):
    Hm = MEM_HEADS
    nq = T // tq
    qb = q_off // dm
    return pl.pallas_call(
        _mem_attn_kernel,
        out_shape=jax.ShapeDtypeStruct((B * T, Hm * dm), BF16),
        grid=(B, nq, Hm),
        in_specs=[
            pl.BlockSpec((tq, dm), lambda b, i, h: (b * nq + i, qb + h)),
            pl.BlockSpec((Mt, dm), lambda b, i, h: (b, h)),
            pl.BlockSpec((Mt, dm), lambda b, i, h: (b, Hm + h)),
        ],
        out_specs=pl.BlockSpec((tq, dm), lambda b, i, h: (b * nq + i, h)),
        compiler_params=_cparams(("parallel", "parallel", "arbitrary")),
        name="memory_attention",
    )(p_big, kv, kv)


def _merge_kernel(a_ref, d_ref, m_ref, wa_ref, wd_ref, wm_ref, g0_ref, g1_ref, g2_ref,
                  b0_ref, b1_ref, b2_ref, o_ref):
    def gate(g_ref, b_ref):
        return jax.nn.sigmoid(g_ref[...].astype(F32) + b_ref[...])

    acc = gate(g0_ref, b0_ref) * jnp.dot(a_ref[...], wa_ref[...], preferred_element_type=F32)
    acc = acc + gate(g1_ref, b1_ref) * jnp.dot(d_ref[...], wd_ref[...], preferred_element_type=F32)
    acc = acc + gate(g2_ref, b2_ref) * jnp.dot(m_ref[...], wm_ref[...], preferred_element_type=F32)
    o_ref[...] = acc.astype(o_ref.dtype)


def _merge(o_gla, o_dsa, o_mem, wa, wd, wm, p_big, b_merge, D, gate_off, tm, tn):
    M = o_gla.shape[0]
    gb = gate_off // tn
    nd = D // tn
    row = lambda w: pl.BlockSpec((tm, w), lambda i, j: (i, 0))
    col = lambda k: pl.BlockSpec((k, tn), lambda i, j: (0, j))
    gspec = lambda r: pl.BlockSpec((tm, tn), lambda i, j: (i, gb + r * nd + j))
    bspec = lambda r: pl.BlockSpec((1, tn), lambda i, j: (0, r * nd + j))
    return pl.pallas_call(
        _merge_kernel,
        out_shape=jax.ShapeDtypeStruct((M, D), BF16),
        grid=(M // tm, D // tn),
        in_specs=[row(o_gla.shape[1]), row(o_dsa.shape[1]), row(o_mem.shape[1]),
                  col(wa.shape[0]), col(wd.shape[0]), col(wm.shape[0]),
                  gspec(0), gspec(1), gspec(2), bspec(0), bspec(1), bspec(2)],
        out_specs=pl.BlockSpec((tm, tn), lambda i, j: (i, j)),
        compiler_params=_cparams(("parallel", "arbitrary")),
        name="gated_merge",
    )(o_gla, o_dsa, o_mem, wa, wd, wm, p_big, p_big, p_big, b_merge, b_merge, b_merge)


def _layer_norm(y, g, b):
    mu = jnp.mean(y, axis=-1, keepdims=True)
    yc = y - mu
    var = jnp.mean(yc * yc, axis=-1, keepdims=True)
    return yc * lax.rsqrt(var + LN_EPS) * g + b


def _proj_ln_kernel(mg_ref, wo_ref, x_ref, g_ref, b_ref, o_ref, *, alpha):
    y = jnp.dot(mg_ref[...], wo_ref[...], preferred_element_type=F32)
    o_ref[...] = _layer_norm(alpha * x_ref[...] + y, g_ref[...], b_ref[...])


def _proj_ln(merged, w_o, x2, g, b, alpha, tm):
    M, D = x2.shape
    vec = pl.BlockSpec((1, D), lambda i: (0, 0))
    return pl.pallas_call(
        functools.partial(_proj_ln_kernel, alpha=alpha),
        out_shape=jax.ShapeDtypeStruct((M, D), F32),
        grid=(M // tm,),
        in_specs=[pl.BlockSpec((tm, D), lambda i: (i, 0)),
                  pl.BlockSpec((D, D), lambda i: (0, 0)),
                  pl.BlockSpec((tm, D), lambda i: (i, 0)), vec, vec],
        out_specs=pl.BlockSpec((tm, D), lambda i: (i, 0)),
        compiler_params=_cparams(("parallel",)),
        name="out_proj_ln",
    )(merged, w_o, x2, g, b)


def _mlp_kernel(x_ref, wu_ref, bu_ref, wd_ref, bd_ref, g_ref, b_ref, o_ref, xb_ref, acc_ref, *, alpha):
    f = pl.program_id(1)

    @pl.when(f == 0)
    def _():
        xb_ref[...] = x_ref[...].astype(BF16)
        acc_ref[...] = jnp.zeros_like(acc_ref)

    h = jnp.dot(xb_ref[...], wu_ref[...], preferred_element_type=F32) + bu_ref[...]
    h = jnp.square(jnp.maximum(h, 0.0))
    acc_ref[...] += jnp.dot(h.astype(BF16), wd_ref[...], preferred_element_type=F32)

    @pl.when(f == pl.num_programs(1) - 1)
    def _():
        y = alpha * x_ref[...] + acc_ref[...] + bd_ref[...]
        o_ref[...] = _layer_norm(y, g_ref[...], b_ref[...])


def _mlp(x1, w_up, b_up, w_down, b_down, g, b, alpha, tm, tf):
    M, D = x1.shape
    F = w_up.shape[1]
    vec = pl.BlockSpec((1, D), lambda i, f: (0, 0))
    return pl.pallas_call(
        functools.partial(_mlp_kernel, alpha=alpha),
        out_shape=jax.ShapeDtypeStruct((M, D), F32),
        grid=(M // tm, F // tf),
        in_specs=[pl.BlockSpec((tm, D), lambda i, f: (i, 0)),
                  pl.BlockSpec((D, tf), lambda i, f: (0, f)),
                  pl.BlockSpec((1, tf), lambda i, f: (0, f)),
                  pl.BlockSpec((tf, D), lambda i, f: (f, 0)),
                  vec, vec, vec],
        out_specs=pl.BlockSpec((tm, D), lambda i, f: (i, 0)),
        scratch_shapes=[pltpu.VMEM((tm, D), BF16), pltpu.VMEM((tm, D), F32)],
        compiler_params=_cparams(("parallel", "arbitrary")),
        name="mlp_ln",
    )(x1, w_up, b_up, w_down, b_down, g, b)


def _tile(n, pref):
    t = min(n, pref)
    assert n % t == 0
    return t


def _layer(x2, mem2, B, T, w_in, w_gate_up, b_gate, norm_g, w_mem_kv, w_br_gla, w_br_dsa, w_br_mem,
           b_merge, w_o, ln1_g, ln1_b, w_up, b_up, w_down, b_down, ln2_g, ln2_b, alpha):
    M, D = x2.shape
    Mt = mem2.shape[0] // B
    dk, dv = D // 8, D // 4
    dh = D // 16
    idim = D // 32
    dm = D // 8
    H, G = GLA_HEADS, IDX_HEADS
    widths = [H * dk, H * dk, H * dv, H * dv, GLA_GATE_RANK, DSA_HEADS * dh, DSA_HEADS * dh, DSA_HEADS * dh,
              G * idim, idim, G, MEM_HEADS * dm, N_BRANCH * D]
    names = ['gla_q', 'gla_k', 'gla_v', 'gla_r', 'glr', 'dsa_q', 'dsa_k', 'dsa_v', 'idx_q', 'idx_k', 'idx_w',
             'mem_q', 'gate']
    offs = dict(zip(names, np.cumsum([0] + widths[:-1]).tolist()))
    wd = dict(zip(names, widths))
    col = lambda n, scale=None: (w_in[:, offs[n]:offs[n] + wd[n]] if scale is None
                                 else w_in[:, offs[n]:offs[n] + wd[n]] * scale)
    big_parts = [('gla_q', dk ** -0.5), ('gla_k', None), ('gla_v', None), ('gla_r', None),
                 ('dsa_q', dh ** -0.5 * float(np.log2(np.e))), ('dsa_k', None), ('dsa_v', None), ('idx_q', None),
                 ('mem_q', dm ** -0.5), ('gate', None)]
    w_big = jnp.concatenate([col(n, s) for n, s in big_parts], axis=1).astype(BF16)
    boff = dict(zip([n for n, _ in big_parts], np.cumsum([0] + [wd[n] for n, _ in big_parts][:-1]).tolist()))
    n_small = wd['glr'] + wd['idx_k'] + wd['idx_w']
    w_small = jnp.concatenate([col('glr'), col('idx_k'), col('idx_w'),
                               jnp.zeros((D, LANES - n_small), w_in.dtype)], axis=1).astype(BF16)

    p_big, small = _in_proj(x2, w_big, w_small, 1024, 1024)

    o_gla = _gla(p_big, small, w_gate_up, b_gate.reshape(1, -1), norm_g.reshape(1, -1), B, T, dk, dv,
                 _tile(T, 256))

    tq, tk = _tile(T, 256), _tile(T, 512)
    topk = min(DSA_TOPK_MAX, T // 4)
    ik = small[:, wd['glr']:wd['glr'] + idim].astype(BF16)
    ikt = ik.reshape(B, T // tk, tk, idim).transpose(0, 1, 3, 2)
    bias = _dsa_mask(p_big, small, ikt, B, T, tq, tk, boff['idx_q'], wd['glr'] + idim, topk)
    o_dsa = _dsa_attn(p_big, bias, B, T, tq, tk, boff['dsa_q'], DSA_HEADS * dh)

    kv = _matmul(mem2, w_mem_kv.astype(BF16), BF16, 512, 512, "mem_kv")
    o_mem = _mem_attn(p_big, kv, B, T, Mt, dm, boff['mem_q'], _tile(T, 1024))

    merged = _merge(o_gla, o_dsa, o_mem, w_br_gla.astype(BF16), w_br_dsa.astype(BF16), w_br_mem.astype(BF16),
                    p_big, b_merge.reshape(1, -1), D, boff['gate'], _tile(M, 1024), 512)
    x1 = _proj_ln(merged, w_o.astype(BF16), x2, ln1_g.reshape(1, -1), ln1_b.reshape(1, -1), alpha, _tile(M, 256))
    return _mlp(x1, w_up.astype(BF16), b_up.reshape(1, -1), w_down.astype(BF16), b_down.reshape(1, -1),
                ln2_g.reshape(1, -1), ln2_b.reshape(1, -1), alpha, _tile(M, 512), 1024)


def kernel(x, mem, w_in, w_gla_gate_up, b_gla_gate, gla_norm_g, w_mem_kv, w_br_gla, w_br_dsa, w_br_mem,
           b_merge, w_o, ln1_g, ln1_b, w_up, b_up, w_down, b_down, ln2_g, ln2_b):
    B, T, D = x.shape
    depth = w_in.shape[0]
    alpha = (2 * depth) ** 0.25
    x2 = x.reshape(B * T, D)
    mem2 = mem.reshape(-1, D)
    for l in range(depth):
        x2 = _layer(x2, mem2, B, T, w_in[l], w_gla_gate_up[l], b_gla_gate[l], gla_norm_g[l], w_mem_kv[l],
                    w_br_gla[l], w_br_dsa[l], w_br_mem[l], b_merge[l], w_o[l], ln1_g[l], ln1_b[l],
                    w_up[l], b_up[l], w_down[l], b_down[l], ln2_g[l], ln2_b[l], alpha)
    return x2.reshape(B, T, D)
```

```python
import functools

import numpy as np
import jax
import jax.numpy as jnp
from jax import lax
from jax.experimental import pallas as pl
from jax.experimental.pallas import tpu as pltpu

F32 = jnp.float32
BF16 = jnp.bfloat16

GLA_HEADS = 4
GLA_GATE_RANK = 16
GLA_GATE_TEMP = 16.0
GLA_CHUNK = 64
DSA_HEADS = 8
IDX_HEADS = 16
DSA_TOPK_MAX = 256
MEM_HEADS = 4
N_BRANCH = 3
LN_EPS = 1e-5
RMS_EPS = 1e-6

V7X_VMEM_LIMIT_BYTES = 56 * 1024 * 1024
LANES = 128
MASK_NEG = -1e30
CAND_PER_GROUP = 7
ALIBI_SPLIT = 3


def _cparams(sem):
    return pltpu.CompilerParams(dimension_semantics=sem, vmem_limit_bytes=V7X_VMEM_LIMIT_BYTES)


def _mm_cast_kernel(x_ref, w_ref, o_ref, xb_ref):
    @pl.when(pl.program_id(1) == 0)
    def _():
        xb_ref[...] = x_ref[...].astype(BF16)

    o_ref[...] = jnp.dot(xb_ref[...], w_ref[...], preferred_element_type=F32).astype(o_ref.dtype)


def _matmul(x, w, out_dtype, tm, tn, name):
    M, K = x.shape
    N = w.shape[1]
    tm, tn = min(tm, M), min(tn, N)
    assert M % tm == 0 and N % tn == 0
    return pl.pallas_call(
        _mm_cast_kernel,
        out_shape=jax.ShapeDtypeStruct((M, N), out_dtype),
        grid=(M // tm, N // tn),
        in_specs=[pl.BlockSpec((tm, K), lambda i, j: (i, 0)),
                  pl.BlockSpec((K, tn), lambda i, j: (0, j))],
        out_specs=pl.BlockSpec((tm, tn), lambda i, j: (i, j)),
        scratch_shapes=[pltpu.VMEM((tm, K), BF16)],
        compiler_params=_cparams(("parallel", "arbitrary")),
        name=name,
    )(x, w)


def _in_proj_kernel(x_ref, w_ref, ws_ref, o_ref, os_ref, xb_ref):
    @pl.when(pl.program_id(1) == 0)
    def _():
        xb_ref[...] = x_ref[...].astype(BF16)
        os_ref[...] = jnp.dot(xb_ref[...], ws_ref[...], preferred_element_type=F32)

    o_ref[...] = jnp.dot(xb_ref[...], w_ref[...], preferred_element_type=F32).astype(o_ref.dtype)


def _in_proj(x, w_big, w_small, tm, tn):
    M, K = x.shape
    N, Ns = w_big.shape[1], w_small.shape[1]
    tm, tn = min(tm, M), min(tn, N)
    assert M % tm == 0 and N % tn == 0
    return pl.pallas_call(
        _in_proj_kernel,
        out_shape=(jax.ShapeDtypeStruct((M, N), BF16), jax.ShapeDtypeStruct((M, Ns), F32)),
        grid=(M // tm, N // tn),
        in_specs=[pl.BlockSpec((tm, K), lambda i, j: (i, 0)),
                  pl.BlockSpec((K, tn), lambda i, j: (0, j)),
                  pl.BlockSpec((K, Ns), lambda i, j: (0, 0))],
        out_specs=(pl.BlockSpec((tm, tn), lambda i, j: (i, j)),
                   pl.BlockSpec((tm, Ns), lambda i, j: (i, 0))),
        scratch_shapes=[pltpu.VMEM((tm, K), BF16)],
        compiler_params=_cparams(("parallel", "arbitrary")),
        name="in_proj",
    )(x, w_big, w_small)


def _log_sigmoid(z):
    return jnp.minimum(z, 0.0) - jnp.log(1.0 + jnp.exp(-jnp.abs(z)))


def _gla_kernel(q_ref, k_ref, v_ref, r_ref, sm_ref, wg_ref, bg_ref, ng_ref, o_ref, st_ref, *, chunk, rank):
    @pl.when(pl.program_id(2) == 0)
    def _():
        st_ref[...] = jnp.zeros_like(st_ref)

    tb, dk = q_ref.shape
    C = chunk
    glr = sm_ref[:, 0:rank]
    z = jnp.dot(glr, wg_ref[...], preferred_element_type=F32) + bg_ref[...]
    la = _log_sigmoid(z) * (1.0 / GLA_GATE_TEMP)
    causal = (lax.broadcasted_iota(jnp.int32, (C, C), 1) <= lax.broadcasted_iota(jnp.int32, (C, C), 0))
    tri = jnp.where(causal, 1.0, 0.0).astype(F32)

    n_chunk = tb // C
    chunk_rows = [slice(c * C, (c + 1) * C) for c in range(n_chunk)]
    bcs = [jnp.dot(tri, la[rows], preferred_element_type=F32, precision=lax.Precision.HIGHEST)
           for rows in chunk_rows]
    scs, q_its, upds, decays = [], [], [], []
    for c, rows in enumerate(chunk_rows):
        bc = bcs[c]
        br = bc[C // 2:C // 2 + 1]
        bl = bc[C - 1:C]
        qc = q_ref[rows, :].astype(F32)
        kc = k_ref[rows, :].astype(F32)
        q_in = (qc * jnp.exp(bc - br)).astype(BF16)
        k_in = (kc * jnp.exp(br - bc)).astype(BF16)
        scs.append(lax.dot_general(q_in, k_in, (((1,), (1,)), ((), ())), preferred_element_type=F32))
        q_its.append((qc * jnp.exp(bc)).astype(BF16))
        k_st = (kc * jnp.exp(bl - bc)).astype(BF16)
        upds.append(lax.dot_general(v_ref[rows, :], k_st, (((0,), (0,)), ((), ())),
                                    preferred_element_type=F32))
        decays.append(jnp.exp(bl))
    o_intra = [jnp.dot(jnp.where(causal, scs[c], 0.0).astype(BF16), v_ref[rows, :], preferred_element_type=F32)
               for c, rows in enumerate(chunk_rows)]

    for c, rows in enumerate(chunk_rows):
        st = st_ref[...]
        o = o_intra[c] + lax.dot_general(q_its[c], st.astype(BF16), (((1,), (1,)), ((), ())),
                                         preferred_element_type=F32)
        st_ref[...] = st * decays[c] + upds[c]
        o = o * lax.rsqrt(jnp.mean(o * o, axis=-1, keepdims=True) + RMS_EPS)
        r = r_ref[rows, :].astype(F32)
        o = o * ng_ref[...] * (r / (1.0 + jnp.exp(-r)))
        o_ref[rows, :] = o.astype(o_ref.dtype)


def _gla(p_big, small, w_gate_up, b_gate, norm_g, B, T, dk, dv, tb):
    M = B * T
    H = GLA_HEADS
    nb = T // tb
    row = lambda b, h, n: b * nb + n
    kern = functools.partial(_gla_kernel, chunk=GLA_CHUNK, rank=GLA_GATE_RANK)
    koff = (H * dk) // dk
    voff = (2 * H * dk) // dv
    roff = (2 * H * dk + H * dv) // dv
    return pl.pallas_call(
        kern,
        out_shape=jax.ShapeDtypeStruct((M, H * dv), BF16),
        grid=(B, H, nb),
        in_specs=[
            pl.BlockSpec((tb, dk), lambda b, h, n: (row(b, h, n), h)),
            pl.BlockSpec((tb, dk), lambda b, h, n: (row(b, h, n), koff + h)),
            pl.BlockSpec((tb, dv), lambda b, h, n: (row(b, h, n), voff + h)),
            pl.BlockSpec((tb, dv), lambda b, h, n: (row(b, h, n), roff + h)),
            pl.BlockSpec((tb, LANES), lambda b, h, n: (row(b, h, n), 0)),
            pl.BlockSpec((GLA_GATE_RANK, dk), lambda b, h, n: (0, h)),
            pl.BlockSpec((1, dk), lambda b, h, n: (0, h)),
            pl.BlockSpec((1, dv), lambda b, h, n: (0, h)),
        ],
        out_specs=pl.BlockSpec((tb, dv), lambda b, h, n: (row(b, h, n), h)),
        scratch_shapes=[pltpu.VMEM((dv, dk), F32)],
        compiler_params=_cparams(("parallel", "parallel", "arbitrary")),
        name="gla",
    )(p_big, p_big, p_big, p_big, small, w_gate_up, b_gate, norm_g)


def _idx_kernel(iq_ref, sm_ref, ikt_ref, bias_ref, sc_ref, iqs_ref, wb_ref, cand_ref, thr_ref, *, topk, w_off, idim):
    i = pl.program_id(1)
    nT, tq, tk = sc_ref.shape
    G = iqs_ref.shape[0]
    n_lane = tk // LANES
    n_cand = cand_ref.shape[0]
    R = n_cand // n_lane
    ntile = ((i + 1) * tq + tk - 1) // tk
    kf = float(topk)
    RB = min(tq, 128)
    n_rb = tq // RB

    w = sm_ref[:, w_off:w_off + G] * ((G ** -0.5) * (idim ** -0.5))
    for g in range(G):
        iqs_ref[g] = iq_ref[:, g * idim:(g + 1) * idim]
        wb_ref[g] = jnp.broadcast_to(w[:, g:g + 1], (tq, LANES))
    row_pos = i * tq + lax.broadcasted_iota(jnp.int32, (tq, 1), 0)
    cand_ref[...] = jnp.full(cand_ref.shape, -jnp.inf, F32)

    def score_tile(j, carry):
        ik = ikt_ref[j]
        acc = jnp.zeros((tq, tk), F32)
        for g in range(G):
            a = jnp.dot(iqs_ref[g], ik, preferred_element_type=F32)
            wg = jnp.concatenate([wb_ref[g]] * n_lane, axis=1)
            acc = acc + wg * jnp.maximum(a, 0.0)
        col_pos = j * tk + lax.broadcasted_iota(jnp.int32, (1, tk), 1)
        sc = jnp.where(col_pos <= row_pos, acc, -jnp.inf)
        sc_ref[j] = sc
        for c in range(n_lane):
            x = sc[:, c * LANES:(c + 1) * LANES]
            for r in range(R):
                cur = cand_ref[c * R + r]
                cand_ref[c * R + r] = jnp.maximum(cur, x)
                x = jnp.minimum(cur, x)
        return carry

    lax.fori_loop(0, ntile, score_tile, 0)

    def to_key(v):
        bits = pltpu.bitcast(v, jnp.int32)
        return bits ^ ((bits >> 31) & 0x7FFFFFFF)

    def from_key(k):
        return pltpu.bitcast(k ^ ((k >> 31) & 0x7FFFFFFF), F32)

    gmax, last_kept = cand_ref[0], cand_ref[R - 1]
    for c in range(1, n_lane):
        gmax = jnp.maximum(gmax, cand_ref[c * R])
        last_kept = jnp.maximum(last_kept, cand_ref[c * R + R - 1])
    hi0 = to_key(jnp.max(gmax, axis=1, keepdims=True))
    last_kept = jnp.max(last_kept, axis=1, keepdims=True)
    n_cover = -(-topk // LANES)
    lo_f = None
    for r in range(min(R, n_cover)):
        need = -(-n_cover // (r + 1))
        if need > n_lane:
            continue
        mins = [jnp.min(cand_ref[c * R + r], axis=1, keepdims=True) for c in range(n_lane)]
        for a in range(need):
            for b2 in range(n_lane - 1, a, -1):
                hi_v, lo_v = jnp.maximum(mins[b2 - 1], mins[b2]), jnp.minimum(mins[b2 - 1], mins[b2])
                mins[b2 - 1], mins[b2] = hi_v, lo_v
        lo_f = mins[need - 1] if lo_f is None else jnp.maximum(lo_f, mins[need - 1])
    assert lo_f is not None, "topk must not exceed the number of key groups"
    lo0 = to_key(lo_f)
    few = row_pos < topk
    all_finite = to_key(jnp.full((tq, 1), -jnp.inf, F32)) + 1
    lo0 = jnp.where(few, all_finite, lo0)
    hi0 = jnp.where(few, all_finite, hi0)

    def count_ge(thr, strict=False):
        outs = []
        for r in range(n_rb):
            rows = slice(r * RB, (r + 1) * RB)
            tb = jnp.broadcast_to(thr[rows], (RB, LANES))

            def body(j, cnt, rows=rows, tb=tb):
                for c in range(n_lane):
                    st = sc_ref[j, rows, c * LANES:(c + 1) * LANES]
                    cnt = cnt + jnp.where((st > tb) if strict else (st >= tb), 1.0, 0.0)
                return cnt

            cnt = lax.fori_loop(0, ntile, body, jnp.zeros((RB, LANES), F32))
            outs.append(jnp.sum(cnt, axis=1, keepdims=True))
        return jnp.concatenate(outs, axis=0)

    def cand_count_ge(thr):
        outs = []
        for r in range(n_rb):
            rows = slice(r * RB, (r + 1) * RB)
            tb = jnp.broadcast_to(thr[rows], (RB, LANES))
            cnt = jnp.zeros((RB, LANES), F32)
            for q in range(n_cand):
                cnt = cnt + jnp.where(cand_ref[q, rows, :] >= tb, 1.0, 0.0)
            outs.append(jnp.sum(cnt, axis=1, keepdims=True))
        return jnp.concatenate(outs, axis=0)

    def cand_min_ge(thr):
        outs = []
        for r in range(n_rb):
            rows = slice(r * RB, (r + 1) * RB)
            tb = jnp.broadcast_to(thr[rows], (RB, LANES))
            cur = jnp.full((RB, LANES), jnp.inf, F32)
            for q in range(n_cand):
                ck = cand_ref[q, rows, :]
                cur = jnp.minimum(cur, jnp.where(ck >= tb, ck, jnp.inf))
            outs.append(jnp.min(cur, axis=1, keepdims=True))
        return jnp.concatenate(outs, axis=0)

    def n_open(lo, hi):
        return jnp.sum(jnp.where(lo < hi, 1.0, 0.0)).astype(jnp.int32)

    def search(count):
        def bisect(state):
            lo, hi, exact, _ = state
            x = lo ^ hi
            mid = (lo & hi) + (x >> 1) + (x & 1)
            cnt = count(from_key(mid))
            ge, eq = cnt >= kf, cnt == kf
            lo = jnp.where(ge, mid, lo)
            hi = jnp.where(eq, mid, jnp.where(ge, hi, mid - 1))
            return lo, hi, jnp.where(eq, 1, exact), n_open(lo, hi)

        lo, _, exact, _ = lax.while_loop(lambda s: s[3] > 0, bisect,
                                         (lo0, hi0, jnp.where(few, 1, 0), n_open(lo0, hi0)))
        return from_key(lo), exact

    thr, exact = search(cand_count_ge)
    thr = jnp.where(few, thr, cand_min_ge(thr))
    thr_ref[0] = thr
    thr_ref[1] = exact.astype(F32)
    n_unsure = jnp.sum(jnp.where((last_kept >= thr) & jnp.logical_not(few), 1.0, 0.0)).astype(jnp.int32)

    @pl.when(n_unsure > 0)
    def _():
        thr_full, exact_full = search(count_ge)
        thr_ref[0] = thr_full
        thr_ref[1] = exact_full.astype(F32)

    thr = thr_ref[0]
    n_tied = jnp.sum(jnp.where(thr_ref[1] > 0.0, 0.0, 1.0)).astype(jnp.int32)

    @pl.when(n_tied == 0)
    def _():
        thr_b = jnp.broadcast_to(thr, (tq, LANES))

        def write_tile(j, carry):
            st = sc_ref[j]
            parts = [jnp.where(st[:, c * LANES:(c + 1) * LANES] >= thr_b, 0.0, MASK_NEG) for c in range(n_lane)]
            bias_ref[j] = jnp.concatenate(parts, axis=1).astype(bias_ref.dtype)
            return carry

        lax.fori_loop(0, ntile, write_tile, 0)

    @pl.when(n_tied > 0)
    def _():
        need = kf - count_ge(thr, strict=True)
        before = (lax.broadcasted_iota(jnp.int32, (tk, tk), 0) < lax.broadcasted_iota(jnp.int32, (tk, tk), 1))
        before = jnp.where(before, 1.0, 0.0).astype(BF16)

        def write_tile(j, run):
            st = sc_ref[j]
            tied = jnp.where(st == thr, 1.0, 0.0)
            rank = jnp.dot(tied.astype(BF16), before, preferred_element_type=F32) + run
            sel = (st > thr) | ((st == thr) & (rank < need))
            bias_ref[j] = jnp.where(sel, 0.0, MASK_NEG).astype(bias_ref.dtype)
            return run + jnp.sum(tied, axis=1, keepdims=True)

        lax.fori_loop(0, ntile, write_tile, jnp.zeros((tq, 1), F32))

    def fill_tile(j, carry):
        bias_ref[j] = jnp.full((tq, tk), MASK_NEG, bias_ref.dtype)
        return carry

    lax.fori_loop(ntile, nT, fill_tile, 0)


def _dsa_mask(p_big, small, ikt, B, T, tq, tk, iq_off, w_off, topk):
    G = IDX_HEADS
    idim = ikt.shape[2]
    nQ, nT = T // tq, T // tk
    kern = functools.partial(_idx_kernel, topk=topk, w_off=w_off, idim=idim)
    return pl.pallas_call(
        kern,
        out_shape=jax.ShapeDtypeStruct((B, nQ, nT, tq, tk), BF16),
        grid=(B, nQ),
        in_specs=[
            pl.BlockSpec((tq, G * idim), lambda b, i: (b * nQ + i, iq_off // (G * idim))),
            pl.BlockSpec((tq, LANES), lambda b, i: (b * nQ + i, 0)),
            pl.BlockSpec((None, nT, idim, tk), lambda b, i: (b, 0, 0, 0)),
        ],
        out_specs=pl.BlockSpec((None, None, nT, tq, tk), lambda b, i: (b, i, 0, 0, 0)),
        scratch_shapes=[pltpu.VMEM((nT, tq, tk), F32),
                        pltpu.VMEM((G, tq, idim), BF16),
                        pltpu.VMEM((G, tq, LANES), F32),
                        pltpu.VMEM((tk // LANES * CAND_PER_GROUP, tq, LANES), F32),
                        pltpu.VMEM((2, tq, 1), F32)],
        compiler_params=_cparams(("parallel", "arbitrary")),
        name="dsa_index_topk",
    )(p_big, small, ikt)


def _dsa_attn_kernel(qi_ref, kj_ref, q_ref, k_ref, v_ref, bias_ref, sl_ref, o_ref, m_ref, l_ref, acc_ref, *, heads):
    p = pl.program_id(1)
    qi, kj = qi_ref[p], kj_ref[p]
    tq, tk = bias_ref.shape
    d = q_ref.shape[1] // heads
    n_lane = tk // LANES

    @pl.when(kj == 0)
    def _():
        m_ref[...] = jnp.full_like(m_ref, MASK_NEG)
        l_ref[...] = jnp.zeros_like(l_ref)
        acc_ref[...] = jnp.zeros_like(acc_ref)

    rel = (kj * tk - qi * tq) + lax.broadcasted_iota(jnp.int32, (tk, LANES), 0)
    lane = lax.broadcasted_iota(jnp.int32, (tk, LANES), 1)
    pos = jnp.where(lane < 2 * ALIBI_SPLIT, jnp.where(lane % 2 == 0, rel >> 8, rel & 255), 0)
    pos = pos.astype(F32).astype(BF16)
    bias = bias_ref[...].astype(F32)
    ones_col = jnp.where(lane == 0, 1.0, 0.0).astype(BF16)

    def masked_logits(h):
        cols = slice(h * d, (h + 1) * d)
        qa = jnp.concatenate([q_ref[:, cols], sl_ref[h]], axis=1)
        ka = jnp.concatenate([k_ref[:, cols], pos], axis=1)
        s = lax.dot_general(qa, ka, (((1,), (1,)), ((), ())), preferred_element_type=F32)
        return [s[:, c * LANES:(c + 1) * LANES] + bias[:, c * LANES:(c + 1) * LANES] for c in range(n_lane)]

    sb_next = masked_logits(0)
    for h in range(heads):
        cols = slice(h * d, (h + 1) * d)
        sb = sb_next
        if h + 1 < heads:
            sb_next = masked_logits(h + 1)
        mx = sb[0]
        for c in range(1, n_lane):
            mx = jnp.maximum(mx, sb[c])
        m_old = m_ref[h]
        m_new = jnp.maximum(m_old, jnp.max(mx, axis=1, keepdims=True))
        corr = jnp.exp2(m_old - m_new)
        ps = [jnp.exp2((sb[c] - m_new).astype(BF16)) for c in range(n_lane)]
        va = jnp.concatenate([v_ref[:, cols], ones_col], axis=1)
        pv = jnp.dot(jnp.concatenate(ps, axis=1), va, preferred_element_type=F32)
        l_ref[h] = corr * l_ref[h] + pv[:, d:]
        acc_ref[:, cols] = corr * acc_ref[:, cols] + pv[:, :d]
        m_ref[h] = m_new

    @pl.when((kj + 1) * tk >= (qi + 1) * tq)
    def _():
        for h in range(heads):
            cols = slice(h * d, (h + 1) * d)
            l = jnp.sum(l_ref[h], axis=1, keepdims=True)
            o_ref[:, cols] = (acc_ref[:, cols] / l).astype(o_ref.dtype)


def _alibi_columns(heads, tq):
    out = np.zeros((heads, LANES), np.float32)
    for h in range(heads):
        rest = np.float64(2.0 ** (-8.0 * (h + 1) / heads) * np.log2(np.e))
        for i in range(ALIBI_SPLIT):
            piece = np.float64(np.float32(rest).astype(BF16))
            out[h, 2 * i], out[h, 2 * i + 1] = 256.0 * piece, piece
            rest = rest - piece
    return jnp.asarray(np.broadcast_to(out.astype(BF16)[:, None, :], (heads, tq, LANES)))


def _dsa_attn(p_big, bias, B, T, tq, tk, q_off, width):
    nQ, nT = T // tq, T // tk
    pairs = [(i, j) for i in range(nQ) for j in range(((i + 1) * tq + tk - 1) // tk)]
    qi_tbl = jnp.asarray(np.array([p[0] for p in pairs], np.int32))
    kj_tbl = jnp.asarray(np.array([p[1] for p in pairs], np.int32))
    qb = q_off // width
    heads = DSA_HEADS
    kern = functools.partial(_dsa_attn_kernel, heads=heads)
    grid_spec = pltpu.PrefetchScalarGridSpec(
        num_scalar_prefetch=2,
        grid=(B, len(pairs)),
        in_specs=[
            pl.BlockSpec((tq, width), lambda b, p, qi, kj: (b * nQ + qi[p], qb)),
            pl.BlockSpec((tk, width), lambda b, p, qi, kj: (b * nT + kj[p], qb + 1)),
            pl.BlockSpec((tk, width), lambda b, p, qi, kj: (b * nT + kj[p], qb + 2)),
            pl.BlockSpec((None, None, None, tq, tk), lambda b, p, qi, kj: (b, qi[p], kj[p], 0, 0)),
            pl.BlockSpec((heads, tq, LANES), lambda b, p, qi, kj: (0, 0, 0)),
        ],
        out_specs=pl.BlockSpec((tq, width), lambda b, p, qi, kj: (b * nQ + qi[p], 0)),
        scratch_shapes=[pltpu.VMEM((heads, tq, LANES), F32),
                        pltpu.VMEM((heads, tq, LANES), F32),
                        pltpu.VMEM((tq, width), F32)],
    )
    return pl.pallas_call(
        kern,
        out_shape=jax.ShapeDtypeStruct((B * T, width), BF16),
        grid_spec=grid_spec,
        compiler_params=_cparams(("arbitrary", "arbitrary")),
        name="dsa_attention",
    )(qi_tbl, kj_tbl, p_big, p_big, p_big, bias, _alibi_columns(heads, tq))


def _mem_attn_kernel(q_ref, k_ref, v_ref, o_ref):
    s = lax.dot_general(q_ref[...], k_ref[...], (((1,), (1,)), ((), ())), preferred_element_type=F32)
    m = jnp.max(s, axis=1, keepdims=True)
    p = jnp.exp(s - m)
    l = jnp.sum(p, axis=1, keepdims=True)
    o = jnp.dot(p.astype(BF16), v_ref[...], preferred_element_type=F32)
    o_ref[...] = (o / l).astype(o_ref.dtype)


def _mem_attn(p_big, kv, B, T, Mt, dm, q_off, tq):
    Hm = MEM_HEADS
    nq = T // tq
    qb = q_off // dm
    return pl.pallas_call(
        _mem_attn_kernel,
        out_shape=jax.ShapeDtypeStruct((B * T, Hm * dm), BF16),
        grid=(B, nq, Hm),
        in_specs=[
            pl.BlockSpec((tq, dm), lambda b, i, h: (b * nq + i, qb + h)),
            pl.BlockSpec((Mt, dm), lambda b, i, h: (b, h)),
            pl.BlockSpec((Mt, dm), lambda b, i, h: (b, Hm + h)),
        ],
        out_specs=pl.BlockSpec((tq, dm), lambda b, i, h: (b * nq + i, h)),
        compiler_params=_cparams(("parallel", "parallel", "arbitrary")),
        name="memory_attention",
    )(p_big, kv, kv)


def _merge_kernel(a_ref, d_ref, m_ref, wa_ref, wd_ref, wm_ref, g0_ref, g1_ref, g2_ref,
                  b0_ref, b1_ref, b2_ref, o_ref):
    def gate(g_ref, b_ref):
        return jax.nn.sigmoid(g_ref[...].astype(F32) + b_ref[...])

    acc = gate(g0_ref, b0_ref) * jnp.dot(a_ref[...], wa_ref[...], preferred_element_type=F32)
    acc = acc + gate(g1_ref, b1_ref) * jnp.dot(d_ref[...], wd_ref[...], preferred_element_type=F32)
    acc = acc + gate(g2_ref, b2_ref) * jnp.dot(m_ref[...], wm_ref[...], preferred_element_type=F32)
    o_ref[...] = acc.astype(o_ref.dtype)


def _merge(o_gla, o_dsa, o_mem, wa, wd, wm, p_big, b_merge, D, gate_off, tm, tn):
    M = o_gla.shape[0]
    gb = gate_off // tn
    nd = D // tn
    row = lambda w: pl.BlockSpec((tm, w), lambda i, j: (i, 0))
    col = lambda k: pl.BlockSpec((k, tn), lambda i, j: (0, j))
    gspec = lambda r: pl.BlockSpec((tm, tn), lambda i, j: (i, gb + r * nd + j))
    bspec = lambda r: pl.BlockSpec((1, tn), lambda i, j: (0, r * nd + j))
    return pl.pallas_call(
        _merge_kernel,
        out_shape=jax.ShapeDtypeStruct((M, D), BF16),
        grid=(M // tm, D // tn),
        in_specs=[row(o_gla.shape[1]), row(o_dsa.shape[1]), row(o_mem.shape[1]),
                  col(wa.shape[0]), col(wd.shape[0]), col(wm.shape[0]),
                  gspec(0), gspec(1), gspec(2), bspec(0), bspec(1), bspec(2)],
        out_specs=pl.BlockSpec((tm, tn), lambda i, j: (i, j)),
        compiler_params=_cparams(("parallel", "arbitrary")),
        name="gated_merge",
    )(o_gla, o_dsa, o_mem, wa, wd, wm, p_big, p_big, p_big, b_merge, b_merge, b_merge)


def _layer_norm(y, g, b):
    mu = jnp.mean(y, axis=-1, keepdims=True)
    yc = y - mu
    var = jnp.mean(yc * yc, axis=-1, keepdims=True)
    return yc * lax.rsqrt(var + LN_EPS) * g + b


def _proj_ln_kernel(mg_ref, wo_ref, x_ref, g_ref, b_ref, o_ref, *, alpha):
    y = jnp.dot(mg_ref[...], wo_ref[...], preferred_element_type=F32)
    o_ref[...] = _layer_norm(alpha * x_ref[...] + y, g_ref[...], b_ref[...])


def _proj_ln(merged, w_o, x2, g, b, alpha, tm):
    M, D = x2.shape
    vec = pl.BlockSpec((1, D), lambda i: (0, 0))
    return pl.pallas_call(
        functools.partial(_proj_ln_kernel, alpha=alpha),
        out_shape=jax.ShapeDtypeStruct((M, D), F32),
        grid=(M // tm,),
        in_specs=[pl.BlockSpec((tm, D), lambda i: (i, 0)),
                  pl.BlockSpec((D, D), lambda i: (0, 0)),
                  pl.BlockSpec((tm, D), lambda i: (i, 0)), vec, vec],
        out_specs=pl.BlockSpec((tm, D), lambda i: (i, 0)),
        compiler_params=_cparams(("parallel",)),
        name="out_proj_ln",
    )(merged, w_o, x2, g, b)


def _mlp_kernel(x_ref, wu_ref, bu_ref, wd_ref, bd_ref, g_ref, b_ref, o_ref, xb_ref, acc_ref, *, alpha):
    f = pl.program_id(1)

    @pl.when(f == 0)
    def _():
        xb_ref[...] = x_ref[...].astype(BF16)
        acc_ref[...] = jnp.zeros_like(acc_ref)

    h = jnp.dot(xb_ref[...], wu_ref[...], preferred_element_type=F32) + bu_ref[...]
    h = jnp.square(jnp.maximum(h, 0.0))
    acc_ref[...] += jnp.dot(h.astype(BF16), wd_ref[...], preferred_element_type=F32)

    @pl.when(f == pl.num_programs(1) - 1)
    def _():
        y = alpha * x_ref[...] + acc_ref[...] + bd_ref[...]
        o_ref[...] = _layer_norm(y, g_ref[...], b_ref[...])


def _mlp(x1, w_up, b_up, w_down, b_down, g, b, alpha, tm, tf):
    M, D = x1.shape
    F = w_up.shape[1]
    vec = pl.BlockSpec((1, D), lambda i, f: (0, 0))
    return pl.pallas_call(
        functools.partial(_mlp_kernel, alpha=alpha),
        out_shape=jax.ShapeDtypeStruct((M, D), F32),
        grid=(M // tm, F // tf),
        in_specs=[pl.BlockSpec((tm, D), lambda i, f: (i, 0)),
                  pl.BlockSpec((D, tf), lambda i, f: (0, f)),
                  pl.BlockSpec((1, tf), lambda i, f: (0, f)),
                  pl.BlockSpec((tf, D), lambda i, f: (f, 0)),
                  vec, vec, vec],
        out_specs=pl.BlockSpec((tm, D), lambda i, f: (i, 0)),
        scratch_shapes=[pltpu.VMEM((tm, D), BF16), pltpu.VMEM((tm, D), F32)],
        compiler_params=_cparams(("parallel", "arbitrary")),
        name="mlp_ln",
    )(x1, w_up, b_up, w_down, b_down, g, b)


def _tile(n, pref):
    t = min(n, pref)
    assert n % t == 0
    return t


def _layer(x2, mem2, B, T, w_in, w_gate_up, b_gate, norm_g, w_mem_kv, w_br_gla, w_br_dsa, w_br_mem,
           b_merge, w_o, ln1_g, ln1_b, w_up, b_up, w_down, b_down, ln2_g, ln2_b, alpha):
    M, D = x2.shape
    Mt = mem2.shape[0] // B
    dk, dv = D // 8, D // 4
    dh = D // 16
    idim = D // 32
    dm = D // 8
    H, G = GLA_HEADS, IDX_HEADS
    widths = [H * dk, H * dk, H * dv, H * dv, GLA_GATE_RANK, DSA_HEADS * dh, DSA_HEADS * dh, DSA_HEADS * dh,
              G * idim, idim, G, MEM_HEADS * dm, N_BRANCH * D]
    names = ['gla_q', 'gla_k', 'gla_v', 'gla_r', 'glr', 'dsa_q', 'dsa_k', 'dsa_v', 'idx_q', 'idx_k', 'idx_w',
             'mem_q', 'gate']
    offs = dict(zip(names, np.cumsum([0] + widths[:-1]).tolist()))
    wd = dict(zip(names, widths))
    col = lambda n, scale=None: (w_in[:, offs[n]:offs[n] + wd[n]] if scale is None
                                 else w_in[:, offs[n]:offs[n] + wd[n]] * scale)
    big_parts = [('gla_q', dk ** -0.5), ('gla_k', None), ('gla_v', None), ('gla_r', None),
                 ('dsa_q', dh ** -0.5 * float(np.log2(np.e))), ('dsa_k', None), ('dsa_v', None), ('idx_q', None),
                 ('mem_q', dm ** -0.5), ('gate', None)]
    w_big = jnp.concatenate([col(n, s) for n, s in big_parts], axis=1).astype(BF16)
    boff = dict(zip([n for n, _ in big_parts], np.cumsum([0] + [wd[n] for n, _ in big_parts][:-1]).tolist()))
    n_small = wd['glr'] + wd['idx_k'] + wd['idx_w']
    w_small = jnp.concatenate([col('glr'), col('idx_k'), col('idx_w'),
                               jnp.zeros((D, LANES - n_small), w_in.dtype)], axis=1).astype(BF16)

    p_big, small = _in_proj(x2, w_big, w_small, 1024, 1024)

    o_gla = _gla(p_big, small, w_gate_up, b_gate.reshape(1, -1), norm_g.reshape(1, -1), B, T, dk, dv,
                 _tile(T, 256))

    tq, tk = _tile(T, 256), _tile(T, 512)
    topk = min(DSA_TOPK_MAX, T // 4)
    ik = small[:, wd['glr']:wd['glr'] + idim].astype(BF16)
    ikt = ik.reshape(B, T // tk, tk, idim).transpose(0, 1, 3, 2)
    bias = _dsa_mask(p_big, small, ikt, B, T, tq, tk, boff['idx_q'], wd['glr'] + idim, topk)
    o_dsa = _dsa_attn(p_big, bias, B, T, tq, tk, boff['dsa_q'], DSA_HEADS * dh)

    kv = _matmul(mem2, w_mem_kv.astype(BF16), BF16, 512, 512, "mem_kv")
    o_mem = _mem_attn(p_big, kv, B, T, Mt, dm, boff['mem_q'], _tile(T, 1024))

    merged = _merge(o_gla, o_dsa, o_mem, w_br_gla.astype(BF16), w_br_dsa.astype(BF16), w_br_mem.astype(BF16),
                    p_big, b_merge.reshape(1, -1), D, boff['gate'], _tile(M, 1024), 512)
    x1 = _proj_ln(merged, w_o.astype(BF16), x2, ln1_g.reshape(1, -1), ln1_b.reshape(1, -1), alpha, _tile(M, 256))
    return _mlp(x1, w_up.astype(BF16), b_up.reshape(1, -1), w_down.astype(BF16), b_down.reshape(1, -1),
                ln2_g.reshape(1, -1), ln2_b.reshape(1, -1), alpha, _tile(M, 512), 1024)


def kernel(x, mem, w_in, w_gla_gate_up, b_gla_gate, gla_norm_g, w_mem_kv, w_br_gla, w_br_dsa, w_br_mem,
           b_merge, w_o, ln1_g, ln1_b, w_up, b_up, w_down, b_down, ln2_g, ln2_b):
    B, T, D = x.shape
    depth = w_in.shape[0]
    alpha = (2 * depth) ** 0.25
    x2 = x.reshape(B * T, D)
    mem2 = mem.reshape(-1, D)
    for l in range(depth):
        x2 = _layer(x2, mem2, B, T, w_in[l], w_gla_gate_up[l], b_gla_gate[l], gla_norm_g[l], w_mem_kv[l],
                    w_br_gla[l], w_br_dsa[l], w_br_mem[l], b_merge[l], w_o[l], ln1_g[l], ln1_b[l],
                    w_up[l], b_up[l], w_down[l], b_down[l], ln2_g[l], ln2_b[l], alpha)
    return x2.reshape(B, T, D)
```

```python
import functools

import numpy as np
import jax
import jax.numpy as jnp
from jax import lax
from jax.experimental import pallas as pl
from jax.experimental.pallas import tpu as pltpu

F32 = jnp.float32
BF16 = jnp.bfloat16

GLA_HEADS = 4
GLA_GATE_RANK = 16
GLA_GATE_TEMP = 16.0
GLA_CHUNK = 64
DSA_HEADS = 8
IDX_HEADS = 16
DSA_TOPK_MAX = 256
MEM_HEADS = 4
N_BRANCH = 3
LN_EPS = 1e-5
RMS_EPS = 1e-6

V7X_VMEM_LIMIT_BYTES = 56 * 1024 * 1024
LANES = 128
MASK_NEG = -1e30
CAND_PER_GROUP = 7
ALIBI_SPLIT = 3


def _cparams(sem):
    return pltpu.CompilerParams(dimension_semantics=sem, vmem_limit_bytes=V7X_VMEM_LIMIT_BYTES)


def _mm_cast_kernel(x_ref, w_ref, o_ref, xb_ref):
    @pl.when(pl.program_id(1) == 0)
    def _():
        xb_ref[...] = x_ref[...].astype(BF16)

    o_ref[...] = jnp.dot(xb_ref[...], w_ref[...], preferred_element_type=F32).astype(o_ref.dtype)


def _matmul(x, w, out_dtype, tm, tn, name):
    M, K = x.shape
    N = w.shape[1]
    tm, tn = min(tm, M), min(tn, N)
    assert M % tm == 0 and N % tn == 0
    return pl.pallas_call(
        _mm_cast_kernel,
        out_shape=jax.ShapeDtypeStruct((M, N), out_dtype),
        grid=(M // tm, N // tn),
        in_specs=[pl.BlockSpec((tm, K), lambda i, j: (i, 0)),
                  pl.BlockSpec((K, tn), lambda i, j: (0, j))],
        out_specs=pl.BlockSpec((tm, tn), lambda i, j: (i, j)),
        scratch_shapes=[pltpu.VMEM((tm, K), BF16)],
        compiler_params=_cparams(("parallel", "arbitrary")),
        name=name,
    )(x, w)


def _in_proj_kernel(x_ref, w_ref, ws_ref, o_ref, os_ref, xb_ref):
    @pl.when(pl.program_id(1) == 0)
    def _():
        xb_ref[...] = x_ref[...].astype(BF16)
        os_ref[...] = jnp.dot(xb_ref[...], ws_ref[...], preferred_element_type=F32)

    o_ref[...] = jnp.dot(xb_ref[...], w_ref[...], preferred_element_type=F32).astype(o_ref.dtype)


def _in_proj(x, w_big, w_small, tm, tn):
    M, K = x.shape
    N, Ns = w_big.shape[1], w_small.shape[1]
    tm, tn = min(tm, M), min(tn, N)
    assert M % tm == 0 and N % tn == 0
    return pl.pallas_call(
        _in_proj_kernel,
        out_shape=(jax.ShapeDtypeStruct((M, N), BF16), jax.ShapeDtypeStruct((M, Ns), F32)),
        grid=(M // tm, N // tn),
        in_specs=[pl.BlockSpec((tm, K), lambda i, j: (i, 0)),
                  pl.BlockSpec((K, tn), lambda i, j: (0, j)),
                  pl.BlockSpec((K, Ns), lambda i, j: (0, 0))],
        out_specs=(pl.BlockSpec((tm, tn), lambda i, j: (i, j)),
                   pl.BlockSpec((tm, Ns), lambda i, j: (i, 0))),
        scratch_shapes=[pltpu.VMEM((tm, K), BF16)],
        compiler_params=_cparams(("parallel", "arbitrary")),
        name="in_proj",
    )(x, w_big, w_small)


def _log_sigmoid(z):
    return jnp.minimum(z, 0.0) - jnp.log(1.0 + jnp.exp(-jnp.abs(z)))


def _gla_kernel(q_ref, k_ref, v_ref, r_ref, sm_ref, wg_ref, bg_ref, ng_ref, o_ref, st_ref, *, chunk, rank):
    @pl.when(pl.program_id(2) == 0)
    def _():
        st_ref[...] = jnp.zeros_like(st_ref)

    tb, dk = q_ref.shape
    C = chunk
    glr = sm_ref[:, 0:rank]
    z = jnp.dot(glr, wg_ref[...], preferred_element_type=F32) + bg_ref[...]
    la = _log_sigmoid(z) * (1.0 / GLA_GATE_TEMP)
    causal = (lax.broadcasted_iota(jnp.int32, (C, C), 1) <= lax.broadcasted_iota(jnp.int32, (C, C), 0))
    tri = jnp.where(causal, 1.0, 0.0).astype(F32)

    n_chunk = tb // C
    chunk_rows = [slice(c * C, (c + 1) * C) for c in range(n_chunk)]
    bcs = [jnp.dot(tri, la[rows], preferred_element_type=F32, precision=lax.Precision.HIGHEST)
           for rows in chunk_rows]
    scs, q_its, upds, decays = [], [], [], []
    for c, rows in enumerate(chunk_rows):
        bc = bcs[c]
        br = bc[C // 2:C // 2 + 1]
        bl = bc[C - 1:C]
        qc = q_ref[rows, :].astype(F32)
        kc = k_ref[rows, :].astype(F32)
        q_in = (qc * jnp.exp(bc - br)).astype(BF16)
        k_in = (kc * jnp.exp(br - bc)).astype(BF16)
        scs.append(lax.dot_general(q_in, k_in, (((1,), (1,)), ((), ())), preferred_element_type=F32))
        q_its.append((qc * jnp.exp(bc)).astype(BF16))
        k_st = (kc * jnp.exp(bl - bc)).astype(BF16)
        upds.append(lax.dot_general(v_ref[rows, :], k_st, (((0,), (0,)), ((), ())),
                                    preferred_element_type=F32))
        decays.append(jnp.exp(bl))
    o_intra = [jnp.dot(jnp.where(causal, scs[c], 0.0).astype(BF16), v_ref[rows, :], preferred_element_type=F32)
               for c, rows in enumerate(chunk_rows)]

    for c, rows in enumerate(chunk_rows):
        st = st_ref[...]
        o = o_intra[c] + lax.dot_general(q_its[c], st.astype(BF16), (((1,), (1,)), ((), ())),
                                         preferred_element_type=F32)
        st_ref[...] = st * decays[c] + upds[c]
        o = o * lax.rsqrt(jnp.mean(o * o, axis=-1, keepdims=True) + RMS_EPS)
        r = r_ref[rows, :].astype(F32)
        o = o * ng_ref[...] * (r / (1.0 + jnp.exp(-r)))
        o_ref[rows, :] = o.astype(o_ref.dtype)


def _gla(p_big, small, w_gate_up, b_gate, norm_g, B, T, dk, dv, tb):
    M = B * T
    H = GLA_HEADS
    nb = T // tb
    row = lambda b, h, n: b * nb + n
    kern = functools.partial(_gla_kernel, chunk=GLA_CHUNK, rank=GLA_GATE_RANK)
    koff = (H * dk) // dk
    voff = (2 * H * dk) // dv
    roff = (2 * H * dk + H * dv) // dv
    return pl.pallas_call(
        kern,
        out_shape=jax.ShapeDtypeStruct((M, H * dv), BF16),
        grid=(B, H, nb),
        in_specs=[
            pl.BlockSpec((tb, dk), lambda b, h, n: (row(b, h, n), h)),
            pl.BlockSpec((tb, dk), lambda b, h, n: (row(b, h, n), koff + h)),
            pl.BlockSpec((tb, dv), lambda b, h, n: (row(b, h, n), voff + h)),
            pl.BlockSpec((tb, dv), lambda b, h, n: (row(b, h, n), roff + h)),
            pl.BlockSpec((tb, LANES), lambda b, h, n: (row(b, h, n), 0)),
            pl.BlockSpec((GLA_GATE_RANK, dk), lambda b, h, n: (0, h)),
            pl.BlockSpec((1, dk), lambda b, h, n: (0, h)),
            pl.BlockSpec((1, dv), lambda b, h, n: (0, h)),
        ],
        out_specs=pl.BlockSpec((tb, dv), lambda b, h, n: (row(b, h, n), h)),
        scratch_shapes=[pltpu.VMEM((dv, dk), F32)],
        compiler_params=_cparams(("parallel", "parallel", "arbitrary")),
        name="gla",
    )(p_big, p_big, p_big, p_big, small, w_gate_up, b_gate, norm_g)


def _idx_kernel(iq_ref, sm_ref, ikt_ref, bias_ref, sc_ref, iqs_ref, wb_ref, cand_ref, thr_ref, *, topk, w_off, idim):
    i = pl.program_id(1)
    nT, tq, tk = sc_ref.shape
    G = iqs_ref.shape[0]
    n_lane = tk // LANES
    n_cand = cand_ref.shape[0]
    R = n_cand // n_lane
    ntile = ((i + 1) * tq + tk - 1) // tk
    kf = float(topk)
    RB = min(tq, 128)
    n_rb = tq // RB

    w = sm_ref[:, w_off:w_off + G] * ((G ** -0.5) * (idim ** -0.5))
    for g in range(G):
        iqs_ref[g] = iq_ref[:, g * idim:(g + 1) * idim]
        wb_ref[g] = jnp.broadcast_to(w[:, g:g + 1], (tq, LANES))
    row_pos = i * tq + lax.broadcasted_iota(jnp.int32, (tq, 1), 0)
    cand_ref[...] = jnp.full(cand_ref.shape, -jnp.inf, F32)

    def score_tile(j, carry):
        ik = ikt_ref[j]
        acc = jnp.zeros((tq, tk), F32)
        for g in range(G):
            a = jnp.dot(iqs_ref[g], ik, preferred_element_type=F32)
            wg = jnp.concatenate([wb_ref[g]] * n_lane, axis=1)
            acc = acc + wg * jnp.maximum(a, 0.0)
        col_pos = j * tk + lax.broadcasted_iota(jnp.int32, (1, tk), 1)
        sc = jnp.where(col_pos <= row_pos, acc, -jnp.inf)
        sc_ref[j] = sc
        for c in range(n_lane):
            x = sc[:, c * LANES:(c + 1) * LANES]
            for r in range(R):
                cur = cand_ref[c * R + r]
                cand_ref[c * R + r] = jnp.maximum(cur, x)
                x = jnp.minimum(cur, x)
        return carry

    lax.fori_loop(0, ntile, score_tile, 0)

    def to_key(v):
        bits = pltpu.bitcast(v, jnp.int32)
        return bits ^ ((bits >> 31) & 0x7FFFFFFF)

    def from_key(k):
        return pltpu.bitcast(k ^ ((k >> 31) & 0x7FFFFFFF), F32)

    gmax, last_kept = cand_ref[0], cand_ref[R - 1]
    for c in range(1, n_lane):
        gmax = jnp.maximum(gmax, cand_ref[c * R])
        last_kept = jnp.maximum(last_kept, cand_ref[c * R + R - 1])
    hi0 = to_key(jnp.max(gmax, axis=1, keepdims=True))
    last_kept = jnp.max(last_kept, axis=1, keepdims=True)
    n_cover = -(-topk // LANES)
    lo_f = None
    for r in range(min(R, n_cover)):
        need = -(-n_cover // (r + 1))
        if need > n_lane:
            continue
        mins = [jnp.min(cand_ref[c * R + r], axis=1, keepdims=True) for c in range(n_lane)]
        for a in range(need):
            for b2 in range(n_lane - 1, a, -1):
                hi_v, lo_v = jnp.maximum(mins[b2 - 1], mins[b2]), jnp.minimum(mins[b2 - 1], mins[b2])
                mins[b2 - 1], mins[b2] = hi_v, lo_v
        lo_f = mins[need - 1] if lo_f is None else jnp.maximum(lo_f, mins[need - 1])
    assert lo_f is not None, "topk must not exceed the number of key groups"
    lo0 = to_key(lo_f)
    few = row_pos < topk
    all_finite = to_key(jnp.full((tq, 1), -jnp.inf, F32)) + 1
    lo0 = jnp.where(few, all_finite, lo0)
    hi0 = jnp.where(few, all_finite, hi0)

    def count_ge(thr, strict=False):
        outs = []
        for r in range(n_rb):
            rows = slice(r * RB, (r + 1) * RB)
            tb = jnp.broadcast_to(thr[rows], (RB, LANES))

            def body(j, cnt, rows=rows, tb=tb):
                for c in range(n_lane):
                    st = sc_ref[j, rows, c * LANES:(c + 1) * LANES]
                    cnt = cnt + jnp.where((st > tb) if strict else (st >= tb), 1.0, 0.0)
                return cnt

            cnt = lax.fori_loop(0, ntile, body, jnp.zeros((RB, LANES), F32))
            outs.append(jnp.sum(cnt, axis=1, keepdims=True))
        return jnp.concatenate(outs, axis=0)

    def cand_count_ge(thr):
        outs = []
        for r in range(n_rb):
            rows = slice(r * RB, (r + 1) * RB)
            tb = jnp.broadcast_to(thr[rows], (RB, LANES))
            cnt = jnp.zeros((RB, LANES), F32)
            for q in range(n_cand):
                cnt = cnt + jnp.where(cand_ref[q, rows, :] >= tb, 1.0, 0.0)
            outs.append(jnp.sum(cnt, axis=1, keepdims=True))
        return jnp.concatenate(outs, axis=0)

    def cand_min_ge(thr):
        outs = []
        for r in range(n_rb):
            rows = slice(r * RB, (r + 1) * RB)
            tb = jnp.broadcast_to(thr[rows], (RB, LANES))
            cur = jnp.full((RB, LANES), jnp.inf, F32)
            for q in range(n_cand):
                ck = cand_ref[q, rows, :]
                cur = jnp.minimum(cur, jnp.where(ck >= tb, ck, jnp.inf))
            outs.append(jnp.min(cur, axis=1, keepdims=True))
        return jnp.concatenate(outs, axis=0)

    def n_open(lo, hi):
        return jnp.sum(jnp.where(lo < hi, 1.0, 0.0)).astype(jnp.int32)

    def search(count):
        def bisect(state):
            lo, hi, exact, _ = state
            x = lo ^ hi
            mid = (lo & hi) + (x >> 1) + (x & 1)
            cnt = count(from_key(mid))
            ge, eq = cnt >= kf, cnt == kf
            lo = jnp.where(ge, mid, lo)
            hi = jnp.where(eq, mid, jnp.where(ge, hi, mid - 1))
            return lo, hi, jnp.where(eq, 1, exact), n_open(lo, hi)

        lo, _, exact, _ = lax.while_loop(lambda s: s[3] > 0, bisect,
                                         (lo0, hi0, jnp.where(few, 1, 0), n_open(lo0, hi0)))
        return from_key(lo), exact

    thr, exact = search(cand_count_ge)
    thr = jnp.where(few, thr, cand_min_ge(thr))
    thr_ref[0] = thr
    thr_ref[1] = exact.astype(F32)
    n_unsure = jnp.sum(jnp.where((last_kept >= thr) & jnp.logical_not(few), 1.0, 0.0)).astype(jnp.int32)

    @pl.when(n_unsure > 0)
    def _():
        thr_full, exact_full = search(count_ge)
        thr_ref[0] = thr_full
        thr_ref[1] = exact_full.astype(F32)

    thr = thr_ref[0]
    n_tied = jnp.sum(jnp.where(thr_ref[1] > 0.0, 0.0, 1.0)).astype(jnp.int32)

    @pl.when(n_tied == 0)
    def _():
        thr_b = jnp.broadcast_to(thr, (tq, LANES))

        def write_tile(j, carry):
            st = sc_ref[j]
            parts = [jnp.where(st[:, c * LANES:(c + 1) * LANES] >= thr_b, 0.0, MASK_NEG) for c in range(n_lane)]
            bias_ref[j] = jnp.concatenate(parts, axis=1).astype(bias_ref.dtype)
            return carry

        lax.fori_loop(0, ntile, write_tile, 0)

    @pl.when(n_tied > 0)
    def _():
        need = kf - count_ge(thr, strict=True)
        before = (lax.broadcasted_iota(jnp.int32, (tk, tk), 0) < lax.broadcasted_iota(jnp.int32, (tk, tk), 1))
        before = jnp.where(before, 1.0, 0.0).astype(BF16)

        def write_tile(j, run):
            st = sc_ref[j]
            tied = jnp.where(st == thr, 1.0, 0.0)
            rank = jnp.dot(tied.astype(BF16), before, preferred_element_type=F32) + run
            sel = (st > thr) | ((st == thr) & (rank < need))
            bias_ref[j] = jnp.where(sel, 0.0, MASK_NEG).astype(bias_ref.dtype)
            return run + jnp.sum(tied, axis=1, keepdims=True)

        lax.fori_loop(0, ntile, write_tile, jnp.zeros((tq, 1), F32))

    def fill_tile(j, carry):
        bias_ref[j] = jnp.full((tq, tk), MASK_NEG, bias_ref.dtype)
        return carry

    lax.fori_loop(ntile, nT, fill_tile, 0)


def _dsa_mask(p_big, small, ikt, B, T, tq, tk, iq_off, w_off, topk):
    G = IDX_HEADS
    idim = ikt.shape[2]
    nQ, nT = T // tq, T // tk
    kern = functools.partial(_idx_kernel, topk=topk, w_off=w_off, idim=idim)
    return pl.pallas_call(
        kern,
        out_shape=jax.ShapeDtypeStruct((B, nQ, nT, tq, tk), BF16),
        grid=(B, nQ),
        in_specs=[
            pl.BlockSpec((tq, G * idim), lambda b, i: (b * nQ + i, iq_off // (G * idim))),
            pl.BlockSpec((tq, LANES), lambda b, i: (b * nQ + i, 0)),
            pl.BlockSpec((None, nT, idim, tk), lambda b, i: (b, 0, 0, 0)),
        ],
        out_specs=pl.BlockSpec((None, None, nT, tq, tk), lambda b, i: (b, i, 0, 0, 0)),
        scratch_shapes=[pltpu.VMEM((nT, tq, tk), F32),
                        pltpu.VMEM((G, tq, idim), BF16),
                        pltpu.VMEM((G, tq, LANES), F32),
                        pltpu.VMEM((tk // LANES * CAND_PER_GROUP, tq, LANES), F32),
                        pltpu.VMEM((2, tq, 1), F32)],
        compiler_params=_cparams(("parallel", "arbitrary")),
        name="dsa_index_topk",
    )(p_big, small, ikt)


def _dsa_attn_kernel(qi_ref, kj_ref, q_ref, k_ref, v_ref, bias_ref, sl_ref, o_ref, m_ref, l_ref, acc_ref, *, heads):
    p = pl.program_id(1)
    qi, kj = qi_ref[p], kj_ref[p]
    tq, tk = bias_ref.shape
    d = q_ref.shape[1] // heads
    n_lane = tk // LANES

    @pl.when(kj == 0)
    def _():
        m_ref[...] = jnp.full_like(m_ref, MASK_NEG)
        l_ref[...] = jnp.zeros_like(l_ref)
        acc_ref[...] = jnp.zeros_like(acc_ref)

    rel = (kj * tk - qi * tq) + lax.broadcasted_iota(jnp.int32, (tk, LANES), 0)
    lane = lax.broadcasted_iota(jnp.int32, (tk, LANES), 1)
    pos = jnp.where(lane < 2 * ALIBI_SPLIT, jnp.where(lane % 2 == 0, rel >> 8, rel & 255), 0)
    pos = pos.astype(F32).astype(BF16)
    bias = bias_ref[...].astype(F32)
    ones_col = jnp.where(lane == 0, 1.0, 0.0).astype(BF16)

    def masked_logits(h):
        cols = slice(h * d, (h + 1) * d)
        qa = jnp.concatenate([q_ref[:, cols], sl_ref[h]], axis=1)
        ka = jnp.concatenate([k_ref[:, cols], pos], axis=1)
        s = lax.dot_general(qa, ka, (((1,), (1,)), ((), ())), preferred_element_type=F32)
        return [s[:, c * LANES:(c + 1) * LANES] + bias[:, c * LANES:(c + 1) * LANES] for c in range(n_lane)]

    sb_next = masked_logits(0)
    for h in range(heads):
        cols = slice(h * d, (h + 1) * d)
        sb = sb_next
        if h + 1 < heads:
            sb_next = masked_logits(h + 1)
        mx = sb[0]
        for c in range(1, n_lane):
            mx = jnp.maximum(mx, sb[c])
        m_old = m_ref[h]
        m_new = jnp.maximum(m_old, jnp.max(mx, axis=1, keepdims=True))
        corr = jnp.exp2(m_old - m_new)
        ps = [jnp.exp2((sb[c] - m_new).astype(BF16)) for c in range(n_lane)]
        va = jnp.concatenate([v_ref[:, cols], ones_col], axis=1)
        pv = jnp.dot(jnp.concatenate(ps, axis=1), va, preferred_element_type=F32)
        l_ref[h] = corr * l_ref[h] + pv[:, d:]
        acc_ref[:, cols] = corr * acc_ref[:, cols] + pv[:, :d]
        m_ref[h] = m_new

    @pl.when((kj + 1) * tk >= (qi + 1) * tq)
    def _():
        for h in range(heads):
            cols = slice(h * d, (h + 1) * d)
            l = jnp.sum(l_ref[h], axis=1, keepdims=True)
            o_ref[:, cols] = (acc_ref[:, cols] / l).astype(o_ref.dtype)


def _alibi_columns(heads, tq):
    out = np.zeros((heads, LANES), np.float32)
    for h in range(heads):
        rest = np.float64(2.0 ** (-8.0 * (h + 1) / heads) * np.log2(np.e))
        for i in range(ALIBI_SPLIT):
            piece = np.float64(np.float32(rest).astype(BF16))
            out[h, 2 * i], out[h, 2 * i + 1] = 256.0 * piece, piece
            rest = rest - piece
    return jnp.asarray(np.broadcast_to(out.astype(BF16)[:, None, :], (heads, tq, LANES)))


def _dsa_attn(p_big, bias, B, T, tq, tk, q_off, width):
    nQ, nT = T // tq, T // tk
    pairs = [(i, j) for i in range(nQ) for j in range(((i + 1) * tq + tk - 1) // tk)]
    qi_tbl = jnp.asarray(np.array([p[0] for p in pairs], np.int32))
    kj_tbl = jnp.asarray(np.array([p[1] for p in pairs], np.int32))
    qb = q_off // width
    heads = DSA_HEADS
    kern = functools.partial(_dsa_attn_kernel, heads=heads)
    grid_spec = pltpu.PrefetchScalarGridSpec(
        num_scalar_prefetch=2,
        grid=(B, len(pairs)),
        in_specs=[
            pl.BlockSpec((tq, width), lambda b, p, qi, kj: (b * nQ + qi[p], qb)),
            pl.BlockSpec((tk, width), lambda b, p, qi, kj: (b * nT + kj[p], qb + 1)),
            pl.BlockSpec((tk, width), lambda b, p, qi, kj: (b * nT + kj[p], qb + 2)),
            pl.BlockSpec((None, None, None, tq, tk), lambda b, p, qi, kj: (b, qi[p], kj[p], 0, 0)),
            pl.BlockSpec((heads, tq, LANES), lambda b, p, qi, kj: (0, 0, 0)),
        ],
        out_specs=pl.BlockSpec((tq, width), lambda b, p, qi, kj: (b * nQ + qi[p], 0)),
        scratch_shapes=[pltpu.VMEM((heads, tq, LANES), F32),
                        pltpu.VMEM((heads, tq, LANES), F32),
                        pltpu.VMEM((tq, width), F32)],
    )
    return pl.pallas_call(
        kern,
        out_shape=jax.ShapeDtypeStruct((B * T, width), BF16),
        grid_spec=grid_spec,
        compiler_params=_cparams(("arbitrary", "arbitrary")),
        name="dsa_attention",
    )(qi_tbl, kj_tbl, p_big, p_big, p_big, bias, _alibi_columns(heads, tq))


def _mem_attn_kernel(q_ref, k_ref, v_ref, o_ref):
    s = lax.dot_general(q_ref[...], k_ref[...], (((1,), (1,)), ((), ())), preferred_element_type=F32)
    m = jnp.max(s, axis=1, keepdims=True)
    p = jnp.exp(s - m)
    l = jnp.sum(p, axis=1, keepdims=True)
    o = jnp.dot(p.astype(BF16), v_ref[...], preferred_element_type=F32)
    o_ref[...] = (o / l).astype(o_ref.dtype)


def _mem_attn(p_big, kv, B, T, Mt, dm, q_off, tq):
    Hm = MEM_HEADS
    nq = T // tq
    qb = q_off // dm
    return pl.pallas_call(
        _mem_attn_kernel,
        out_shape=jax.ShapeDtypeStruct((B * T, Hm * dm), BF16),
        grid=(B, nq, Hm),
        in_specs=[
            pl.BlockSpec((tq, dm), lambda b, i, h: (b * nq + i, qb + h)),
            pl.BlockSpec((Mt, dm), lambda b, i, h: (b, h)),
            pl.BlockSpec((Mt, dm), lambda b, i, h: (b, Hm + h)),
        ],
        out_specs=pl.BlockSpec((tq, dm), lambda b, i, h: (b * nq + i, h)),
        compiler_params=_cparams(("parallel", "parallel", "arbitrary")),
        name="memory_attention",
    )(p_big, kv, kv)


def _merge_kernel(a_ref, d_ref, m_ref, wa_ref, wd_ref, wm_ref, g0_ref, g1_ref, g2_ref,
                  b0_ref, b1_ref, b2_ref, o_ref):
    def gate(g_ref, b_ref):
        return jax.nn.sigmoid(g_ref[...].astype(F32) + b_ref[...])

    ya = jnp.dot(a_ref[...], wa_ref[...], preferred_element_type=F32)
    yd = jnp.dot(d_ref[...], wd_ref[...], preferred_element_type=F32)
    ym = jnp.dot(m_ref[...], wm_ref[...], preferred_element_type=F32)
    acc = gate(g0_ref, b0_ref) * ya + gate(g1_ref, b1_ref) * yd + gate(g2_ref, b2_ref) * ym
    o_ref[...] = acc.astype(o_ref.dtype)


def _merge(o_gla, o_dsa, o_mem, wa, wd, wm, p_big, b_merge, D, gate_off, tm, tn):
    M = o_gla.shape[0]
    gb = gate_off // tn
    nd = D // tn
    row = lambda w: pl.BlockSpec((tm, w), lambda i, j: (i, 0))
    col = lambda k: pl.BlockSpec((k, tn), lambda i, j: (0, j))
    gspec = lambda r: pl.BlockSpec((tm, tn), lambda i, j: (i, gb + r * nd + j))
    bspec = lambda r: pl.BlockSpec((1, tn), lambda i, j: (0, r * nd + j))
    return pl.pallas_call(
        _merge_kernel,
        out_shape=jax.ShapeDtypeStruct((M, D), BF16),
        grid=(M // tm, D // tn),
        in_specs=[row(o_gla.shape[1]), row(o_dsa.shape[1]), row(o_mem.shape[1]),
                  col(wa.shape[0]), col(wd.shape[0]), col(wm.shape[0]),
                  gspec(0), gspec(1), gspec(2), bspec(0), bspec(1), bspec(2)],
        out_specs=pl.BlockSpec((tm, tn), lambda i, j: (i, j)),
        compiler_params=_cparams(("parallel", "arbitrary")),
        name="gated_merge",
    )(o_gla, o_dsa, o_mem, wa, wd, wm, p_big, p_big, p_big, b_merge, b_merge, b_merge)


def _layer_norm(y, g, b):
    mu = jnp.mean(y, axis=-1, keepdims=True)
    yc = y - mu
    var = jnp.mean(yc * yc, axis=-1, keepdims=True)
    return yc * lax.rsqrt(var + LN_EPS) * g + b


def _proj_ln_kernel(mg_ref, wo_ref, x_ref, g_ref, b_ref, o_ref, *, alpha):
    tm = mg_ref.shape[0]
    halves = [slice(0, tm // 2), slice(tm // 2, tm)]
    ys = [jnp.dot(mg_ref[rows, :], wo_ref[...], preferred_element_type=F32) for rows in halves]
    for rows, y in zip(halves, ys):
        o_ref[rows, :] = _layer_norm(alpha * x_ref[rows, :] + y, g_ref[...], b_ref[...])


def _proj_ln(merged, w_o, x2, g, b, alpha, tm):
    M, D = x2.shape
    vec = pl.BlockSpec((1, D), lambda i: (0, 0))
    return pl.pallas_call(
        functools.partial(_proj_ln_kernel, alpha=alpha),
        out_shape=jax.ShapeDtypeStruct((M, D), F32),
        grid=(M // tm,),
        in_specs=[pl.BlockSpec((tm, D), lambda i: (i, 0)),
                  pl.BlockSpec((D, D), lambda i: (0, 0)),
                  pl.BlockSpec((tm, D), lambda i: (i, 0)), vec, vec],
        out_specs=pl.BlockSpec((tm, D), lambda i: (i, 0)),
        compiler_params=_cparams(("parallel",)),
        name="out_proj_ln",
    )(merged, w_o, x2, g, b)


def _mlp_kernel(x_ref, wu_ref, bu_ref, wd_ref, bd_ref, g_ref, b_ref, o_ref, xb_ref, acc_ref, *, alpha):
    f = pl.program_id(1)

    @pl.when(f == 0)
    def _():
        xb_ref[...] = x_ref[...].astype(BF16)
        acc_ref[...] = jnp.zeros_like(acc_ref)

    tf = wu_ref.shape[1]
    halves = [slice(0, tf // 2), slice(tf // 2, tf)]
    ups = [jnp.dot(xb_ref[...], wu_ref[:, cols], preferred_element_type=F32) for cols in halves]
    hs = [jnp.square(jnp.maximum(u + bu_ref[:, cols], 0.0)).astype(BF16) for u, cols in zip(ups, halves)]
    downs = [jnp.dot(h, wd_ref[cols, :], preferred_element_type=F32) for h, cols in zip(hs, halves)]
    acc_ref[...] += downs[0] + downs[1]

    @pl.when(f == pl.num_programs(1) - 1)
    def _():
        y = alpha * x_ref[...] + acc_ref[...] + bd_ref[...]
        o_ref[...] = _layer_norm(y, g_ref[...], b_ref[...])


def _mlp(x1, w_up, b_up, w_down, b_down, g, b, alpha, tm, tf):
    M, D = x1.shape
    F = w_up.shape[1]
    vec = pl.BlockSpec((1, D), lambda i, f: (0, 0))
    return pl.pallas_call(
        functools.partial(_mlp_kernel, alpha=alpha),
        out_shape=jax.ShapeDtypeStruct((M, D), F32),
        grid=(M // tm, F // tf),
        in_specs=[pl.BlockSpec((tm, D), lambda i, f: (i, 0)),
                  pl.BlockSpec((D, tf), lambda i, f: (0, f)),
                  pl.BlockSpec((1, tf), lambda i, f: (0, f)),
                  pl.BlockSpec((tf, D), lambda i, f: (f, 0)),
                  vec, vec, vec],
        out_specs=pl.BlockSpec((tm, D), lambda i, f: (i, 0)),
        scratch_shapes=[pltpu.VMEM((tm, D), BF16), pltpu.VMEM((tm, D), F32)],
        compiler_params=_cparams(("parallel", "arbitrary")),
        name="mlp_ln",
    )(x1, w_up, b_up, w_down, b_down, g, b)


def _tile(n, pref):
    t = min(n, pref)
    assert n % t == 0
    return t


def _layer(x2, mem2, B, T, w_in, w_gate_up, b_gate, norm_g, w_mem_kv, w_br_gla, w_br_dsa, w_br_mem,
           b_merge, w_o, ln1_g, ln1_b, w_up, b_up, w_down, b_down, ln2_g, ln2_b, alpha):
    M, D = x2.shape
    Mt = mem2.shape[0] // B
    dk, dv = D // 8, D // 4
    dh = D // 16
    idim = D // 32
    dm = D // 8
    H, G = GLA_HEADS, IDX_HEADS
    widths = [H * dk, H * dk, H * dv, H * dv, GLA_GATE_RANK, DSA_HEADS * dh, DSA_HEADS * dh, DSA_HEADS * dh,
              G * idim, idim, G, MEM_HEADS * dm, N_BRANCH * D]
    names = ['gla_q', 'gla_k', 'gla_v', 'gla_r', 'glr', 'dsa_q', 'dsa_k', 'dsa_v', 'idx_q', 'idx_k', 'idx_w',
             'mem_q', 'gate']
    offs = dict(zip(names, np.cumsum([0] + widths[:-1]).tolist()))
    wd = dict(zip(names, widths))
    col = lambda n, scale=None: (w_in[:, offs[n]:offs[n] + wd[n]] if scale is None
                                 else w_in[:, offs[n]:offs[n] + wd[n]] * scale)
    big_parts = [('gla_q', dk ** -0.5), ('gla_k', None), ('gla_v', None), ('gla_r', None),
                 ('dsa_q', dh ** -0.5 * float(np.log2(np.e))), ('dsa_k', None), ('dsa_v', None), ('idx_q', None),
                 ('mem_q', dm ** -0.5), ('gate', None)]
    w_big = jnp.concatenate([col(n, s) for n, s in big_parts], axis=1).astype(BF16)
    boff = dict(zip([n for n, _ in big_parts], np.cumsum([0] + [wd[n] for n, _ in big_parts][:-1]).tolist()))
    n_small = wd['glr'] + wd['idx_k'] + wd['idx_w']
    w_small = jnp.concatenate([col('glr'), col('idx_k'), col('idx_w'),
                               jnp.zeros((D, LANES - n_small), w_in.dtype)], axis=1).astype(BF16)

    p_big, small = _in_proj(x2, w_big, w_small, 1024, 1024)

    o_gla = _gla(p_big, small, w_gate_up, b_gate.reshape(1, -1), norm_g.reshape(1, -1), B, T, dk, dv,
                 _tile(T, 256))

    tq, tk = _tile(T, 256), _tile(T, 512)
    topk = min(DSA_TOPK_MAX, T // 4)
    ik = small[:, wd['glr']:wd['glr'] + idim].astype(BF16)
    ikt = ik.reshape(B, T // tk, tk, idim).transpose(0, 1, 3, 2)
    bias = _dsa_mask(p_big, small, ikt, B, T, tq, tk, boff['idx_q'], wd['glr'] + idim, topk)
    o_dsa = _dsa_attn(p_big, bias, B, T, tq, tk, boff['dsa_q'], DSA_HEADS * dh)

    kv = _matmul(mem2, w_mem_kv.astype(BF16), BF16, 512, 512, "mem_kv")
    o_mem = _mem_attn(p_big, kv, B, T, Mt, dm, boff['mem_q'], _tile(T, 1024))

    merged = _merge(o_gla, o_dsa, o_mem, w_br_gla.astype(BF16), w_br_dsa.astype(BF16), w_br_mem.astype(BF16),
                    p_big, b_merge.reshape(1, -1), D, boff['gate'], _tile(M, 1024), 512)
    x1 = _proj_ln(merged, w_o.astype(BF16), x2, ln1_g.reshape(1, -1), ln1_b.reshape(1, -1), alpha, _tile(M, 512))
    return _mlp(x1, w_up.astype(BF16), b_up.reshape(1, -1), w_down.astype(BF16), b_down.reshape(1, -1),
                ln2_g.reshape(1, -1), ln2_b.reshape(1, -1), alpha, _tile(M, 512), 1024)


def kernel(x, mem, w_in, w_gla_gate_up, b_gla_gate, gla_norm_g, w_mem_kv, w_br_gla, w_br_dsa, w_br_mem,
           b_merge, w_o, ln1_g, ln1_b, w_up, b_up, w_down, b_down, ln2_g, ln2_b):
    B, T, D = x.shape
    depth = w_in.shape[0]
    alpha = (2 * depth) ** 0.25
    x2 = x.reshape(B * T, D)
    mem2 = mem.reshape(-1, D)
    for l in range(depth):
        x2 = _layer(x2, mem2, B, T, w_in[l], w_gla_gate_up[l], b_gla_gate[l], gla_norm_g[l], w_mem_kv[l],
                    w_br_gla[l], w_br_dsa[l], w_br_mem[l], b_merge[l], w_o[l], ln1_g[l], ln1_b[l],
                    w_up[l], b_up[l], w_down[l], b_down[l], ln2_g[l], ln2_b[l], alpha)
    return x2.reshape(B, T, D)
```

```python
import functools

import numpy as np
import jax
import jax.numpy as jnp
from jax import lax
from jax.experimental import pallas as pl
from jax.experimental.pallas import tpu as pltpu

F32 = jnp.float32
BF16 = jnp.bfloat16

GLA_HEADS = 4
GLA_GATE_RANK = 16
GLA_GATE_TEMP = 16.0
GLA_CHUNK = 64
DSA_HEADS = 8
IDX_HEADS = 16
DSA_TOPK_MAX = 256
MEM_HEADS = 4
N_BRANCH = 3
LN_EPS = 1e-5
RMS_EPS = 1e-6

V7X_VMEM_LIMIT_BYTES = 56 * 1024 * 1024
LANES = 128
MASK_NEG = -1e30
CAND_PER_GROUP = 7
ALIBI_SPLIT = 3


def _cparams(sem):
    return pltpu.CompilerParams(dimension_semantics=sem, vmem_limit_bytes=V7X_VMEM_LIMIT_BYTES)


def _mm_cast_kernel(x_ref, w_ref, o_ref, xb_ref):
    @pl.when(pl.program_id(1) == 0)
    def _():
        xb_ref[...] = x_ref[...].astype(BF16)

    o_ref[...] = jnp.dot(xb_ref[...], w_ref[...], preferred_element_type=F32).astype(o_ref.dtype)


def _matmul(x, w, out_dtype, tm, tn, name):
    M, K = x.shape
    N = w.shape[1]
    tm, tn = min(tm, M), min(tn, N)
    assert M % tm == 0 and N % tn == 0
    return pl.pallas_call(
        _mm_cast_kernel,
        out_shape=jax.ShapeDtypeStruct((M, N), out_dtype),
        grid=(M // tm, N // tn),
        in_specs=[pl.BlockSpec((tm, K), lambda i, j: (i, 0)),
                  pl.BlockSpec((K, tn), lambda i, j: (0, j))],
        out_specs=pl.BlockSpec((tm, tn), lambda i, j: (i, j)),
        scratch_shapes=[pltpu.VMEM((tm, K), BF16)],
        compiler_params=_cparams(("parallel", "arbitrary")),
        name=name,
    )(x, w)


def _in_proj_kernel(x_ref, w_ref, ws_ref, o_ref, os_ref, xb_ref):
    @pl.when(pl.program_id(1) == 0)
    def _():
        xb_ref[...] = x_ref[...].astype(BF16)
        os_ref[...] = jnp.dot(xb_ref[...], ws_ref[...], preferred_element_type=F32)

    o_ref[...] = jnp.dot(xb_ref[...], w_ref[...], preferred_element_type=F32).astype(o_ref.dtype)


def _in_proj(x, w_big, w_small, tm, tn):
    M, K = x.shape
    N, Ns = w_big.shape[1], w_small.shape[1]
    tm, tn = min(tm, M), min(tn, N)
    assert M % tm == 0 and N % tn == 0
    return pl.pallas_call(
        _in_proj_kernel,
        out_shape=(jax.ShapeDtypeStruct((M, N), BF16), jax.ShapeDtypeStruct((M, Ns), F32)),
        grid=(M // tm, N // tn),
        in_specs=[pl.BlockSpec((tm, K), lambda i, j: (i, 0)),
                  pl.BlockSpec((K, tn), lambda i, j: (0, j)),
                  pl.BlockSpec((K, Ns), lambda i, j: (0, 0))],
        out_specs=(pl.BlockSpec((tm, tn), lambda i, j: (i, j)),
                   pl.BlockSpec((tm, Ns), lambda i, j: (i, 0))),
        scratch_shapes=[pltpu.VMEM((tm, K), BF16)],
        compiler_params=_cparams(("parallel", "arbitrary")),
        name="in_proj",
    )(x, w_big, w_small)


def _log_sigmoid(z):
    return jnp.minimum(z, 0.0) - jnp.log(1.0 + jnp.exp(-jnp.abs(z)))


def _gla_kernel(q_ref, k_ref, v_ref, r_ref, sm_ref, wg_ref, bg_ref, ng_ref, o_ref, st_ref, *, chunk, rank):
    @pl.when(pl.program_id(2) == 0)
    def _():
        st_ref[...] = jnp.zeros_like(st_ref)

    tb, dk = q_ref.shape
    C = chunk
    glr = sm_ref[:, 0:rank]
    z = jnp.dot(glr, wg_ref[...], preferred_element_type=F32) + bg_ref[...]
    la = _log_sigmoid(z) * (1.0 / GLA_GATE_TEMP)
    causal = (lax.broadcasted_iota(jnp.int32, (C, C), 1) <= lax.broadcasted_iota(jnp.int32, (C, C), 0))
    tri = jnp.where(causal, 1.0, 0.0).astype(F32)

    n_chunk = tb // C
    chunk_rows = [slice(c * C, (c + 1) * C) for c in range(n_chunk)]
    bcs = [jnp.dot(tri, la[rows], preferred_element_type=F32, precision=lax.Precision.HIGHEST)
           for rows in chunk_rows]
    scs, q_its, upds, decays = [], [], [], []
    for c, rows in enumerate(chunk_rows):
        bc = bcs[c]
        br = bc[C // 2:C // 2 + 1]
        bl = bc[C - 1:C]
        qc = q_ref[rows, :].astype(F32)
        kc = k_ref[rows, :].astype(F32)
        q_in = (qc * jnp.exp(bc - br)).astype(BF16)
        k_in = (kc * jnp.exp(br - bc)).astype(BF16)
        scs.append(lax.dot_general(q_in, k_in, (((1,), (1,)), ((), ())), preferred_element_type=F32))
        q_its.append((qc * jnp.exp(bc)).astype(BF16))
        k_st = (kc * jnp.exp(bl - bc)).astype(BF16)
        upds.append(lax.dot_general(v_ref[rows, :], k_st, (((0,), (0,)), ((), ())),
                                    preferred_element_type=F32))
        decays.append(jnp.exp(bl))
    o_intra = [jnp.dot(jnp.where(causal, scs[c], 0.0).astype(BF16), v_ref[rows, :], preferred_element_type=F32)
               for c, rows in enumerate(chunk_rows)]

    for c, rows in enumerate(chunk_rows):
        st = st_ref[...]
        o = o_intra[c] + lax.dot_general(q_its[c], st.astype(BF16), (((1,), (1,)), ((), ())),
                                         preferred_element_type=F32)
        st_ref[...] = st * decays[c] + upds[c]
        o = o * lax.rsqrt(jnp.mean(o * o, axis=-1, keepdims=True) + RMS_EPS)
        r = r_ref[rows, :].astype(F32)
        o = o * ng_ref[...] * (r / (1.0 + jnp.exp(-r)))
        o_ref[rows, :] = o.astype(o_ref.dtype)


def _gla(p_big, small, w_gate_up, b_gate, norm_g, B, T, dk, dv, tb):
    M = B * T
    H = GLA_HEADS
    nb = T // tb
    row = lambda b, h, n: b * nb + n
    kern = functools.partial(_gla_kernel, chunk=GLA_CHUNK, rank=GLA_GATE_RANK)
    koff = (H * dk) // dk
    voff = (2 * H * dk) // dv
    roff = (2 * H * dk + H * dv) // dv
    return pl.pallas_call(
        kern,
        out_shape=jax.ShapeDtypeStruct((M, H * dv), BF16),
        grid=(B, H, nb),
        in_specs=[
            pl.BlockSpec((tb, dk), lambda b, h, n: (row(b, h, n), h)),
            pl.BlockSpec((tb, dk), lambda b, h, n: (row(b, h, n), koff + h)),
            pl.BlockSpec((tb, dv), lambda b, h, n: (row(b, h, n), voff + h)),
            pl.BlockSpec((tb, dv), lambda b, h, n: (row(b, h, n), roff + h)),
            pl.BlockSpec((tb, LANES), lambda b, h, n: (row(b, h, n), 0)),
            pl.BlockSpec((GLA_GATE_RANK, dk), lambda b, h, n: (0, h)),
            pl.BlockSpec((1, dk), lambda b, h, n: (0, h)),
            pl.BlockSpec((1, dv), lambda b, h, n: (0, h)),
        ],
        out_specs=pl.BlockSpec((tb, dv), lambda b, h, n: (row(b, h, n), h)),
        scratch_shapes=[pltpu.VMEM((dv, dk), F32)],
        compiler_params=_cparams(("parallel", "parallel", "arbitrary")),
        name="gla",
    )(p_big, p_big, p_big, p_big, small, w_gate_up, b_gate, norm_g)


def _idx_kernel(iq_ref, sm_ref, ikt_ref, bias_ref, sc_ref, iqs_ref, wb_ref, cand_ref, thr_ref, *, topk, w_off, idim):
    i = pl.program_id(1)
    nT, tq, tk = sc_ref.shape
    G = iqs_ref.shape[0]
    n_lane = tk // LANES
    n_cand = cand_ref.shape[0]
    R = n_cand // n_lane
    ntile = ((i + 1) * tq + tk - 1) // tk
    kf = float(topk)
    RB = min(tq, 128)
    n_rb = tq // RB

    w = sm_ref[:, w_off:w_off + G] * ((G ** -0.5) * (idim ** -0.5))
    for g in range(G):
        iqs_ref[g] = iq_ref[:, g * idim:(g + 1) * idim]
        wb_ref[g] = jnp.broadcast_to(w[:, g:g + 1], (tq, LANES))
    row_pos = i * tq + lax.broadcasted_iota(jnp.int32, (tq, 1), 0)
    cand_ref[...] = jnp.full(cand_ref.shape, -jnp.inf, F32)

    def score_tile(j, carry):
        ik = ikt_ref[j]
        acc = jnp.zeros((tq, tk), F32)
        for g in range(G):
            a = jnp.dot(iqs_ref[g], ik, preferred_element_type=F32)
            wg = jnp.concatenate([wb_ref[g]] * n_lane, axis=1)
            acc = acc + wg * jnp.maximum(a, 0.0)
        col_pos = j * tk + lax.broadcasted_iota(jnp.int32, (1, tk), 1)
        sc = jnp.where(col_pos <= row_pos, acc, -jnp.inf)
        sc_ref[j] = sc
        for c in range(n_lane):
            x = sc[:, c * LANES:(c + 1) * LANES]
            for r in range(R):
                cur = cand_ref[c * R + r]
                cand_ref[c * R + r] = jnp.maximum(cur, x)
                x = jnp.minimum(cur, x)
        return carry

    lax.fori_loop(0, ntile, score_tile, 0)

    def to_key(v):
        bits = pltpu.bitcast(v, jnp.int32)
        return bits ^ ((bits >> 31) & 0x7FFFFFFF)

    def from_key(k):
        return pltpu.bitcast(k ^ ((k >> 31) & 0x7FFFFFFF), F32)

    gmax, last_kept = cand_ref[0], cand_ref[R - 1]
    for c in range(1, n_lane):
        gmax = jnp.maximum(gmax, cand_ref[c * R])
        last_kept = jnp.maximum(last_kept, cand_ref[c * R + R - 1])
    hi0 = to_key(jnp.max(gmax, axis=1, keepdims=True))
    last_kept = jnp.max(last_kept, axis=1, keepdims=True)
    n_cover = -(-topk // LANES)
    lo_f = None
    for r in range(min(R, n_cover)):
        need = -(-n_cover // (r + 1))
        if need > n_lane:
            continue
        mins = [jnp.min(cand_ref[c * R + r], axis=1, keepdims=True) for c in range(n_lane)]
        for a in range(need):
            for b2 in range(n_lane - 1, a, -1):
                hi_v, lo_v = jnp.maximum(mins[b2 - 1], mins[b2]), jnp.minimum(mins[b2 - 1], mins[b2])
                mins[b2 - 1], mins[b2] = hi_v, lo_v
        lo_f = mins[need - 1] if lo_f is None else jnp.maximum(lo_f, mins[need - 1])
    assert lo_f is not None, "topk must not exceed the number of key groups"
    lo0 = to_key(lo_f)
    few = row_pos < topk
    all_finite = to_key(jnp.full((tq, 1), -jnp.inf, F32)) + 1
    lo0 = jnp.where(few, all_finite, lo0)
    hi0 = jnp.where(few, all_finite, hi0)

    def count_ge(thr, strict=False):
        outs = []
        for r in range(n_rb):
            rows = slice(r * RB, (r + 1) * RB)
            tb = jnp.broadcast_to(thr[rows], (RB, LANES))

            def body(j, cnt, rows=rows, tb=tb):
                for c in range(n_lane):
                    st = sc_ref[j, rows, c * LANES:(c + 1) * LANES]
                    cnt = cnt + jnp.where((st > tb) if strict else (st >= tb), 1.0, 0.0)
                return cnt

            cnt = lax.fori_loop(0, ntile, body, jnp.zeros((RB, LANES), F32))
            outs.append(jnp.sum(cnt, axis=1, keepdims=True))
        return jnp.concatenate(outs, axis=0)

    def cand_count_ge(thr):
        outs = []
        for r in range(n_rb):
            rows = slice(r * RB, (r + 1) * RB)
            tb = jnp.broadcast_to(thr[rows], (RB, LANES))
            cnt = jnp.zeros((RB, LANES), F32)
            for q in range(n_cand):
                cnt = cnt + jnp.where(cand_ref[q, rows, :] >= tb, 1.0, 0.0)
            outs.append(jnp.sum(cnt, axis=1, keepdims=True))
        return jnp.concatenate(outs, axis=0)

    def cand_min_ge(thr):
        outs = []
        for r in range(n_rb):
            rows = slice(r * RB, (r + 1) * RB)
            tb = jnp.broadcast_to(thr[rows], (RB, LANES))
            cur = jnp.full((RB, LANES), jnp.inf, F32)
            for q in range(n_cand):
                ck = cand_ref[q, rows, :]
                cur = jnp.minimum(cur, jnp.where(ck >= tb, ck, jnp.inf))
            outs.append(jnp.min(cur, axis=1, keepdims=True))
        return jnp.concatenate(outs, axis=0)

    def n_open(lo, hi):
        return jnp.sum(jnp.where(lo < hi, 1.0, 0.0)).astype(jnp.int32)

    def search(count):
        def bisect(state):
            lo, hi, exact, _ = state
            x = lo ^ hi
            mid = (lo & hi) + (x >> 1) + (x & 1)
            cnt = count(from_key(mid))
            ge, eq = cnt >= kf, cnt == kf
            lo = jnp.where(ge, mid, lo)
            hi = jnp.where(eq, mid, jnp.where(ge, hi, mid - 1))
            return lo, hi, jnp.where(eq, 1, exact), n_open(lo, hi)

        lo, _, exact, _ = lax.while_loop(lambda s: s[3] > 0, bisect,
                                         (lo0, hi0, jnp.where(few, 1, 0), n_open(lo0, hi0)))
        return from_key(lo), exact

    thr, exact = search(cand_count_ge)
    thr = jnp.where(few, thr, cand_min_ge(thr))
    thr_ref[0] = thr
    thr_ref[1] = exact.astype(F32)
    n_unsure = jnp.sum(jnp.where((last_kept >= thr) & jnp.logical_not(few), 1.0, 0.0)).astype(jnp.int32)

    @pl.when(n_unsure > 0)
    def _():
        thr_full, exact_full = search(count_ge)
        thr_ref[0] = thr_full
        thr_ref[1] = exact_full.astype(F32)

    thr = thr_ref[0]
    n_tied = jnp.sum(jnp.where(thr_ref[1] > 0.0, 0.0, 1.0)).astype(jnp.int32)

    @pl.when(n_tied == 0)
    def _():
        thr_b = jnp.broadcast_to(thr, (tq, LANES))

        def write_tile(j, carry):
            st = sc_ref[j]
            parts = [jnp.where(st[:, c * LANES:(c + 1) * LANES] >= thr_b, 0.0, MASK_NEG) for c in range(n_lane)]
            bias_ref[j] = jnp.concatenate(parts, axis=1).astype(bias_ref.dtype)
            return carry

        lax.fori_loop(0, ntile, write_tile, 0)

    @pl.when(n_tied > 0)
    def _():
        need = kf - count_ge(thr, strict=True)
        before = (lax.broadcasted_iota(jnp.int32, (tk, tk), 0) < lax.broadcasted_iota(jnp.int32, (tk, tk), 1))
        before = jnp.where(before, 1.0, 0.0).astype(BF16)

        def write_tile(j, run):
            st = sc_ref[j]
            tied = jnp.where(st == thr, 1.0, 0.0)
            rank = jnp.dot(tied.astype(BF16), before, preferred_element_type=F32) + run
            sel = (st > thr) | ((st == thr) & (rank < need))
            bias_ref[j] = jnp.where(sel, 0.0, MASK_NEG).astype(bias_ref.dtype)
            return run + jnp.sum(tied, axis=1, keepdims=True)

        lax.fori_loop(0, ntile, write_tile, jnp.zeros((tq, 1), F32))

    def fill_tile(j, carry):
        bias_ref[j] = jnp.full((tq, tk), MASK_NEG, bias_ref.dtype)
        return carry

    lax.fori_loop(ntile, nT, fill_tile, 0)


def _dsa_mask(p_big, small, ikt, B, T, tq, tk, iq_off, w_off, topk):
    G = IDX_HEADS
    idim = ikt.shape[2]
    nQ, nT = T // tq, T // tk
    kern = functools.partial(_idx_kernel, topk=topk, w_off=w_off, idim=idim)
    return pl.pallas_call(
        kern,
        out_shape=jax.ShapeDtypeStruct((B, nQ, nT, tq, tk), BF16),
        grid=(B, nQ),
        in_specs=[
            pl.BlockSpec((tq, G * idim), lambda b, i: (b * nQ + i, iq_off // (G * idim))),
            pl.BlockSpec((tq, LANES), lambda b, i: (b * nQ + i, 0)),
            pl.BlockSpec((None, nT, idim, tk), lambda b, i: (b, 0, 0, 0)),
        ],
        out_specs=pl.BlockSpec((None, None, nT, tq, tk), lambda b, i: (b, i, 0, 0, 0)),
        scratch_shapes=[pltpu.VMEM((nT, tq, tk), F32),
                        pltpu.VMEM((G, tq, idim), BF16),
                        pltpu.VMEM((G, tq, LANES), F32),
                        pltpu.VMEM((tk // LANES * CAND_PER_GROUP, tq, LANES), F32),
                        pltpu.VMEM((2, tq, 1), F32)],
        compiler_params=_cparams(("parallel", "arbitrary")),
        name="dsa_index_topk",
    )(p_big, small, ikt)


def _dsa_attn_kernel(qi_ref, kj_ref, q_ref, k_ref, v_ref, bias_ref, sl_ref, o_ref, m_ref, l_ref, acc_ref, *, heads):
    p = pl.program_id(1)
    qi, kj = qi_ref[p], kj_ref[p]
    tq, tk = bias_ref.shape
    d = q_ref.shape[1] // heads
    n_lane = tk // LANES

    @pl.when(kj == 0)
    def _():
        m_ref[...] = jnp.full_like(m_ref, MASK_NEG)
        l_ref[...] = jnp.zeros_like(l_ref)
        acc_ref[...] = jnp.zeros_like(acc_ref)

    rel = (kj * tk - qi * tq) + lax.broadcasted_iota(jnp.int32, (tk, LANES), 0)
    lane = lax.broadcasted_iota(jnp.int32, (tk, LANES), 1)
    pos = jnp.where(lane < 2 * ALIBI_SPLIT, jnp.where(lane % 2 == 0, rel >> 8, rel & 255), 0)
    pos = pos.astype(F32).astype(BF16)
    bias = bias_ref[...].astype(F32)
    ones_col = jnp.where(lane == 0, 1.0, 0.0).astype(BF16)

    def masked_logits(h):
        cols = slice(h * d, (h + 1) * d)
        qa = jnp.concatenate([q_ref[:, cols], sl_ref[h]], axis=1)
        ka = jnp.concatenate([k_ref[:, cols], pos], axis=1)
        s = lax.dot_general(qa, ka, (((1,), (1,)), ((), ())), preferred_element_type=F32)
        return [s[:, c * LANES:(c + 1) * LANES] + bias[:, c * LANES:(c + 1) * LANES] for c in range(n_lane)]

    sb_next = masked_logits(0)
    for h in range(heads):
        cols = slice(h * d, (h + 1) * d)
        sb = sb_next
        if h + 1 < heads:
            sb_next = masked_logits(h + 1)
        mx = sb[0]
        for c in range(1, n_lane):
            mx = jnp.maximum(mx, sb[c])
        m_old = m_ref[h]
        m_new = jnp.maximum(m_old, jnp.max(mx, axis=1, keepdims=True))
        corr = jnp.exp2(m_old - m_new)
        ps = [jnp.exp2((sb[c] - m_new).astype(BF16)) for c in range(n_lane)]
        va = jnp.concatenate([v_ref[:, cols], ones_col], axis=1)
        pv = jnp.dot(jnp.concatenate(ps, axis=1), va, preferred_element_type=F32)
        l_ref[h] = corr * l_ref[h] + pv[:, d:]
        acc_ref[:, cols] = corr * acc_ref[:, cols] + pv[:, :d]
        m_ref[h] = m_new

    @pl.when((kj + 1) * tk >= (qi + 1) * tq)
    def _():
        for h in range(heads):
            cols = slice(h * d, (h + 1) * d)
            l = jnp.sum(l_ref[h], axis=1, keepdims=True)
            o_ref[:, cols] = (acc_ref[:, cols] / l).astype(o_ref.dtype)


def _alibi_columns(heads, tq):
    out = np.zeros((heads, LANES), np.float32)
    for h in range(heads):
        rest = np.float64(2.0 ** (-8.0 * (h + 1) / heads) * np.log2(np.e))
        for i in range(ALIBI_SPLIT):
            piece = np.float64(np.float32(rest).astype(BF16))
            out[h, 2 * i], out[h, 2 * i + 1] = 256.0 * piece, piece
            rest = rest - piece
    return jnp.asarray(np.broadcast_to(out.astype(BF16)[:, None, :], (heads, tq, LANES)))


def _dsa_attn(p_big, bias, B, T, tq, tk, q_off, width):
    nQ, nT = T // tq, T // tk
    pairs = [(i, j) for i in range(nQ) for j in range(((i + 1) * tq + tk - 1) // tk)]
    qi_tbl = jnp.asarray(np.array([p[0] for p in pairs], np.int32))
    kj_tbl = jnp.asarray(np.array([p[1] for p in pairs], np.int32))
    qb = q_off // width
    heads = DSA_HEADS
    kern = functools.partial(_dsa_attn_kernel, heads=heads)
    grid_spec = pltpu.PrefetchScalarGridSpec(
        num_scalar_prefetch=2,
        grid=(B, len(pairs)),
        in_specs=[
            pl.BlockSpec((tq, width), lambda b, p, qi, kj: (b * nQ + qi[p], qb)),
            pl.BlockSpec((tk, width), lambda b, p, qi, kj: (b * nT + kj[p], qb + 1)),
            pl.BlockSpec((tk, width), lambda b, p, qi, kj: (b * nT + kj[p], qb + 2)),
            pl.BlockSpec((None, None, None, tq, tk), lambda b, p, qi, kj: (b, qi[p], kj[p], 0, 0)),
            pl.BlockSpec((heads, tq, LANES), lambda b, p, qi, kj: (0, 0, 0)),
        ],
        out_specs=pl.BlockSpec((tq, width), lambda b, p, qi, kj: (b * nQ + qi[p], 0)),
        scratch_shapes=[pltpu.VMEM((heads, tq, LANES), F32),
                        pltpu.VMEM((heads, tq, LANES), F32),
                        pltpu.VMEM((tq, width), F32)],
    )
    return pl.pallas_call(
        kern,
        out_shape=jax.ShapeDtypeStruct((B * T, width), BF16),
        grid_spec=grid_spec,
        compiler_params=_cparams(("arbitrary", "arbitrary")),
        name="dsa_attention",
    )(qi_tbl, kj_tbl, p_big, p_big, p_big, bias, _alibi_columns(heads, tq))


def _mem_attn_kernel(q_ref, k_ref, v_ref, o_ref):
    s = lax.dot_general(q_ref[...], k_ref[...], (((1,), (1,)), ((), ())), preferred_element_type=F32)
    m = jnp.max(s, axis=1, keepdims=True)
    p = jnp.exp(s - m)
    l = jnp.sum(p, axis=1, keepdims=True)
    o = jnp.dot(p.astype(BF16), v_ref[...], preferred_element_type=F32)
    o_ref[...] = (o / l).astype(o_ref.dtype)


def _mem_attn(p_big, kv, B, T, Mt, dm, q_off, tq):
    Hm = MEM_HEADS
    nq = T // tq
    qb = q_off // dm
    return pl.pallas_call(
        _mem_attn_kernel,
        out_shape=jax.ShapeDtypeStruct((B * T, Hm * dm), BF16),
        grid=(B, nq, Hm),
        in_specs=[
            pl.BlockSpec((tq, dm), lambda b, i, h: (b * nq + i, qb + h)),
            pl.BlockSpec((Mt, dm), lambda b, i, h: (b, h)),
            pl.BlockSpec((Mt, dm), lambda b, i, h: (b, Hm + h)),
        ],
        out_specs=pl.BlockSpec((tq, dm), lambda b, i, h: (b * nq + i, h)),
        compiler_params=_cparams(("parallel", "parallel", "arbitrary")),
        name="memory_attention",
    )(p_big, kv, kv)


def _merge_kernel(a_ref, d_ref, m_ref, wa_ref, wd_ref, wm_ref, g0_ref, g1_ref, g2_ref, b_ref, o_ref, *, tn):
    D = o_ref.shape[1]

    def products(cols):
        return (jnp.dot(a_ref[...], wa_ref[:, cols], preferred_element_type=F32),
                jnp.dot(d_ref[...], wd_ref[:, cols], preferred_element_type=F32),
                jnp.dot(m_ref[...], wm_ref[:, cols], preferred_element_type=F32))

    def gated_sum(cols, ys):
        acc = None
        for r, (g_ref, y) in enumerate(zip((g0_ref, g1_ref, g2_ref), ys)):
            bias = b_ref[:, r * D + cols.start:r * D + cols.stop]
            term = jax.nn.sigmoid(g_ref[:, cols].astype(F32) + bias) * y
            acc = term if acc is None else acc + term
        o_ref[:, cols] = acc.astype(o_ref.dtype)

    chunks = [slice(j * tn, (j + 1) * tn) for j in range(D // tn)]
    pending = None
    for cols in chunks:
        ys = products(cols)
        if pending is not None:
            gated_sum(*pending)
        pending = (cols, ys)
    gated_sum(*pending)


def _merge(o_gla, o_dsa, o_mem, wa, wd, wm, p_big, b_merge, D, gate_off, tm, tn):
    M = o_gla.shape[0]
    gb = gate_off // D
    row = lambda w: pl.BlockSpec((tm, w), lambda i: (i, 0))
    resident = lambda k: pl.BlockSpec((k, D), lambda i: (0, 0), pipeline_mode=pl.Buffered(1))
    gspec = lambda r: pl.BlockSpec((tm, D), lambda i: (i, gb + r))
    return pl.pallas_call(
        functools.partial(_merge_kernel, tn=tn),
        out_shape=jax.ShapeDtypeStruct((M, D), BF16),
        grid=(M // tm,),
        in_specs=[row(o_gla.shape[1]), row(o_dsa.shape[1]), row(o_mem.shape[1]),
                  resident(wa.shape[0]), resident(wd.shape[0]), resident(wm.shape[0]),
                  gspec(0), gspec(1), gspec(2),
                  pl.BlockSpec((1, N_BRANCH * D), lambda i: (0, 0))],
        out_specs=pl.BlockSpec((tm, D), lambda i: (i, 0)),
        compiler_params=_cparams(("parallel",)),
        name="gated_merge",
    )(o_gla, o_dsa, o_mem, wa, wd, wm, p_big, p_big, p_big, b_merge)


def _layer_norm(y, g, b):
    mu = jnp.mean(y, axis=-1, keepdims=True)
    yc = y - mu
    var = jnp.mean(yc * yc, axis=-1, keepdims=True)
    return yc * lax.rsqrt(var + LN_EPS) * g + b


def _proj_ln_kernel(mg_ref, wo_ref, x_ref, g_ref, b_ref, o_ref, *, alpha):
    tm = mg_ref.shape[0]
    halves = [slice(0, tm // 2), slice(tm // 2, tm)]
    ys = [jnp.dot(mg_ref[rows, :], wo_ref[...], preferred_element_type=F32) for rows in halves]
    for rows, y in zip(halves, ys):
        o_ref[rows, :] = _layer_norm(alpha * x_ref[rows, :] + y, g_ref[...], b_ref[...])


def _proj_ln(merged, w_o, x2, g, b, alpha, tm):
    M, D = x2.shape
    vec = pl.BlockSpec((1, D), lambda i: (0, 0))
    return pl.pallas_call(
        functools.partial(_proj_ln_kernel, alpha=alpha),
        out_shape=jax.ShapeDtypeStruct((M, D), F32),
        grid=(M // tm,),
        in_specs=[pl.BlockSpec((tm, D), lambda i: (i, 0)),
                  pl.BlockSpec((D, D), lambda i: (0, 0)),
                  pl.BlockSpec((tm, D), lambda i: (i, 0)), vec, vec],
        out_specs=pl.BlockSpec((tm, D), lambda i: (i, 0)),
        compiler_params=_cparams(("parallel",)),
        name="out_proj_ln",
    )(merged, w_o, x2, g, b)


def _mlp_kernel(x_ref, wu_ref, bu_ref, wd_ref, bd_ref, g_ref, b_ref, o_ref, xb_ref, acc_ref, *, alpha):
    f = pl.program_id(1)

    @pl.when(f == 0)
    def _():
        xb_ref[...] = x_ref[...].astype(BF16)
        acc_ref[...] = jnp.zeros_like(acc_ref)

    tf = wu_ref.shape[1]
    halves = [slice(0, tf // 2), slice(tf // 2, tf)]
    ups = [jnp.dot(xb_ref[...], wu_ref[:, cols], preferred_element_type=F32) for cols in halves]
    hs = [jnp.square(jnp.maximum(u + bu_ref[:, cols], 0.0)).astype(BF16) for u, cols in zip(ups, halves)]
    downs = [jnp.dot(h, wd_ref[cols, :], preferred_element_type=F32) for h, cols in zip(hs, halves)]
    acc_ref[...] += downs[0] + downs[1]

    @pl.when(f == pl.num_programs(1) - 1)
    def _():
        y = alpha * x_ref[...] + acc_ref[...] + bd_ref[...]
        o_ref[...] = _layer_norm(y, g_ref[...], b_ref[...])


def _mlp(x1, w_up, b_up, w_down, b_down, g, b, alpha, tm, tf):
    M, D = x1.shape
    F = w_up.shape[1]
    vec = pl.BlockSpec((1, D), lambda i, f: (0, 0))
    return pl.pallas_call(
        functools.partial(_mlp_kernel, alpha=alpha),
        out_shape=jax.ShapeDtypeStruct((M, D), F32),
        grid=(M // tm, F // tf),
        in_specs=[pl.BlockSpec((tm, D), lambda i, f: (i, 0), pipeline_mode=pl.Buffered(1)),
                  pl.BlockSpec((D, tf), lambda i, f: (0, f)),
                  pl.BlockSpec((1, tf), lambda i, f: (0, f)),
                  pl.BlockSpec((tf, D), lambda i, f: (f, 0)),
                  vec, vec, vec],
        out_specs=pl.BlockSpec((tm, D), lambda i, f: (i, 0), pipeline_mode=pl.Buffered(1)),
        scratch_shapes=[pltpu.VMEM((tm, D), BF16), pltpu.VMEM((tm, D), F32)],
        compiler_params=_cparams(("parallel", "arbitrary")),
        name="mlp_ln",
    )(x1, w_up, b_up, w_down, b_down, g, b)


def _tile(n, pref):
    t = min(n, pref)
    assert n % t == 0
    return t


def _layer(x2, mem2, B, T, w_in, w_gate_up, b_gate, norm_g, w_mem_kv, w_br_gla, w_br_dsa, w_br_mem,
           b_merge, w_o, ln1_g, ln1_b, w_up, b_up, w_down, b_down, ln2_g, ln2_b, alpha):
    M, D = x2.shape
    Mt = mem2.shape[0] // B
    dk, dv = D // 8, D // 4
    dh = D // 16
    idim = D // 32
    dm = D // 8
    H, G = GLA_HEADS, IDX_HEADS
    widths = [H * dk, H * dk, H * dv, H * dv, GLA_GATE_RANK, DSA_HEADS * dh, DSA_HEADS * dh, DSA_HEADS * dh,
              G * idim, idim, G, MEM_HEADS * dm, N_BRANCH * D]
    names = ['gla_q', 'gla_k', 'gla_v', 'gla_r', 'glr', 'dsa_q', 'dsa_k', 'dsa_v', 'idx_q', 'idx_k', 'idx_w',
             'mem_q', 'gate']
    offs = dict(zip(names, np.cumsum([0] + widths[:-1]).tolist()))
    wd = dict(zip(names, widths))
    col = lambda n, scale=None: (w_in[:, offs[n]:offs[n] + wd[n]] if scale is None
                                 else w_in[:, offs[n]:offs[n] + wd[n]] * scale)
    big_parts = [('gla_q', dk ** -0.5), ('gla_k', None), ('gla_v', None), ('gla_r', None), ('gate', None),
                 ('dsa_q', dh ** -0.5 * float(np.log2(np.e))), ('dsa_k', None), ('dsa_v', None), ('idx_q', None),
                 ('mem_q', dm ** -0.5)]
    w_big = jnp.concatenate([col(n, s) for n, s in big_parts], axis=1).astype(BF16)
    boff = dict(zip([n for n, _ in big_parts], np.cumsum([0] + [wd[n] for n, _ in big_parts][:-1]).tolist()))
    n_small = wd['glr'] + wd['idx_k'] + wd['idx_w']
    w_small = jnp.concatenate([col('glr'), col('idx_k'), col('idx_w'),
                               jnp.zeros((D, LANES - n_small), w_in.dtype)], axis=1).astype(BF16)

    p_big, small = _in_proj(x2, w_big, w_small, 1024, 1024)

    o_gla = _gla(p_big, small, w_gate_up, b_gate.reshape(1, -1), norm_g.reshape(1, -1), B, T, dk, dv,
                 _tile(T, 256))

    tq, tk = _tile(T, 256), _tile(T, 512)
    topk = min(DSA_TOPK_MAX, T // 4)
    ik = small[:, wd['glr']:wd['glr'] + idim].astype(BF16)
    ikt = ik.reshape(B, T // tk, tk, idim).transpose(0, 1, 3, 2)
    bias = _dsa_mask(p_big, small, ikt, B, T, tq, tk, boff['idx_q'], wd['glr'] + idim, topk)
    o_dsa = _dsa_attn(p_big, bias, B, T, tq, tk, boff['dsa_q'], DSA_HEADS * dh)

    kv = _matmul(mem2, w_mem_kv.astype(BF16), BF16, 512, 512, "mem_kv")
    o_mem = _mem_attn(p_big, kv, B, T, Mt, dm, boff['mem_q'], _tile(T, 1024))

    merged = _merge(o_gla, o_dsa, o_mem, w_br_gla.astype(BF16), w_br_dsa.astype(BF16), w_br_mem.astype(BF16),
                    p_big, b_merge.reshape(1, -1), D, boff['gate'], _tile(M, 512), 512)
    x1 = _proj_ln(merged, w_o.astype(BF16), x2, ln1_g.reshape(1, -1), ln1_b.reshape(1, -1), alpha, _tile(M, 512))
    return _mlp(x1, w_up.astype(BF16), b_up.reshape(1, -1), w_down.astype(BF16), b_down.reshape(1, -1),
                ln2_g.reshape(1, -1), ln2_b.reshape(1, -1), alpha, _tile(M, 1024), 512)


def kernel(x, mem, w_in, w_gla_gate_up, b_gla_gate, gla_norm_g, w_mem_kv, w_br_gla, w_br_dsa, w_br_mem,
           b_merge, w_o, ln1_g, ln1_b, w_up, b_up, w_down, b_down, ln2_g, ln2_b):
    B, T, D = x.shape
    depth = w_in.shape[0]
    alpha = (2 * depth) ** 0.25
    x2 = x.reshape(B * T, D)
    mem2 = mem.reshape(-1, D)
    for l in range(depth):
        x2 = _layer(x2, mem2, B, T, w_in[l], w_gla_gate_up[l], b_gla_gate[l], gla_norm_g[l], w_mem_kv[l],
                    w_br_gla[l], w_br_dsa[l], w_br_mem[l], b_merge[l], w_o[l], ln1_g[l], ln1_b[l],
                    w_up[l], b_up[l], w_down[l], b_down[l], ln2_g[l], ln2_b[l], alpha)
    return x2.reshape(B, T, D)
```

```python
import functools

import numpy as np
import jax
import jax.numpy as jnp
from jax import lax
from jax.experimental import pallas as pl
from jax.experimental.pallas import tpu as pltpu

F32 = jnp.float32
BF16 = jnp.bfloat16

GLA_HEADS = 4
GLA_GATE_RANK = 16
GLA_GATE_TEMP = 16.0
GLA_CHUNK = 64
DSA_HEADS = 8
IDX_HEADS = 16
DSA_TOPK_MAX = 256
MEM_HEADS = 4
N_BRANCH = 3
LN_EPS = 1e-5
RMS_EPS = 1e-6

V7X_VMEM_LIMIT_BYTES = 56 * 1024 * 1024
LANES = 128
MASK_NEG = -1e30
SCORE_SUB_ROWS, SCORE_SUB_COLS = 128, 256
CAND_PER_GROUP = 7
ALIBI_SPLIT = 3


def _cparams(sem):
    return pltpu.CompilerParams(dimension_semantics=sem, vmem_limit_bytes=V7X_VMEM_LIMIT_BYTES)


def _mm_cast_kernel(x_ref, w_ref, o_ref, xb_ref):
    @pl.when(pl.program_id(1) == 0)
    def _():
        xb_ref[...] = x_ref[...].astype(BF16)

    o_ref[...] = jnp.dot(xb_ref[...], w_ref[...], preferred_element_type=F32).astype(o_ref.dtype)


def _matmul(x, w, out_dtype, tm, tn, name):
    M, K = x.shape
    N = w.shape[1]
    tm, tn = min(tm, M), min(tn, N)
    assert M % tm == 0 and N % tn == 0
    return pl.pallas_call(
        _mm_cast_kernel,
        out_shape=jax.ShapeDtypeStruct((M, N), out_dtype),
        grid=(M // tm, N // tn),
        in_specs=[pl.BlockSpec((tm, K), lambda i, j: (i, 0)),
                  pl.BlockSpec((K, tn), lambda i, j: (0, j))],
        out_specs=pl.BlockSpec((tm, tn), lambda i, j: (i, j)),
        scratch_shapes=[pltpu.VMEM((tm, K), BF16)],
        compiler_params=_cparams(("parallel", "arbitrary")),
        name=name,
    )(x, w)


def _in_proj_kernel(x_ref, w_ref, ws_ref, o_ref, os_ref, xb_ref):
    @pl.when(pl.program_id(1) == 0)
    def _():
        xb_ref[...] = x_ref[...].astype(BF16)
        os_ref[...] = jnp.dot(xb_ref[...], ws_ref[...], preferred_element_type=F32)

    o_ref[...] = jnp.dot(xb_ref[...], w_ref[...], preferred_element_type=F32).astype(o_ref.dtype)


def _in_proj(x, w_big, w_small, tm, tn):
    M, K = x.shape
    N, Ns = w_big.shape[1], w_small.shape[1]
    tm, tn = min(tm, M), min(tn, N)
    assert M % tm == 0 and N % tn == 0
    return pl.pallas_call(
        _in_proj_kernel,
        out_shape=(jax.ShapeDtypeStruct((M, N), BF16), jax.ShapeDtypeStruct((M, Ns), F32)),
        grid=(M // tm, N // tn),
        in_specs=[pl.BlockSpec((tm, K), lambda i, j: (i, 0)),
                  pl.BlockSpec((K, tn), lambda i, j: (0, j)),
                  pl.BlockSpec((K, Ns), lambda i, j: (0, 0))],
        out_specs=(pl.BlockSpec((tm, tn), lambda i, j: (i, j)),
                   pl.BlockSpec((tm, Ns), lambda i, j: (i, 0))),
        scratch_shapes=[pltpu.VMEM((tm, K), BF16)],
        compiler_params=_cparams(("parallel", "arbitrary")),
        name="in_proj",
    )(x, w_big, w_small)


def _log_sigmoid(z):
    return jnp.minimum(z, 0.0) - jnp.log(1.0 + jnp.exp(-jnp.abs(z)))


def _gla_kernel(q_ref, k_ref, v_ref, r_ref, sm_ref, wg_ref, bg_ref, ng_ref, o_ref, st_ref, *, chunk, rank):
    @pl.when(pl.program_id(2) == 0)
    def _():
        st_ref[...] = jnp.zeros_like(st_ref)

    tb, dk = q_ref.shape
    C = chunk
    glr = sm_ref[:, 0:rank]
    z = jnp.dot(glr, wg_ref[...], preferred_element_type=F32) + bg_ref[...]
    la = _log_sigmoid(z) * (1.0 / GLA_GATE_TEMP)
    causal = (lax.broadcasted_iota(jnp.int32, (C, C), 1) <= lax.broadcasted_iota(jnp.int32, (C, C), 0))
    tri = jnp.where(causal, 1.0, 0.0).astype(F32)

    n_chunk = tb // C
    chunk_rows = [slice(c * C, (c + 1) * C) for c in range(n_chunk)]
    bcs = [jnp.dot(tri, la[rows], preferred_element_type=F32, precision=lax.Precision.HIGHEST)
           for rows in chunk_rows]
    scs, q_its, upds, decays = [], [], [], []
    for c, rows in enumerate(chunk_rows):
        bc = bcs[c]
        br = bc[C // 2:C // 2 + 1]
        bl = bc[C - 1:C]
        qc = q_ref[rows, :].astype(F32)
        kc = k_ref[rows, :].astype(F32)
        q_in = (qc * jnp.exp(bc - br)).astype(BF16)
        k_in = (kc * jnp.exp(br - bc)).astype(BF16)
        scs.append(lax.dot_general(q_in, k_in, (((1,), (1,)), ((), ())), preferred_element_type=F32))
        q_its.append((qc * jnp.exp(bc)).astype(BF16))
        k_st = (kc * jnp.exp(bl - bc)).astype(BF16)
        upds.append(lax.dot_general(v_ref[rows, :], k_st, (((0,), (0,)), ((), ())),
                                    preferred_element_type=F32))
        decays.append(jnp.exp(bl))
    o_intra = [jnp.dot(jnp.where(causal, scs[c], 0.0).astype(BF16), v_ref[rows, :], preferred_element_type=F32)
               for c, rows in enumerate(chunk_rows)]

    for c, rows in enumerate(chunk_rows):
        st = st_ref[...]
        o = o_intra[c] + lax.dot_general(q_its[c], st.astype(BF16), (((1,), (1,)), ((), ())),
                                         preferred_element_type=F32)
        st_ref[...] = st * decays[c] + upds[c]
        o = o * lax.rsqrt(jnp.mean(o * o, axis=-1, keepdims=True) + RMS_EPS)
        r = r_ref[rows, :].astype(F32)
        o = o * ng_ref[...] * (r / (1.0 + jnp.exp(-r)))
        o_ref[rows, :] = o.astype(o_ref.dtype)


def _gla(p_big, small, w_gate_up, b_gate, norm_g, B, T, dk, dv, tb):
    M = B * T
    H = GLA_HEADS
    nb = T // tb
    row = lambda b, h, n: b * nb + n
    kern = functools.partial(_gla_kernel, chunk=GLA_CHUNK, rank=GLA_GATE_RANK)
    koff = (H * dk) // dk
    voff = (2 * H * dk) // dv
    roff = (2 * H * dk + H * dv) // dv
    return pl.pallas_call(
        kern,
        out_shape=jax.ShapeDtypeStruct((M, H * dv), BF16),
        grid=(B, H, nb),
        in_specs=[
            pl.BlockSpec((tb, dk), lambda b, h, n: (row(b, h, n), h)),
            pl.BlockSpec((tb, dk), lambda b, h, n: (row(b, h, n), koff + h)),
            pl.BlockSpec((tb, dv), lambda b, h, n: (row(b, h, n), voff + h)),
            pl.BlockSpec((tb, dv), lambda b, h, n: (row(b, h, n), roff + h)),
            pl.BlockSpec((tb, LANES), lambda b, h, n: (row(b, h, n), 0)),
            pl.BlockSpec((GLA_GATE_RANK, dk), lambda b, h, n: (0, h)),
            pl.BlockSpec((1, dk), lambda b, h, n: (0, h)),
            pl.BlockSpec((1, dv), lambda b, h, n: (0, h)),
        ],
        out_specs=pl.BlockSpec((tb, dv), lambda b, h, n: (row(b, h, n), h)),
        scratch_shapes=[pltpu.VMEM((dv, dk), F32)],
        compiler_params=_cparams(("parallel", "parallel", "arbitrary")),
        name="gla",
    )(p_big, p_big, p_big, p_big, small, w_gate_up, b_gate, norm_g)


def _idx_kernel(iq_ref, sm_ref, ikt_ref, bias_ref, sc_ref, iqs_ref, wb_ref, cand_ref, thr_ref, *, topk, w_off, idim):
    i = pl.program_id(1)
    nT, tq, tk = sc_ref.shape
    G = iqs_ref.shape[0]
    n_lane = tk // LANES
    n_cand = cand_ref.shape[0]
    R = n_cand // n_lane
    ntile = ((i + 1) * tq + tk - 1) // tk
    kf = float(topk)
    RB = min(tq, 128)
    n_rb = tq // RB

    w = sm_ref[:, w_off:w_off + G] * ((G ** -0.5) * (idim ** -0.5))
    for g in range(G):
        iqs_ref[g] = iq_ref[:, g * idim:(g + 1) * idim]
        wb_ref[g] = jnp.broadcast_to(w[:, g:g + 1], (tq, LANES))
    row_pos = i * tq + lax.broadcasted_iota(jnp.int32, (tq, 1), 0)
    cand_ref[...] = jnp.full(cand_ref.shape, -jnp.inf, F32)

    sub_rows, sub_cols = min(tq, SCORE_SUB_ROWS), min(tk, SCORE_SUB_COLS)

    def score_tile(j, carry):
        col_pos = j * tk + lax.broadcasted_iota(jnp.int32, (1, tk), 1)
        for rh in range(tq // sub_rows):
            rows = slice(rh * sub_rows, (rh + 1) * sub_rows)
            for lh in range(tk // sub_cols):
                cols = slice(lh * sub_cols, (lh + 1) * sub_cols)
                ik = ikt_ref[j, :, cols]
                acc = jnp.zeros((sub_rows, sub_cols), F32)
                for g in range(G):
                    a = jnp.dot(iqs_ref[g, rows, :], ik, preferred_element_type=F32)
                    wg = jnp.concatenate([wb_ref[g, rows, :]] * (sub_cols // LANES), axis=1)
                    acc = acc + wg * jnp.maximum(a, 0.0)
                sc = jnp.where(col_pos[:, cols] <= row_pos[rows], acc, -jnp.inf)
                sc_ref[j, rows, cols] = sc
                for cc in range(sub_cols // LANES):
                    c = lh * (sub_cols // LANES) + cc
                    x = sc[:, cc * LANES:(cc + 1) * LANES]
                    for r in range(R):
                        cur = cand_ref[c * R + r, rows, :]
                        cand_ref[c * R + r, rows, :] = jnp.maximum(cur, x)
                        x = jnp.minimum(cur, x)
        return carry

    lax.fori_loop(0, ntile, score_tile, 0)

    def to_key(v):
        bits = pltpu.bitcast(v, jnp.int32)
        return bits ^ ((bits >> 31) & 0x7FFFFFFF)

    def from_key(k):
        return pltpu.bitcast(k ^ ((k >> 31) & 0x7FFFFFFF), F32)

    gmax, last_kept = cand_ref[0], cand_ref[R - 1]
    for c in range(1, n_lane):
        gmax = jnp.maximum(gmax, cand_ref[c * R])
        last_kept = jnp.maximum(last_kept, cand_ref[c * R + R - 1])
    hi0 = to_key(jnp.max(gmax, axis=1, keepdims=True))
    last_kept = jnp.max(last_kept, axis=1, keepdims=True)
    n_cover = -(-topk // LANES)
    lo_f = None
    for r in range(min(R, n_cover)):
        need = -(-n_cover // (r + 1))
        if need > n_lane:
            continue
        mins = [jnp.min(cand_ref[c * R + r], axis=1, keepdims=True) for c in range(n_lane)]
        for a in range(need):
            for b2 in range(n_lane - 1, a, -1):
                hi_v, lo_v = jnp.maximum(mins[b2 - 1], mins[b2]), jnp.minimum(mins[b2 - 1], mins[b2])
                mins[b2 - 1], mins[b2] = hi_v, lo_v
        lo_f = mins[need - 1] if lo_f is None else jnp.maximum(lo_f, mins[need - 1])
    assert lo_f is not None, "topk must not exceed the number of key groups"
    lo0 = to_key(lo_f)
    few = row_pos < topk
    all_finite = to_key(jnp.full((tq, 1), -jnp.inf, F32)) + 1
    lo0 = jnp.where(few, all_finite, lo0)
    hi0 = jnp.where(few, all_finite, hi0)

    def count_ge(thr, strict=False):
        outs = []
        for r in range(n_rb):
            rows = slice(r * RB, (r + 1) * RB)
            tb = jnp.broadcast_to(thr[rows], (RB, LANES))

            def body(j, cnt, rows=rows, tb=tb):
                for c in range(n_lane):
                    st = sc_ref[j, rows, c * LANES:(c + 1) * LANES]
                    cnt = cnt + jnp.where((st > tb) if strict else (st >= tb), 1.0, 0.0)
                return cnt

            cnt = lax.fori_loop(0, ntile, body, jnp.zeros((RB, LANES), F32))
            outs.append(jnp.sum(cnt, axis=1, keepdims=True))
        return jnp.concatenate(outs, axis=0)

    def cand_count_ge(thr):
        outs = []
        for r in range(n_rb):
            rows = slice(r * RB, (r + 1) * RB)
            tb = jnp.broadcast_to(thr[rows], (RB, LANES))
            cnt = jnp.zeros((RB, LANES), F32)
            for q in range(n_cand):
                cnt = cnt + jnp.where(cand_ref[q, rows, :] >= tb, 1.0, 0.0)
            outs.append(jnp.sum(cnt, axis=1, keepdims=True))
        return jnp.concatenate(outs, axis=0)

    def cand_min_ge(thr):
        outs = []
        for r in range(n_rb):
            rows = slice(r * RB, (r + 1) * RB)
            tb = jnp.broadcast_to(thr[rows], (RB, LANES))
            cur = jnp.full((RB, LANES), jnp.inf, F32)
            for q in range(n_cand):
                ck = cand_ref[q, rows, :]
                cur = jnp.minimum(cur, jnp.where(ck >= tb, ck, jnp.inf))
            outs.append(jnp.min(cur, axis=1, keepdims=True))
        return jnp.concatenate(outs, axis=0)

    def n_open(lo, hi):
        return jnp.sum(jnp.where(lo < hi, 1.0, 0.0)).astype(jnp.int32)

    def search(count):
        def bisect(state):
            lo, hi, exact, _ = state
            x = lo ^ hi
            mid = (lo & hi) + (x >> 1) + (x & 1)
            cnt = count(from_key(mid))
            ge, eq = cnt >= kf, cnt == kf
            lo = jnp.where(ge, mid, lo)
            hi = jnp.where(eq, mid, jnp.where(ge, hi, mid - 1))
            return lo, hi, jnp.where(eq, 1, exact), n_open(lo, hi)

        lo, _, exact, _ = lax.while_loop(lambda s: s[3] > 0, bisect,
                                         (lo0, hi0, jnp.where(few, 1, 0), n_open(lo0, hi0)))
        return from_key(lo), exact

    thr, exact = search(cand_count_ge)
    thr = jnp.where(few, thr, cand_min_ge(thr))
    thr_ref[0] = thr
    thr_ref[1] = exact.astype(F32)
    n_unsure = jnp.sum(jnp.where((last_kept >= thr) & jnp.logical_not(few), 1.0, 0.0)).astype(jnp.int32)

    @pl.when(n_unsure > 0)
    def _():
        thr_full, exact_full = search(count_ge)
        thr_ref[0] = thr_full
        thr_ref[1] = exact_full.astype(F32)

    thr = thr_ref[0]
    n_tied = jnp.sum(jnp.where(thr_ref[1] > 0.0, 0.0, 1.0)).astype(jnp.int32)

    @pl.when(n_tied == 0)
    def _():
        thr_b = jnp.broadcast_to(thr, (tq, LANES))

        def write_tile(j, carry):
            st = sc_ref[j]
            parts = [jnp.where(st[:, c * LANES:(c + 1) * LANES] >= thr_b, 0.0, MASK_NEG) for c in range(n_lane)]
            bias_ref[j] = jnp.concatenate(parts, axis=1).astype(bias_ref.dtype)
            return carry

        lax.fori_loop(0, ntile, write_tile, 0)

    @pl.when(n_tied > 0)
    def _():
        need = kf - count_ge(thr, strict=True)
        before = (lax.broadcasted_iota(jnp.int32, (tk, tk), 0) < lax.broadcasted_iota(jnp.int32, (tk, tk), 1))
        before = jnp.where(before, 1.0, 0.0).astype(BF16)

        def write_tile(j, run):
            st = sc_ref[j]
            tied = jnp.where(st == thr, 1.0, 0.0)
            rank = jnp.dot(tied.astype(BF16), before, preferred_element_type=F32) + run
            sel = (st > thr) | ((st == thr) & (rank < need))
            bias_ref[j] = jnp.where(sel, 0.0, MASK_NEG).astype(bias_ref.dtype)
            return run + jnp.sum(tied, axis=1, keepdims=True)

        lax.fori_loop(0, ntile, write_tile, jnp.zeros((tq, 1), F32))

    def fill_tile(j, carry):
        bias_ref[j] = jnp.full((tq, tk), MASK_NEG, bias_ref.dtype)
        return carry

    lax.fori_loop(ntile, nT, fill_tile, 0)


def _dsa_mask(p_big, small, ikt, B, T, tq, tk, iq_off, w_off, topk):
    G = IDX_HEADS
    idim = ikt.shape[2]
    nQ, nT = T // tq, T // tk
    kern = functools.partial(_idx_kernel, topk=topk, w_off=w_off, idim=idim)
    return pl.pallas_call(
        kern,
        out_shape=jax.ShapeDtypeStruct((B, nQ, nT, tq, tk), BF16),
        grid=(B, nQ),
        in_specs=[
            pl.BlockSpec((tq, G * idim), lambda b, i: (b * nQ + i, iq_off // (G * idim))),
            pl.BlockSpec((tq, LANES), lambda b, i: (b * nQ + i, 0)),
            pl.BlockSpec((None, nT, idim, tk), lambda b, i: (b, 0, 0, 0)),
        ],
        out_specs=pl.BlockSpec((None, None, nT, tq, tk), lambda b, i: (b, i, 0, 0, 0)),
        scratch_shapes=[pltpu.VMEM((nT, tq, tk), F32),
                        pltpu.VMEM((G, tq, idim), BF16),
                        pltpu.VMEM((G, tq, LANES), F32),
                        pltpu.VMEM((tk // LANES * CAND_PER_GROUP, tq, LANES), F32),
                        pltpu.VMEM((2, tq, 1), F32)],
        compiler_params=_cparams(("parallel", "arbitrary")),
        name="dsa_index_topk",
    )(p_big, small, ikt)


def _dsa_attn_kernel(qi_ref, kj_ref, q_ref, k_ref, v_ref, bias_ref, sl_ref, o_ref, m_ref, l_ref, acc_ref, *, heads):
    p = pl.program_id(1)
    qi, kj = qi_ref[p], kj_ref[p]
    tq, tk = bias_ref.shape
    d = q_ref.shape[1] // heads
    n_lane = tk // LANES

    @pl.when(kj == 0)
    def _():
        m_ref[...] = jnp.full_like(m_ref, MASK_NEG)
        l_ref[...] = jnp.zeros_like(l_ref)
        acc_ref[...] = jnp.zeros_like(acc_ref)

    rel = (kj * tk - qi * tq) + lax.broadcasted_iota(jnp.int32, (tk, LANES), 0)
    lane = lax.broadcasted_iota(jnp.int32, (tk, LANES), 1)
    pos = jnp.where(lane < 2 * ALIBI_SPLIT, jnp.where(lane % 2 == 0, rel >> 8, rel & 255), 0)
    pos = pos.astype(F32).astype(BF16)
    bias = bias_ref[...].astype(F32)
    ones_col = jnp.where(lane == 0, 1.0, 0.0).astype(BF16)

    def masked_logits(h):
        cols = slice(h * d, (h + 1) * d)
        qa = jnp.concatenate([q_ref[:, cols], sl_ref[h]], axis=1)
        ka = jnp.concatenate([k_ref[:, cols], pos], axis=1)
        s = lax.dot_general(qa, ka, (((1,), (1,)), ((), ())), preferred_element_type=F32)
        return [s[:, c * LANES:(c + 1) * LANES] + bias[:, c * LANES:(c + 1) * LANES] for c in range(n_lane)]

    sb_next = masked_logits(0)
    for h in range(heads):
        cols = slice(h * d, (h + 1) * d)
        sb = sb_next
        if h + 1 < heads:
            sb_next = masked_logits(h + 1)
        mx = sb[0]
        for c in range(1, n_lane):
            mx = jnp.maximum(mx, sb[c])
        m_old = m_ref[h]
        m_new = jnp.maximum(m_old, jnp.max(mx, axis=1, keepdims=True))
        corr = jnp.exp2(m_old - m_new)
        ps = [jnp.exp2((sb[c] - m_new).astype(BF16)) for c in range(n_lane)]
        va = jnp.concatenate([v_ref[:, cols], ones_col], axis=1)
        pv = jnp.dot(jnp.concatenate(ps, axis=1), va, preferred_element_type=F32)
        l_ref[h] = corr * l_ref[h] + pv[:, d:]
        acc_ref[:, cols] = corr * acc_ref[:, cols] + pv[:, :d]
        m_ref[h] = m_new

    @pl.when((kj + 1) * tk >= (qi + 1) * tq)
    def _():
        for h in range(heads):
            cols = slice(h * d, (h + 1) * d)
            l = jnp.sum(l_ref[h], axis=1, keepdims=True)
            o_ref[:, cols] = (acc_ref[:, cols] / l).astype(o_ref.dtype)


def _alibi_columns(heads, tq):
    out = np.zeros((heads, LANES), np.float32)
    for h in range(heads):
        rest = np.float64(2.0 ** (-8.0 * (h + 1) / heads) * np.log2(np.e))
        for i in range(ALIBI_SPLIT):
            piece = np.float64(np.float32(rest).astype(BF16))
            out[h, 2 * i], out[h, 2 * i + 1] = 256.0 * piece, piece
            rest = rest - piece
    return jnp.asarray(np.broadcast_to(out.astype(BF16)[:, None, :], (heads, tq, LANES)))


def _dsa_attn(p_big, bias, B, T, tq, tk, q_off, width):
    nQ, nT = T // tq, T // tk
    pairs = [(i, j) for i in range(nQ) for j in range(((i + 1) * tq + tk - 1) // tk)]
    qi_tbl = jnp.asarray(np.array([p[0] for p in pairs], np.int32))
    kj_tbl = jnp.asarray(np.array([p[1] for p in pairs], np.int32))
    qb = q_off // width
    heads = DSA_HEADS
    kern = functools.partial(_dsa_attn_kernel, heads=heads)
    grid_spec = pltpu.PrefetchScalarGridSpec(
        num_scalar_prefetch=2,
        grid=(B, len(pairs)),
        in_specs=[
            pl.BlockSpec((tq, width), lambda b, p, qi, kj: (b * nQ + qi[p], qb)),
            pl.BlockSpec((tk, width), lambda b, p, qi, kj: (b * nT + kj[p], qb + 1)),
            pl.BlockSpec((tk, width), lambda b, p, qi, kj: (b * nT + kj[p], qb + 2)),
            pl.BlockSpec((None, None, None, tq, tk), lambda b, p, qi, kj: (b, qi[p], kj[p], 0, 0)),
            pl.BlockSpec((heads, tq, LANES), lambda b, p, qi, kj: (0, 0, 0)),
        ],
        out_specs=pl.BlockSpec((tq, width), lambda b, p, qi, kj: (b * nQ + qi[p], 0)),
        scratch_shapes=[pltpu.VMEM((heads, tq, LANES), F32),
                        pltpu.VMEM((heads, tq, LANES), F32),
                        pltpu.VMEM((tq, width), F32)],
    )
    return pl.pallas_call(
        kern,
        out_shape=jax.ShapeDtypeStruct((B * T, width), BF16),
        grid_spec=grid_spec,
        compiler_params=_cparams(("arbitrary", "arbitrary")),
        name="dsa_attention",
    )(qi_tbl, kj_tbl, p_big, p_big, p_big, bias, _alibi_columns(heads, tq))


def _mem_attn_kernel(q_ref, k_ref, v_ref, o_ref):
    s = lax.dot_general(q_ref[...], k_ref[...], (((1,), (1,)), ((), ())), preferred_element_type=F32)
    m = jnp.max(s, axis=1, keepdims=True)
    p = jnp.exp(s - m)
    l = jnp.sum(p, axis=1, keepdims=True)
    o = jnp.dot(p.astype(BF16), v_ref[...], preferred_element_type=F32)
    o_ref[...] = (o / l).astype(o_ref.dtype)


def _mem_attn(p_big, kv, B, T, Mt, dm, q_off, tq):
    Hm = MEM_HEADS
    nq = T // tq
    qb = q_off // dm
    return pl.pallas_call(
        _mem_attn_kernel,
        out_shape=jax.ShapeDtypeStruct((B * T, Hm * dm), BF16),
        grid=(B, nq, Hm),
        in_specs=[
            pl.BlockSpec((tq, dm), lambda b, i, h: (b * nq + i, qb + h)),
            pl.BlockSpec((Mt, dm), lambda b, i, h: (b, h)),
            pl.BlockSpec((Mt, dm), lambda b, i, h: (b, Hm + h)),
        ],
        out_specs=pl.BlockSpec((tq, dm), lambda b, i, h: (b * nq + i, h)),
        compiler_params=_cparams(("parallel", "parallel", "arbitrary")),
        name="memory_attention",
    )(p_big, kv, kv)


def _merge_kernel(a_ref, d_ref, m_ref, wa_ref, wd_ref, wm_ref, g0_ref, g1_ref, g2_ref, b_ref, o_ref, *, tn):
    D = o_ref.shape[1]

    def products(cols):
        return (jnp.dot(a_ref[...], wa_ref[:, cols], preferred_element_type=F32),
                jnp.dot(d_ref[...], wd_ref[:, cols], preferred_element_type=F32),
                jnp.dot(m_ref[...], wm_ref[:, cols], preferred_element_type=F32))

    def gated_sum(cols, ys):
        acc = None
        for r, (g_ref, y) in enumerate(zip((g0_ref, g1_ref, g2_ref), ys)):
            bias = b_ref[:, r * D + cols.start:r * D + cols.stop]
            term = jax.nn.sigmoid(g_ref[:, cols].astype(F32) + bias) * y
            acc = term if acc is None else acc + term
        o_ref[:, cols] = acc.astype(o_ref.dtype)

    chunks = [slice(j * tn, (j + 1) * tn) for j in range(D // tn)]
    pending = None
    for cols in chunks:
        ys = products(cols)
        if pending is not None:
            gated_sum(*pending)
        pending = (cols, ys)
    gated_sum(*pending)


def _merge(o_gla, o_dsa, o_mem, wa, wd, wm, p_big, b_merge, D, gate_off, tm, tn):
    M = o_gla.shape[0]
    gb = gate_off // D
    row = lambda w: pl.BlockSpec((tm, w), lambda i: (i, 0))
    resident = lambda k: pl.BlockSpec((k, D), lambda i: (0, 0), pipeline_mode=pl.Buffered(1))
    gspec = lambda r: pl.BlockSpec((tm, D), lambda i: (i, gb + r))
    return pl.pallas_call(
        functools.partial(_merge_kernel, tn=tn),
        out_shape=jax.ShapeDtypeStruct((M, D), BF16),
        grid=(M // tm,),
        in_specs=[row(o_gla.shape[1]), row(o_dsa.shape[1]), row(o_mem.shape[1]),
                  resident(wa.shape[0]), resident(wd.shape[0]), resident(wm.shape[0]),
                  gspec(0), gspec(1), gspec(2),
                  pl.BlockSpec((1, N_BRANCH * D), lambda i: (0, 0))],
        out_specs=pl.BlockSpec((tm, D), lambda i: (i, 0)),
        compiler_params=_cparams(("parallel",)),
        name="gated_merge",
    )(o_gla, o_dsa, o_mem, wa, wd, wm, p_big, p_big, p_big, b_merge)


def _layer_norm(y, g, b):
    mu = jnp.mean(y, axis=-1, keepdims=True)
    yc = y - mu
    var = jnp.mean(yc * yc, axis=-1, keepdims=True)
    return yc * lax.rsqrt(var + LN_EPS) * g + b


def _proj_ln_kernel(mg_ref, wo_ref, x_ref, g_ref, b_ref, o_ref, *, alpha):
    tm = mg_ref.shape[0]
    halves = [slice(0, tm // 2), slice(tm // 2, tm)]
    ys = [jnp.dot(mg_ref[rows, :], wo_ref[...], preferred_element_type=F32) for rows in halves]
    for rows, y in zip(halves, ys):
        o_ref[rows, :] = _layer_norm(alpha * x_ref[rows, :] + y, g_ref[...], b_ref[...])


def _proj_ln(merged, w_o, x2, g, b, alpha, tm):
    M, D = x2.shape
    vec = pl.BlockSpec((1, D), lambda i: (0, 0))
    return pl.pallas_call(
        functools.partial(_proj_ln_kernel, alpha=alpha),
        out_shape=jax.ShapeDtypeStruct((M, D), F32),
        grid=(M // tm,),
        in_specs=[pl.BlockSpec((tm, D), lambda i: (i, 0)),
                  pl.BlockSpec((D, D), lambda i: (0, 0)),
                  pl.BlockSpec((tm, D), lambda i: (i, 0)), vec, vec],
        out_specs=pl.BlockSpec((tm, D), lambda i: (i, 0)),
        compiler_params=_cparams(("parallel",)),
        name="out_proj_ln",
    )(merged, w_o, x2, g, b)


def _mlp_kernel(x_ref, wu_ref, bu_ref, wd_ref, bd_ref, g_ref, b_ref, o_ref, xb_ref, acc_ref, *, alpha):
    f = pl.program_id(1)

    @pl.when(f == 0)
    def _():
        xb_ref[...] = x_ref[...].astype(BF16)
        acc_ref[...] = jnp.zeros_like(acc_ref)

    tf = wu_ref.shape[1]
    halves = [slice(0, tf // 2), slice(tf // 2, tf)]
    ups = [jnp.dot(xb_ref[...], wu_ref[:, cols], preferred_element_type=F32) for cols in halves]
    hs = [jnp.square(jnp.maximum(u + bu_ref[:, cols], 0.0)).astype(BF16) for u, cols in zip(ups, halves)]
    downs = [jnp.dot(h, wd_ref[cols, :], preferred_element_type=F32) for h, cols in zip(hs, halves)]
    acc_ref[...] += downs[0] + downs[1]

    @pl.when(f == pl.num_programs(1) - 1)
    def _():
        y = alpha * x_ref[...] + acc_ref[...] + bd_ref[...]
        o_ref[...] = _layer_norm(y, g_ref[...], b_ref[...])


def _mlp(x1, w_up, b_up, w_down, b_down, g, b, alpha, tm, tf):
    M, D = x1.shape
    F = w_up.shape[1]
    vec = pl.BlockSpec((1, D), lambda i, f: (0, 0))
    return pl.pallas_call(
        functools.partial(_mlp_kernel, alpha=alpha),
        out_shape=jax.ShapeDtypeStruct((M, D), F32),
        grid=(M // tm, F // tf),
        in_specs=[pl.BlockSpec((tm, D), lambda i, f: (i, 0)),
                  pl.BlockSpec((D, tf), lambda i, f: (0, f)),
                  pl.BlockSpec((1, tf), lambda i, f: (0, f)),
                  pl.BlockSpec((tf, D), lambda i, f: (f, 0)),
                  vec, vec, vec],
        out_specs=pl.BlockSpec((tm, D), lambda i, f: (i, 0)),
        scratch_shapes=[pltpu.VMEM((tm, D), BF16), pltpu.VMEM((tm, D), F32)],
        compiler_params=_cparams(("parallel", "arbitrary")),
        name="mlp_ln",
    )(x1, w_up, b_up, w_down, b_down, g, b)


def _tile(n, pref):
    t = min(n, pref)
    assert n % t == 0
    return t


def _layer(x2, mem2, B, T, w_in, w_gate_up, b_gate, norm_g, w_mem_kv, w_br_gla, w_br_dsa, w_br_mem,
           b_merge, w_o, ln1_g, ln1_b, w_up, b_up, w_down, b_down, ln2_g, ln2_b, alpha):
    M, D = x2.shape
    Mt = mem2.shape[0] // B
    dk, dv = D // 8, D // 4
    dh = D // 16
    idim = D // 32
    dm = D // 8
    H, G = GLA_HEADS, IDX_HEADS
    widths = [H * dk, H * dk, H * dv, H * dv, GLA_GATE_RANK, DSA_HEADS * dh, DSA_HEADS * dh, DSA_HEADS * dh,
              G * idim, idim, G, MEM_HEADS * dm, N_BRANCH * D]
    names = ['gla_q', 'gla_k', 'gla_v', 'gla_r', 'glr', 'dsa_q', 'dsa_k', 'dsa_v', 'idx_q', 'idx_k', 'idx_w',
             'mem_q', 'gate']
    offs = dict(zip(names, np.cumsum([0] + widths[:-1]).tolist()))
    wd = dict(zip(names, widths))
    col = lambda n, scale=None: (w_in[:, offs[n]:offs[n] + wd[n]] if scale is None
                                 else w_in[:, offs[n]:offs[n] + wd[n]] * scale)
    big_parts = [('gla_q', dk ** -0.5), ('gla_k', None), ('gla_v', None), ('gla_r', None), ('gate', None),
                 ('dsa_q', dh ** -0.5 * float(np.log2(np.e))), ('dsa_k', None), ('dsa_v', None), ('idx_q', None),
                 ('mem_q', dm ** -0.5)]
    w_big = jnp.concatenate([col(n, s) for n, s in big_parts], axis=1).astype(BF16)
    boff = dict(zip([n for n, _ in big_parts], np.cumsum([0] + [wd[n] for n, _ in big_parts][:-1]).tolist()))
    n_small = wd['glr'] + wd['idx_k'] + wd['idx_w']
    w_small = jnp.concatenate([col('glr'), col('idx_k'), col('idx_w'),
                               jnp.zeros((D, LANES - n_small), w_in.dtype)], axis=1).astype(BF16)

    p_big, small = _in_proj(x2, w_big, w_small, 1024, 1024)

    o_gla = _gla(p_big, small, w_gate_up, b_gate.reshape(1, -1), norm_g.reshape(1, -1), B, T, dk, dv,
                 _tile(T, 256))

    tq, tk = _tile(T, 256), _tile(T, 512)
    topk = min(DSA_TOPK_MAX, T // 4)
    ik = small[:, wd['glr']:wd['glr'] + idim].astype(BF16)
    ikt = ik.reshape(B, T // tk, tk, idim).transpose(0, 1, 3, 2)
    bias = _dsa_mask(p_big, small, ikt, B, T, tq, tk, boff['idx_q'], wd['glr'] + idim, topk)
    o_dsa = _dsa_attn(p_big, bias, B, T, tq, tk, boff['dsa_q'], DSA_HEADS * dh)

    kv = _matmul(mem2, w_mem_kv.astype(BF16), BF16, 512, 512, "mem_kv")
    o_mem = _mem_attn(p_big, kv, B, T, Mt, dm, boff['mem_q'], _tile(T, 1024))

    merged = _merge(o_gla, o_dsa, o_mem, w_br_gla.astype(BF16), w_br_dsa.astype(BF16), w_br_mem.astype(BF16),
                    p_big, b_merge.reshape(1, -1), D, boff['gate'], _tile(M, 512), 512)
    x1 = _proj_ln(merged, w_o.astype(BF16), x2, ln1_g.reshape(1, -1), ln1_b.reshape(1, -1), alpha, _tile(M, 512))
    return _mlp(x1, w_up.astype(BF16), b_up.reshape(1, -1), w_down.astype(BF16), b_down.reshape(1, -1),
                ln2_g.reshape(1, -1), ln2_b.reshape(1, -1), alpha, _tile(M, 512), 1024)


def kernel(x, mem, w_in, w_gla_gate_up, b_gla_gate, gla_norm_g, w_mem_kv, w_br_gla, w_br_dsa, w_br_mem,
           b_merge, w_o, ln1_g, ln1_b, w_up, b_up, w_down, b_down, ln2_g, ln2_b):
    B, T, D = x.shape
    depth = w_in.shape[0]
    alpha = (2 * depth) ** 0.25
    x2 = x.reshape(B * T, D)
    mem2 = mem.reshape(-1, D)
    for l in range(depth):
        x2 = _layer(x2, mem2, B, T, w_in[l], w_gla_gate_up[l], b_gla_gate[l], gla_norm_g[l], w_mem_kv[l],
                    w_br_gla[l], w_br_dsa[l], w_br_mem[l], b_merge[l], w_o[l], ln1_g[l], ln1_b[l],
                    w_up[l], b_up[l], w_down[l], b_down[l], ln2_g[l], ln2_b[l], alpha)
    return x2.reshape(B, T, D)
```

```python
import functools

import numpy as np
import jax
import jax.numpy as jnp
from jax import lax
from jax.experimental import pallas as pl
from jax.experimental.pallas import tpu as pltpu

F32 = jnp.float32
BF16 = jnp.bfloat16

GLA_HEADS = 4
GLA_GATE_RANK = 16
GLA_GATE_TEMP = 16.0
GLA_CHUNK = 64
DSA_HEADS = 8
IDX_HEADS = 16
DSA_TOPK_MAX = 256
MEM_HEADS = 4
N_BRANCH = 3
LN_EPS = 1e-5
RMS_EPS = 1e-6

V7X_VMEM_LIMIT_BYTES = 56 * 1024 * 1024
LANES = 128
MASK_NEG = -1e30
SCORE_SUB_ROWS, SCORE_SUB_COLS = 128, 256
CAND_PER_GROUP = 7
ALIBI_SPLIT = 3


def _cparams(sem):
    return pltpu.CompilerParams(dimension_semantics=sem, vmem_limit_bytes=V7X_VMEM_LIMIT_BYTES)


def _mm_cast_kernel(x_ref, w_ref, o_ref, xb_ref):
    @pl.when(pl.program_id(1) == 0)
    def _():
        xb_ref[...] = x_ref[...].astype(BF16)

    o_ref[...] = jnp.dot(xb_ref[...], w_ref[...], preferred_element_type=F32).astype(o_ref.dtype)


def _matmul(x, w, out_dtype, tm, tn, name):
    M, K = x.shape
    N = w.shape[1]
    tm, tn = min(tm, M), min(tn, N)
    assert M % tm == 0 and N % tn == 0
    return pl.pallas_call(
        _mm_cast_kernel,
        out_shape=jax.ShapeDtypeStruct((M, N), out_dtype),
        grid=(M // tm, N // tn),
        in_specs=[pl.BlockSpec((tm, K), lambda i, j: (i, 0)),
                  pl.BlockSpec((K, tn), lambda i, j: (0, j))],
        out_specs=pl.BlockSpec((tm, tn), lambda i, j: (i, j)),
        scratch_shapes=[pltpu.VMEM((tm, K), BF16)],
        compiler_params=_cparams(("parallel", "arbitrary")),
        name=name,
    )(x, w)


def _in_proj_kernel(x_ref, w_ref, ws_ref, o_ref, os_ref, xb_ref):
    @pl.when(pl.program_id(1) == 0)
    def _():
        xb_ref[...] = x_ref[...].astype(BF16)
        os_ref[...] = jnp.dot(xb_ref[...], ws_ref[...], preferred_element_type=F32)

    o_ref[...] = jnp.dot(xb_ref[...], w_ref[...], preferred_element_type=F32).astype(o_ref.dtype)


def _in_proj(x, w_big, w_small, tm, tn):
    M, K = x.shape
    N, Ns = w_big.shape[1], w_small.shape[1]
    tm, tn = min(tm, M), min(tn, N)
    assert M % tm == 0 and N % tn == 0
    return pl.pallas_call(
        _in_proj_kernel,
        out_shape=(jax.ShapeDtypeStruct((M, N), BF16), jax.ShapeDtypeStruct((M, Ns), F32)),
        grid=(M // tm, N // tn),
        in_specs=[pl.BlockSpec((tm, K), lambda i, j: (i, 0)),
                  pl.BlockSpec((K, tn), lambda i, j: (0, j)),
                  pl.BlockSpec((K, Ns), lambda i, j: (0, 0))],
        out_specs=(pl.BlockSpec((tm, tn), lambda i, j: (i, j)),
                   pl.BlockSpec((tm, Ns), lambda i, j: (i, 0))),
        scratch_shapes=[pltpu.VMEM((tm, K), BF16)],
        compiler_params=_cparams(("parallel", "arbitrary")),
        name="in_proj",
    )(x, w_big, w_small)


def _log_sigmoid(z):
    return jnp.minimum(z, 0.0) - jnp.log(1.0 + jnp.exp(-jnp.abs(z)))


def _gla_kernel(q_ref, k_ref, v_ref, r_ref, sm_ref, wg_ref, bg_ref, ng_ref, o_ref, st_ref, *, chunk, rank):
    @pl.when(pl.program_id(2) == 0)
    def _():
        st_ref[...] = jnp.zeros_like(st_ref)

    tb, dk = q_ref.shape
    C = chunk
    glr = sm_ref[:, 0:rank]
    z = jnp.dot(glr, wg_ref[...], preferred_element_type=F32) + bg_ref[...]
    la = _log_sigmoid(z) * (1.0 / GLA_GATE_TEMP)
    causal = (lax.broadcasted_iota(jnp.int32, (C, C), 1) <= lax.broadcasted_iota(jnp.int32, (C, C), 0))
    tri = jnp.where(causal, 1.0, 0.0).astype(F32)

    n_chunk = tb // C
    chunk_rows = [slice(c * C, (c + 1) * C) for c in range(n_chunk)]
    bcs = [jnp.dot(tri, la[rows], preferred_element_type=F32, precision=lax.Precision.HIGHEST)
           for rows in chunk_rows]
    scs, q_its, upds, decays = [], [], [], []
    for c, rows in enumerate(chunk_rows):
        bc = bcs[c]
        br = bc[C // 2:C // 2 + 1]
        bl = bc[C - 1:C]
        qc = q_ref[rows, :].astype(F32)
        kc = k_ref[rows, :].astype(F32)
        q_in = (qc * jnp.exp(bc - br)).astype(BF16)
        k_in = (kc * jnp.exp(br - bc)).astype(BF16)
        scs.append(lax.dot_general(q_in, k_in, (((1,), (1,)), ((), ())), preferred_element_type=F32))
        q_its.append((qc * jnp.exp(bc)).astype(BF16))
        k_st = (kc * jnp.exp(bl - bc)).astype(BF16)
        upds.append(lax.dot_general(v_ref[rows, :], k_st, (((0,), (0,)), ((), ())),
                                    preferred_element_type=F32))
        decays.append(jnp.exp(bl))
    o_intra = [jnp.dot(jnp.where(causal, scs[c], 0.0).astype(BF16), v_ref[rows, :], preferred_element_type=F32)
               for c, rows in enumerate(chunk_rows)]

    for c, rows in enumerate(chunk_rows):
        st = st_ref[...]
        o = o_intra[c] + lax.dot_general(q_its[c], st.astype(BF16), (((1,), (1,)), ((), ())),
                                         preferred_element_type=F32)
        st_ref[...] = st * decays[c] + upds[c]
        o = o * lax.rsqrt(jnp.mean(o * o, axis=-1, keepdims=True) + RMS_EPS)
        r = r_ref[rows, :].astype(F32)
        o = o * ng_ref[...] * (r / (1.0 + jnp.exp(-r)))
        o_ref[rows, :] = o.astype(o_ref.dtype)


def _gla(p_big, small, w_gate_up, b_gate, norm_g, B, T, dk, dv, tb):
    M = B * T
    H = GLA_HEADS
    nb = T // tb
    row = lambda b, h, n: b * nb + n
    kern = functools.partial(_gla_kernel, chunk=GLA_CHUNK, rank=GLA_GATE_RANK)
    koff = (H * dk) // dk
    voff = (2 * H * dk) // dv
    roff = (2 * H * dk + H * dv) // dv
    return pl.pallas_call(
        kern,
        out_shape=jax.ShapeDtypeStruct((M, H * dv), BF16),
        grid=(B, H, nb),
        in_specs=[
            pl.BlockSpec((tb, dk), lambda b, h, n: (row(b, h, n), h)),
            pl.BlockSpec((tb, dk), lambda b, h, n: (row(b, h, n), koff + h)),
            pl.BlockSpec((tb, dv), lambda b, h, n: (row(b, h, n), voff + h)),
            pl.BlockSpec((tb, dv), lambda b, h, n: (row(b, h, n), roff + h)),
            pl.BlockSpec((tb, LANES), lambda b, h, n: (row(b, h, n), 0)),
            pl.BlockSpec((GLA_GATE_RANK, dk), lambda b, h, n: (0, h)),
            pl.BlockSpec((1, dk), lambda b, h, n: (0, h)),
            pl.BlockSpec((1, dv), lambda b, h, n: (0, h)),
        ],
        out_specs=pl.BlockSpec((tb, dv), lambda b, h, n: (row(b, h, n), h)),
        scratch_shapes=[pltpu.VMEM((dv, dk), F32)],
        compiler_params=_cparams(("parallel", "parallel", "arbitrary")),
        name="gla",
    )(p_big, p_big, p_big, p_big, small, w_gate_up, b_gate, norm_g)


def _idx_kernel(iq_ref, sm_ref, ikt_ref, bias_ref, sc_ref, iqs_ref, wb_ref, cand_ref, thr_ref, *, topk, w_off, idim):
    i = pl.program_id(1)
    nT, tq, tk = sc_ref.shape
    G = iqs_ref.shape[0]
    n_lane = tk // LANES
    n_cand = cand_ref.shape[0]
    R = n_cand // n_lane
    ntile = ((i + 1) * tq + tk - 1) // tk
    kf = float(topk)
    RB = min(tq, 128)
    n_rb = tq // RB

    w = sm_ref[:, w_off:w_off + G] * ((G ** -0.5) * (idim ** -0.5))
    for g in range(G):
        iqs_ref[g] = iq_ref[:, g * idim:(g + 1) * idim]
        wb_ref[g] = jnp.broadcast_to(w[:, g:g + 1], (tq, LANES))
    row_pos = i * tq + lax.broadcasted_iota(jnp.int32, (tq, 1), 0)
    cand_ref[...] = jnp.full(cand_ref.shape, -jnp.inf, F32)

    sub_rows, sub_cols = min(tq, SCORE_SUB_ROWS), min(tk, SCORE_SUB_COLS)

    def score_tile(j, carry):
        col_pos = j * tk + lax.broadcasted_iota(jnp.int32, (1, tk), 1)
        for rh in range(tq // sub_rows):
            rows = slice(rh * sub_rows, (rh + 1) * sub_rows)
            for lh in range(tk // sub_cols):
                cols = slice(lh * sub_cols, (lh + 1) * sub_cols)
                ik = ikt_ref[j, :, cols]
                acc = jnp.zeros((sub_rows, sub_cols), F32)
                for g in range(G):
                    a = jnp.dot(iqs_ref[g, rows, :], ik, preferred_element_type=F32)
                    wg = jnp.concatenate([wb_ref[g, rows, :]] * (sub_cols // LANES), axis=1)
                    acc = acc + wg * jnp.maximum(a, 0.0)
                sc = jnp.where(col_pos[:, cols] <= row_pos[rows], acc, -jnp.inf)
                sc_ref[j, rows, cols] = sc
                for cc in range(sub_cols // LANES):
                    c = lh * (sub_cols // LANES) + cc
                    x = sc[:, cc * LANES:(cc + 1) * LANES]
                    for r in range(R):
                        cur = cand_ref[c * R + r, rows, :]
                        cand_ref[c * R + r, rows, :] = jnp.maximum(cur, x)
                        x = jnp.minimum(cur, x)
        return carry

    lax.fori_loop(0, ntile, score_tile, 0)

    def to_key(v):
        bits = pltpu.bitcast(v, jnp.int32)
        return bits ^ ((bits >> 31) & 0x7FFFFFFF)

    def from_key(k):
        return pltpu.bitcast(k ^ ((k >> 31) & 0x7FFFFFFF), F32)

    gmax, last_kept = cand_ref[0], cand_ref[R - 1]
    for c in range(1, n_lane):
        gmax = jnp.maximum(gmax, cand_ref[c * R])
        last_kept = jnp.maximum(last_kept, cand_ref[c * R + R - 1])
    hi0 = to_key(jnp.max(gmax, axis=1, keepdims=True))
    last_kept = jnp.max(last_kept, axis=1, keepdims=True)
    n_cover = -(-topk // LANES)
    lo_f = None
    for r in range(min(R, n_cover)):
        need = -(-n_cover // (r + 1))
        if need > n_lane:
            continue
        mins = [jnp.min(cand_ref[c * R + r], axis=1, keepdims=True) for c in range(n_lane)]
        for a in range(need):
            for b2 in range(n_lane - 1, a, -1):
                hi_v, lo_v = jnp.maximum(mins[b2 - 1], mins[b2]), jnp.minimum(mins[b2 - 1], mins[b2])
                mins[b2 - 1], mins[b2] = hi_v, lo_v
        lo_f = mins[need - 1] if lo_f is None else jnp.maximum(lo_f, mins[need - 1])
    assert lo_f is not None, "topk must not exceed the number of key groups"
    lo0 = to_key(lo_f)
    few = row_pos < topk
    all_finite = to_key(jnp.full((tq, 1), -jnp.inf, F32)) + 1
    lo0 = jnp.where(few, all_finite, lo0)
    hi0 = jnp.where(few, all_finite, hi0)

    def count_ge(thr, strict=False):
        outs = []
        for r in range(n_rb):
            rows = slice(r * RB, (r + 1) * RB)
            tb = jnp.broadcast_to(thr[rows], (RB, LANES))

            def body(j, cnt, rows=rows, tb=tb):
                for c in range(n_lane):
                    st = sc_ref[j, rows, c * LANES:(c + 1) * LANES]
                    cnt = cnt + jnp.where((st > tb) if strict else (st >= tb), 1.0, 0.0)
                return cnt

            cnt = lax.fori_loop(0, ntile, body, jnp.zeros((RB, LANES), F32))
            outs.append(jnp.sum(cnt, axis=1, keepdims=True))
        return jnp.concatenate(outs, axis=0)

    def sorted_count_ge(vals, tb):
        if not vals:
            return None
        mid = len(vals) // 2
        hit = vals[mid] >= tb
        base = jnp.where(hit, float(mid + 1), 0.0)
        rest = sorted_count_ge([jnp.where(hit, lo_v, hi_v) for hi_v, lo_v in zip(vals[:mid], vals[mid + 1:])], tb)
        return base if rest is None else base + rest

    def cand_count_ge(thr):
        assert (R + 1) & R == 0, "candidate lists must have 2**d - 1 entries"
        outs = []
        for r in range(n_rb):
            rows = slice(r * RB, (r + 1) * RB)
            tb = jnp.broadcast_to(thr[rows], (RB, LANES))
            cnt = jnp.zeros((RB, LANES), F32)
            for c in range(n_lane):
                cnt = cnt + sorted_count_ge([cand_ref[c * R + q, rows, :] for q in range(R)], tb)
            outs.append(jnp.sum(cnt, axis=1, keepdims=True))
        return jnp.concatenate(outs, axis=0)

    def cand_min_ge(thr):
        outs = []
        for r in range(n_rb):
            rows = slice(r * RB, (r + 1) * RB)
            tb = jnp.broadcast_to(thr[rows], (RB, LANES))
            cur = jnp.full((RB, LANES), jnp.inf, F32)
            for q in range(n_cand):
                ck = cand_ref[q, rows, :]
                cur = jnp.minimum(cur, jnp.where(ck >= tb, ck, jnp.inf))
            outs.append(jnp.min(cur, axis=1, keepdims=True))
        return jnp.concatenate(outs, axis=0)

    def n_open(lo, hi):
        return jnp.sum(jnp.where(lo < hi, 1.0, 0.0)).astype(jnp.int32)

    def search(count):
        def bisect(state):
            lo, hi, exact, _ = state
            x = lo ^ hi
            mid = (lo & hi) + (x >> 1) + (x & 1)
            cnt = count(from_key(mid))
            ge, eq = cnt >= kf, cnt == kf
            lo = jnp.where(ge, mid, lo)
            hi = jnp.where(eq, mid, jnp.where(ge, hi, mid - 1))
            return lo, hi, jnp.where(eq, 1, exact), n_open(lo, hi)

        lo, _, exact, _ = lax.while_loop(lambda s: s[3] > 0, bisect,
                                         (lo0, hi0, jnp.where(few, 1, 0), n_open(lo0, hi0)))
        return from_key(lo), exact

    thr, exact = search(cand_count_ge)
    thr = jnp.where(few, thr, cand_min_ge(thr))
    thr_ref[0] = thr
    thr_ref[1] = exact.astype(F32)
    n_unsure = jnp.sum(jnp.where((last_kept >= thr) & jnp.logical_not(few), 1.0, 0.0)).astype(jnp.int32)

    @pl.when(n_unsure > 0)
    def _():
        thr_full, exact_full = search(count_ge)
        thr_ref[0] = thr_full
        thr_ref[1] = exact_full.astype(F32)

    thr = thr_ref[0]
    n_tied = jnp.sum(jnp.where(thr_ref[1] > 0.0, 0.0, 1.0)).astype(jnp.int32)

    @pl.when(n_tied == 0)
    def _():
        thr_b = jnp.broadcast_to(thr, (tq, LANES))

        def write_tile(j, carry):
            st = sc_ref[j]
            parts = [jnp.where(st[:, c * LANES:(c + 1) * LANES] >= thr_b, 0.0, MASK_NEG) for c in range(n_lane)]
            bias_ref[j] = jnp.concatenate(parts, axis=1).astype(bias_ref.dtype)
            return carry

        lax.fori_loop(0, ntile, write_tile, 0)

    @pl.when(n_tied > 0)
    def _():
        need = kf - count_ge(thr, strict=True)
        before = (lax.broadcasted_iota(jnp.int32, (tk, tk), 0) < lax.broadcasted_iota(jnp.int32, (tk, tk), 1))
        before = jnp.where(before, 1.0, 0.0).astype(BF16)

        def write_tile(j, run):
            st = sc_ref[j]
            tied = jnp.where(st == thr, 1.0, 0.0)
            rank = jnp.dot(tied.astype(BF16), before, preferred_element_type=F32) + run
            sel = (st > thr) | ((st == thr) & (rank < need))
            bias_ref[j] = jnp.where(sel, 0.0, MASK_NEG).astype(bias_ref.dtype)
            return run + jnp.sum(tied, axis=1, keepdims=True)

        lax.fori_loop(0, ntile, write_tile, jnp.zeros((tq, 1), F32))

    def fill_tile(j, carry):
        bias_ref[j] = jnp.full((tq, tk), MASK_NEG, bias_ref.dtype)
        return carry

    lax.fori_loop(ntile, nT, fill_tile, 0)


def _dsa_mask(p_big, small, ikt, B, T, tq, tk, iq_off, w_off, topk):
    G = IDX_HEADS
    idim = ikt.shape[2]
    nQ, nT = T // tq, T // tk
    kern = functools.partial(_idx_kernel, topk=topk, w_off=w_off, idim=idim)
    return pl.pallas_call(
        kern,
        out_shape=jax.ShapeDtypeStruct((B, nQ, nT, tq, tk), BF16),
        grid=(B, nQ),
        in_specs=[
            pl.BlockSpec((tq, G * idim), lambda b, i: (b * nQ + i, iq_off // (G * idim))),
            pl.BlockSpec((tq, LANES), lambda b, i: (b * nQ + i, 0)),
            pl.BlockSpec((None, nT, idim, tk), lambda b, i: (b, 0, 0, 0)),
        ],
        out_specs=pl.BlockSpec((None, None, nT, tq, tk), lambda b, i: (b, i, 0, 0, 0)),
        scratch_shapes=[pltpu.VMEM((nT, tq, tk), F32),
                        pltpu.VMEM((G, tq, idim), BF16),
                        pltpu.VMEM((G, tq, LANES), F32),
                        pltpu.VMEM((tk // LANES * CAND_PER_GROUP, tq, LANES), F32),
                        pltpu.VMEM((2, tq, 1), F32)],
        compiler_params=_cparams(("parallel", "arbitrary")),
        name="dsa_index_topk",
    )(p_big, small, ikt)


def _dsa_attn_kernel(qi_ref, kj_ref, q_ref, k_ref, v_ref, bias_ref, sl_ref, o_ref, m_ref, l_ref, acc_ref, *, heads):
    p = pl.program_id(1)
    qi, kj = qi_ref[p], kj_ref[p]
    tq, tk = bias_ref.shape
    d = q_ref.shape[1] // heads
    n_lane = tk // LANES

    @pl.when(kj == 0)
    def _():
        m_ref[...] = jnp.full_like(m_ref, MASK_NEG)
        l_ref[...] = jnp.zeros_like(l_ref)
        acc_ref[...] = jnp.zeros_like(acc_ref)

    rel = (kj * tk - qi * tq) + lax.broadcasted_iota(jnp.int32, (tk, LANES), 0)
    lane = lax.broadcasted_iota(jnp.int32, (tk, LANES), 1)
    pos = jnp.where(lane < 2 * ALIBI_SPLIT, jnp.where(lane % 2 == 0, rel >> 8, rel & 255), 0)
    pos = pos.astype(F32).astype(BF16)
    bias = bias_ref[...].astype(F32)
    ones_col = jnp.where(lane == 0, 1.0, 0.0).astype(BF16)

    def masked_logits(h):
        cols = slice(h * d, (h + 1) * d)
        qa = jnp.concatenate([q_ref[:, cols], sl_ref[h]], axis=1)
        ka = jnp.concatenate([k_ref[:, cols], pos], axis=1)
        s = lax.dot_general(qa, ka, (((1,), (1,)), ((), ())), preferred_element_type=F32)
        return [s[:, c * LANES:(c + 1) * LANES] + bias[:, c * LANES:(c + 1) * LANES] for c in range(n_lane)]

    sb_next = masked_logits(0)
    for h in range(heads):
        cols = slice(h * d, (h + 1) * d)
        sb = sb_next
        if h + 1 < heads:
            sb_next = masked_logits(h + 1)
        mx = sb[0]
        for c in range(1, n_lane):
            mx = jnp.maximum(mx, sb[c])
        m_old = m_ref[h]
        m_new = jnp.maximum(m_old, jnp.max(mx, axis=1, keepdims=True))
        corr = jnp.exp2(m_old - m_new)
        ps = [jnp.exp2((sb[c] - m_new).astype(BF16)) for c in range(n_lane)]
        va = jnp.concatenate([v_ref[:, cols], ones_col], axis=1)
        pv = jnp.dot(jnp.concatenate(ps, axis=1), va, preferred_element_type=F32)
        l_ref[h] = corr * l_ref[h] + pv[:, d:]
        acc_ref[:, cols] = corr * acc_ref[:, cols] + pv[:, :d]
        m_ref[h] = m_new

    @pl.when((kj + 1) * tk >= (qi + 1) * tq)
    def _():
        for h in range(heads):
            cols = slice(h * d, (h + 1) * d)
            l = jnp.sum(l_ref[h], axis=1, keepdims=True)
            o_ref[:, cols] = (acc_ref[:, cols] / l).astype(o_ref.dtype)


def _alibi_columns(heads, tq):
    out = np.zeros((heads, LANES), np.float32)
    for h in range(heads):
        rest = np.float64(2.0 ** (-8.0 * (h + 1) / heads) * np.log2(np.e))
        for i in range(ALIBI_SPLIT):
            piece = np.float64(np.float32(rest).astype(BF16))
            out[h, 2 * i], out[h, 2 * i + 1] = 256.0 * piece, piece
            rest = rest - piece
    return jnp.asarray(np.broadcast_to(out.astype(BF16)[:, None, :], (heads, tq, LANES)))


def _dsa_attn(p_big, bias, B, T, tq, tk, q_off, width):
    nQ, nT = T // tq, T // tk
    pairs = [(i, j) for i in range(nQ) for j in range(((i + 1) * tq + tk - 1) // tk)]
    qi_tbl = jnp.asarray(np.array([p[0] for p in pairs], np.int32))
    kj_tbl = jnp.asarray(np.array([p[1] for p in pairs], np.int32))
    qb = q_off // width
    heads = DSA_HEADS
    kern = functools.partial(_dsa_attn_kernel, heads=heads)
    grid_spec = pltpu.PrefetchScalarGridSpec(
        num_scalar_prefetch=2,
        grid=(B, len(pairs)),
        in_specs=[
            pl.BlockSpec((tq, width), lambda b, p, qi, kj: (b * nQ + qi[p], qb)),
            pl.BlockSpec((tk, width), lambda b, p, qi, kj: (b * nT + kj[p], qb + 1)),
            pl.BlockSpec((tk, width), lambda b, p, qi, kj: (b * nT + kj[p], qb + 2)),
            pl.BlockSpec((None, None, None, tq, tk), lambda b, p, qi, kj: (b, qi[p], kj[p], 0, 0)),
            pl.BlockSpec((heads, tq, LANES), lambda b, p, qi, kj: (0, 0, 0)),
        ],
        out_specs=pl.BlockSpec((tq, width), lambda b, p, qi, kj: (b * nQ + qi[p], 0)),
        scratch_shapes=[pltpu.VMEM((heads, tq, LANES), F32),
                        pltpu.VMEM((heads, tq, LANES), F32),
                        pltpu.VMEM((tq, width), F32)],
    )
    return pl.pallas_call(
        kern,
        out_shape=jax.ShapeDtypeStruct((B * T, width), BF16),
        grid_spec=grid_spec,
        compiler_params=_cparams(("arbitrary", "arbitrary")),
        name="dsa_attention",
    )(qi_tbl, kj_tbl, p_big, p_big, p_big, bias, _alibi_columns(heads, tq))


def _mem_attn_kernel(q_ref, k_ref, v_ref, o_ref):
    s = lax.dot_general(q_ref[...], k_ref[...], (((1,), (1,)), ((), ())), preferred_element_type=F32)
    m = jnp.max(s, axis=1, keepdims=True)
    p = jnp.exp(s - m)
    l = jnp.sum(p, axis=1, keepdims=True)
    o = jnp.dot(p.astype(BF16), v_ref[...], preferred_element_type=F32)
    o_ref[...] = (o / l).astype(o_ref.dtype)


def _mem_attn(p_big, kv, B, T, Mt, dm, q_off, tq):
    Hm = MEM_HEADS
    nq = T // tq
    qb = q_off // dm
    return pl.pallas_call(
        _mem_attn_kernel,
        out_shape=jax.ShapeDtypeStruct((B * T, Hm * dm), BF16),
        grid=(B, nq, Hm),
        in_specs=[
            pl.BlockSpec((tq, dm), lambda b, i, h: (b * nq + i, qb + h)),
            pl.BlockSpec((Mt, dm), lambda b, i, h: (b, h)),
            pl.BlockSpec((Mt, dm), lambda b, i, h: (b, Hm + h)),
        ],
        out_specs=pl.BlockSpec((tq, dm), lambda b, i, h: (b * nq + i, h)),
        compiler_params=_cparams(("parallel", "parallel", "arbitrary")),
        name="memory_attention",
    )(p_big, kv, kv)


def _merge_kernel(a_ref, d_ref, m_ref, wa_ref, wd_ref, wm_ref, g0_ref, g1_ref, g2_ref, b_ref, o_ref, *, tn):
    D = o_ref.shape[1]

    def products(cols):
        return (jnp.dot(a_ref[...], wa_ref[:, cols], preferred_element_type=F32),
                jnp.dot(d_ref[...], wd_ref[:, cols], preferred_element_type=F32),
                jnp.dot(m_ref[...], wm_ref[:, cols], preferred_element_type=F32))

    def gated_sum(cols, ys):
        acc = None
        for r, (g_ref, y) in enumerate(zip((g0_ref, g1_ref, g2_ref), ys)):
            bias = b_ref[:, r * D + cols.start:r * D + cols.stop]
            term = jax.nn.sigmoid(g_ref[:, cols].astype(F32) + bias) * y
            acc = term if acc is None else acc + term
        o_ref[:, cols] = acc.astype(o_ref.dtype)

    chunks = [slice(j * tn, (j + 1) * tn) for j in range(D // tn)]
    pending = None
    for cols in chunks:
        ys = products(cols)
        if pending is not None:
            gated_sum(*pending)
        pending = (cols, ys)
    gated_sum(*pending)


def _merge(o_gla, o_dsa, o_mem, wa, wd, wm, p_big, b_merge, D, gate_off, tm, tn):
    M = o_gla.shape[0]
    gb = gate_off // D
    row = lambda w: pl.BlockSpec((tm, w), lambda i: (i, 0))
    resident = lambda k: pl.BlockSpec((k, D), lambda i: (0, 0), pipeline_mode=pl.Buffered(1))
    gspec = lambda r: pl.BlockSpec((tm, D), lambda i: (i, gb + r))
    return pl.pallas_call(
        functools.partial(_merge_kernel, tn=tn),
        out_shape=jax.ShapeDtypeStruct((M, D), BF16),
        grid=(M // tm,),
        in_specs=[row(o_gla.shape[1]), row(o_dsa.shape[1]), row(o_mem.shape[1]),
                  resident(wa.shape[0]), resident(wd.shape[0]), resident(wm.shape[0]),
                  gspec(0), gspec(1), gspec(2),
                  pl.BlockSpec((1, N_BRANCH * D), lambda i: (0, 0))],
        out_specs=pl.BlockSpec((tm, D), lambda i: (i, 0)),
        compiler_params=_cparams(("parallel",)),
        name="gated_merge",
    )(o_gla, o_dsa, o_mem, wa, wd, wm, p_big, p_big, p_big, b_merge)


def _layer_norm(y, g, b):
    mu = jnp.mean(y, axis=-1, keepdims=True)
    yc = y - mu
    var = jnp.mean(yc * yc, axis=-1, keepdims=True)
    return yc * lax.rsqrt(var + LN_EPS) * g + b


def _proj_ln_kernel(mg_ref, wo_ref, x_ref, g_ref, b_ref, o_ref, *, alpha):
    tm = mg_ref.shape[0]
    halves = [slice(0, tm // 2), slice(tm // 2, tm)]
    ys = [jnp.dot(mg_ref[rows, :], wo_ref[...], preferred_element_type=F32) for rows in halves]
    for rows, y in zip(halves, ys):
        o_ref[rows, :] = _layer_norm(alpha * x_ref[rows, :] + y, g_ref[...], b_ref[...])


def _proj_ln(merged, w_o, x2, g, b, alpha, tm):
    M, D = x2.shape
    vec = pl.BlockSpec((1, D), lambda i: (0, 0))
    return pl.pallas_call(
        functools.partial(_proj_ln_kernel, alpha=alpha),
        out_shape=jax.ShapeDtypeStruct((M, D), F32),
        grid=(M // tm,),
        in_specs=[pl.BlockSpec((tm, D), lambda i: (i, 0)),
                  pl.BlockSpec((D, D), lambda i: (0, 0)),
                  pl.BlockSpec((tm, D), lambda i: (i, 0)), vec, vec],
        out_specs=pl.BlockSpec((tm, D), lambda i: (i, 0)),
        compiler_params=_cparams(("parallel",)),
        name="out_proj_ln",
    )(merged, w_o, x2, g, b)


def _mlp_kernel(x_ref, wu_ref, bu_ref, wd_ref, bd_ref, g_ref, b_ref, o_ref, xb_ref, acc_ref, *, alpha):
    f = pl.program_id(1)

    @pl.when(f == 0)
    def _():
        xb_ref[...] = x_ref[...].astype(BF16)
        acc_ref[...] = jnp.zeros_like(acc_ref)

    tf = wu_ref.shape[1]
    halves = [slice(0, tf // 2), slice(tf // 2, tf)]
    ups = [jnp.dot(xb_ref[...], wu_ref[:, cols], preferred_element_type=F32) for cols in halves]
    hs = [jnp.square(jnp.maximum(u + bu_ref[:, cols], 0.0)).astype(BF16) for u, cols in zip(ups, halves)]
    downs = [jnp.dot(h, wd_ref[cols, :], preferred_element_type=F32) for h, cols in zip(hs, halves)]
    acc_ref[...] += downs[0] + downs[1]

    @pl.when(f == pl.num_programs(1) - 1)
    def _():
        y = alpha * x_ref[...] + acc_ref[...] + bd_ref[...]
        o_ref[...] = _layer_norm(y, g_ref[...], b_ref[...])


def _mlp(x1, w_up, b_up, w_down, b_down, g, b, alpha, tm, tf):
    M, D = x1.shape
    F = w_up.shape[1]
    vec = pl.BlockSpec((1, D), lambda i, f: (0, 0))
    return pl.pallas_call(
        functools.partial(_mlp_kernel, alpha=alpha),
        out_shape=jax.ShapeDtypeStruct((M, D), F32),
        grid=(M // tm, F // tf),
        in_specs=[pl.BlockSpec((tm, D), lambda i, f: (i, 0)),
                  pl.BlockSpec((D, tf), lambda i, f: (0, f)),
                  pl.BlockSpec((1, tf), lambda i, f: (0, f)),
                  pl.BlockSpec((tf, D), lambda i, f: (f, 0)),
                  vec, vec, vec],
        out_specs=pl.BlockSpec((tm, D), lambda i, f: (i, 0)),
        scratch_shapes=[pltpu.VMEM((tm, D), BF16), pltpu.VMEM((tm, D), F32)],
        compiler_params=_cparams(("parallel", "arbitrary")),
        name="mlp_ln",
    )(x1, w_up, b_up, w_down, b_down, g, b)


def _tile(n, pref):
    t = min(n, pref)
    assert n % t == 0
    return t


def _layer(x2, mem2, B, T, w_in, w_gate_up, b_gate, norm_g, w_mem_kv, w_br_gla, w_br_dsa, w_br_mem,
           b_merge, w_o, ln1_g, ln1_b, w_up, b_up, w_down, b_down, ln2_g, ln2_b, alpha):
    M, D = x2.shape
    Mt = mem2.shape[0] // B
    dk, dv = D // 8, D // 4
    dh = D // 16
    idim = D // 32
    dm = D // 8
    H, G = GLA_HEADS, IDX_HEADS
    widths = [H * dk, H * dk, H * dv, H * dv, GLA_GATE_RANK, DSA_HEADS * dh, DSA_HEADS * dh, DSA_HEADS * dh,
              G * idim, idim, G, MEM_HEADS * dm, N_BRANCH * D]
    names = ['gla_q', 'gla_k', 'gla_v', 'gla_r', 'glr', 'dsa_q', 'dsa_k', 'dsa_v', 'idx_q', 'idx_k', 'idx_w',
             'mem_q', 'gate']
    offs = dict(zip(names, np.cumsum([0] + widths[:-1]).tolist()))
    wd = dict(zip(names, widths))
    col = lambda n, scale=None: (w_in[:, offs[n]:offs[n] + wd[n]] if scale is None
                                 else w_in[:, offs[n]:offs[n] + wd[n]] * scale)
    big_parts = [('gla_q', dk ** -0.5), ('gla_k', None), ('gla_v', None), ('gla_r', None), ('gate', None),
                 ('dsa_q', dh ** -0.5 * float(np.log2(np.e))), ('dsa_k', None), ('dsa_v', None), ('idx_q', None),
                 ('mem_q', dm ** -0.5)]
    w_big = jnp.concatenate([col(n, s) for n, s in big_parts], axis=1).astype(BF16)
    boff = dict(zip([n for n, _ in big_parts], np.cumsum([0] + [wd[n] for n, _ in big_parts][:-1]).tolist()))
    n_small = wd['glr'] + wd['idx_k'] + wd['idx_w']
    w_small = jnp.concatenate([col('glr'), col('idx_k'), col('idx_w'),
                               jnp.zeros((D, LANES - n_small), w_in.dtype)], axis=1).astype(BF16)

    p_big, small = _in_proj(x2, w_big, w_small, 1024, 1024)

    o_gla = _gla(p_big, small, w_gate_up, b_gate.reshape(1, -1), norm_g.reshape(1, -1), B, T, dk, dv,
                 _tile(T, 256))

    tq, tk = _tile(T, 256), _tile(T, 512)
    topk = min(DSA_TOPK_MAX, T // 4)
    ik = small[:, wd['glr']:wd['glr'] + idim].astype(BF16)
    ikt = ik.reshape(B, T // tk, tk, idim).transpose(0, 1, 3, 2)
    bias = _dsa_mask(p_big, small, ikt, B, T, tq, tk, boff['idx_q'], wd['glr'] + idim, topk)
    o_dsa = _dsa_attn(p_big, bias, B, T, tq, tk, boff['dsa_q'], DSA_HEADS * dh)

    kv = _matmul(mem2, w_mem_kv.astype(BF16), BF16, 512, 512, "mem_kv")
    o_mem = _mem_attn(p_big, kv, B, T, Mt, dm, boff['mem_q'], _tile(T, 1024))

    merged = _merge(o_gla, o_dsa, o_mem, w_br_gla.astype(BF16), w_br_dsa.astype(BF16), w_br_mem.astype(BF16),
                    p_big, b_merge.reshape(1, -1), D, boff['gate'], _tile(M, 512), 512)
    x1 = _proj_ln(merged, w_o.astype(BF16), x2, ln1_g.reshape(1, -1), ln1_b.reshape(1, -1), alpha, _tile(M, 512))
    return _mlp(x1, w_up.astype(BF16), b_up.reshape(1, -1), w_down.astype(BF16), b_down.reshape(1, -1),
                ln2_g.reshape(1, -1), ln2_b.reshape(1, -1), alpha, _tile(M, 512), 1024)


def kernel(x, mem, w_in, w_gla_gate_up, b_gla_gate, gla_norm_g, w_mem_kv, w_br_gla, w_br_dsa, w_br_mem,
           b_merge, w_o, ln1_g, ln1_b, w_up, b_up, w_down, b_down, ln2_g, ln2_b):
    B, T, D = x.shape
    depth = w_in.shape[0]
    alpha = (2 * depth) ** 0.25
    x2 = x.reshape(B * T, D)
    mem2 = mem.reshape(-1, D)
    for l in range(depth):
        x2 = _layer(x2, mem2, B, T, w_in[l], w_gla_gate_up[l], b_gla_gate[l], gla_norm_g[l], w_mem_kv[l],
                    w_br_gla[l], w_br_dsa[l], w_br_mem[l], b_merge[l], w_o[l], ln1_g[l], ln1_b[l],
                    w_up[l], b_up[l], w_down[l], b_down[l], ln2_g[l], ln2_b[l], alpha)
    return x2.reshape(B, T, D)
```

```python
import functools

import numpy as np
import jax
import jax.numpy as jnp
from jax import lax
from jax.experimental import pallas as pl
from jax.experimental.pallas import tpu as pltpu

F32 = jnp.float32
BF16 = jnp.bfloat16

GLA_HEADS = 4
GLA_GATE_RANK = 16
GLA_GATE_TEMP = 16.0
GLA_CHUNK = 64
DSA_HEADS = 8
IDX_HEADS = 16
DSA_TOPK_MAX = 256
MEM_HEADS = 4
N_BRANCH = 3
LN_EPS = 1e-5
RMS_EPS = 1e-6

V7X_VMEM_LIMIT_BYTES = 56 * 1024 * 1024
LANES = 128
MASK_NEG = -1e30
SCORE_SUB_ROWS, SCORE_SUB_COLS = 128, 256
ATTN_MASK_TILES = 2
CAND_PER_GROUP = 7
ALIBI_SPLIT = 3


def _cparams(sem):
    return pltpu.CompilerParams(dimension_semantics=sem, vmem_limit_bytes=V7X_VMEM_LIMIT_BYTES)


def _mm_cast_kernel(x_ref, w_ref, o_ref, xb_ref):
    @pl.when(pl.program_id(1) == 0)
    def _():
        xb_ref[...] = x_ref[...].astype(BF16)

    o_ref[...] = jnp.dot(xb_ref[...], w_ref[...], preferred_element_type=F32).astype(o_ref.dtype)


def _matmul(x, w, out_dtype, tm, tn, name):
    M, K = x.shape
    N = w.shape[1]
    tm, tn = min(tm, M), min(tn, N)
    assert M % tm == 0 and N % tn == 0
    return pl.pallas_call(
        _mm_cast_kernel,
        out_shape=jax.ShapeDtypeStruct((M, N), out_dtype),
        grid=(M // tm, N // tn),
        in_specs=[pl.BlockSpec((tm, K), lambda i, j: (i, 0)),
                  pl.BlockSpec((K, tn), lambda i, j: (0, j))],
        out_specs=pl.BlockSpec((tm, tn), lambda i, j: (i, j)),
        scratch_shapes=[pltpu.VMEM((tm, K), BF16)],
        compiler_params=_cparams(("parallel", "arbitrary")),
        name=name,
    )(x, w)


def _in_proj_kernel(x_ref, w_ref, ws_ref, o_ref, os_ref, xb_ref):
    @pl.when(pl.program_id(1) == 0)
    def _():
        xb_ref[...] = x_ref[...].astype(BF16)
        os_ref[...] = jnp.dot(xb_ref[...], ws_ref[...], preferred_element_type=F32)

    o_ref[...] = jnp.dot(xb_ref[...], w_ref[...], preferred_element_type=F32).astype(o_ref.dtype)


def _in_proj(x, w_big, w_small, tm, tn):
    M, K = x.shape
    N, Ns = w_big.shape[1], w_small.shape[1]
    tm, tn = min(tm, M), min(tn, N)
    assert M % tm == 0 and N % tn == 0
    return pl.pallas_call(
        _in_proj_kernel,
        out_shape=(jax.ShapeDtypeStruct((M, N), BF16), jax.ShapeDtypeStruct((M, Ns), F32)),
        grid=(M // tm, N // tn),
        in_specs=[pl.BlockSpec((tm, K), lambda i, j: (i, 0)),
                  pl.BlockSpec((K, tn), lambda i, j: (0, j)),
                  pl.BlockSpec((K, Ns), lambda i, j: (0, 0))],
        out_specs=(pl.BlockSpec((tm, tn), lambda i, j: (i, j)),
                   pl.BlockSpec((tm, Ns), lambda i, j: (i, 0))),
        scratch_shapes=[pltpu.VMEM((tm, K), BF16)],
        compiler_params=_cparams(("parallel", "arbitrary")),
        name="in_proj",
    )(x, w_big, w_small)


def _log_sigmoid(z):
    return jnp.minimum(z, 0.0) - jnp.log(1.0 + jnp.exp(-jnp.abs(z)))


def _gla_kernel(q_ref, k_ref, v_ref, r_ref, sm_ref, wg_ref, bg_ref, ng_ref, o_ref, st_ref, *, chunk, rank):
    @pl.when(pl.program_id(2) == 0)
    def _():
        st_ref[...] = jnp.zeros_like(st_ref)

    tb, dk = q_ref.shape
    C = chunk
    glr = sm_ref[:, 0:rank]
    z = jnp.dot(glr, wg_ref[...], preferred_element_type=F32) + bg_ref[...]
    la = _log_sigmoid(z) * (1.0 / GLA_GATE_TEMP)
    causal = (lax.broadcasted_iota(jnp.int32, (C, C), 1) <= lax.broadcasted_iota(jnp.int32, (C, C), 0))
    tri = jnp.where(causal, 1.0, 0.0).astype(F32)

    n_chunk = tb // C
    chunk_rows = [slice(c * C, (c + 1) * C) for c in range(n_chunk)]
    bcs = [jnp.dot(tri, la[rows], preferred_element_type=F32, precision=lax.Precision.HIGHEST)
           for rows in chunk_rows]
    scs, q_its, upds, decays = [], [], [], []
    for c, rows in enumerate(chunk_rows):
        bc = bcs[c]
        br = bc[C // 2:C // 2 + 1]
        bl = bc[C - 1:C]
        qc = q_ref[rows, :].astype(F32)
        kc = k_ref[rows, :].astype(F32)
        q_in = (qc * jnp.exp(bc - br)).astype(BF16)
        k_in = (kc * jnp.exp(br - bc)).astype(BF16)
        scs.append(lax.dot_general(q_in, k_in, (((1,), (1,)), ((), ())), preferred_element_type=F32))
        q_its.append((qc * jnp.exp(bc)).astype(BF16))
        k_st = (kc * jnp.exp(bl - bc)).astype(BF16)
        upds.append(lax.dot_general(v_ref[rows, :], k_st, (((0,), (0,)), ((), ())),
                                    preferred_element_type=F32))
        decays.append(jnp.exp(bl))
    o_intra = [jnp.dot(jnp.where(causal, scs[c], 0.0).astype(BF16), v_ref[rows, :], preferred_element_type=F32)
               for c, rows in enumerate(chunk_rows)]

    for c, rows in enumerate(chunk_rows):
        st = st_ref[...]
        o = o_intra[c] + lax.dot_general(q_its[c], st.astype(BF16), (((1,), (1,)), ((), ())),
                                         preferred_element_type=F32)
        st_ref[...] = st * decays[c] + upds[c]
        o = o * lax.rsqrt(jnp.mean(o * o, axis=-1, keepdims=True) + RMS_EPS)
        r = r_ref[rows, :].astype(F32)
        o = o * ng_ref[...] * (r / (1.0 + jnp.exp(-r)))
        o_ref[rows, :] = o.astype(o_ref.dtype)


def _gla(p_big, small, w_gate_up, b_gate, norm_g, B, T, dk, dv, tb):
    M = B * T
    H = GLA_HEADS
    nb = T // tb
    row = lambda b, h, n: b * nb + n
    kern = functools.partial(_gla_kernel, chunk=GLA_CHUNK, rank=GLA_GATE_RANK)
    koff = (H * dk) // dk
    voff = (2 * H * dk) // dv
    roff = (2 * H * dk + H * dv) // dv
    return pl.pallas_call(
        kern,
        out_shape=jax.ShapeDtypeStruct((M, H * dv), BF16),
        grid=(B, H, nb),
        in_specs=[
            pl.BlockSpec((tb, dk), lambda b, h, n: (row(b, h, n), h)),
            pl.BlockSpec((tb, dk), lambda b, h, n: (row(b, h, n), koff + h)),
            pl.BlockSpec((tb, dv), lambda b, h, n: (row(b, h, n), voff + h)),
            pl.BlockSpec((tb, dv), lambda b, h, n: (row(b, h, n), roff + h)),
            pl.BlockSpec((tb, LANES), lambda b, h, n: (row(b, h, n), 0)),
            pl.BlockSpec((GLA_GATE_RANK, dk), lambda b, h, n: (0, h)),
            pl.BlockSpec((1, dk), lambda b, h, n: (0, h)),
            pl.BlockSpec((1, dv), lambda b, h, n: (0, h)),
        ],
        out_specs=pl.BlockSpec((tb, dv), lambda b, h, n: (row(b, h, n), h)),
        scratch_shapes=[pltpu.VMEM((dv, dk), F32)],
        compiler_params=_cparams(("parallel", "parallel", "arbitrary")),
        name="gla",
    )(p_big, p_big, p_big, p_big, small, w_gate_up, b_gate, norm_g)


def _idx_kernel(iq_ref, sm_ref, ikt_ref, bias_ref, sc_ref, iqs_ref, wb_ref, cand_ref, thr_ref, *, topk, w_off, idim):
    i = pl.program_id(1)
    nT, tq, tk = sc_ref.shape
    G = iqs_ref.shape[0]
    n_lane = tk // LANES
    n_cand = cand_ref.shape[0]
    R = n_cand // n_lane
    ntile = ((i + 1) * tq + tk - 1) // tk
    kf = float(topk)
    RB = min(tq, 128)
    n_rb = tq // RB

    w = sm_ref[:, w_off:w_off + G] * ((G ** -0.5) * (idim ** -0.5))
    for g in range(G):
        iqs_ref[g] = iq_ref[:, g * idim:(g + 1) * idim]
        wb_ref[g] = jnp.broadcast_to(w[:, g:g + 1], (tq, LANES))
    row_pos = i * tq + lax.broadcasted_iota(jnp.int32, (tq, 1), 0)
    cand_ref[...] = jnp.full(cand_ref.shape, -jnp.inf, F32)

    sub_rows, sub_cols = min(tq, SCORE_SUB_ROWS), min(tk, SCORE_SUB_COLS)

    def score_tile(j, carry):
        col_pos = j * tk + lax.broadcasted_iota(jnp.int32, (1, tk), 1)
        for rh in range(tq // sub_rows):
            rows = slice(rh * sub_rows, (rh + 1) * sub_rows)
            for lh in range(tk // sub_cols):
                cols = slice(lh * sub_cols, (lh + 1) * sub_cols)
                ik = ikt_ref[j, :, cols]
                acc = jnp.zeros((sub_rows, sub_cols), F32)
                for g in range(G):
                    a = jnp.dot(iqs_ref[g, rows, :], ik, preferred_element_type=F32)
                    wg = jnp.concatenate([wb_ref[g, rows, :]] * (sub_cols // LANES), axis=1)
                    acc = acc + wg * jnp.maximum(a, 0.0)
                sc = jnp.where(col_pos[:, cols] <= row_pos[rows], acc, -jnp.inf)
                sc_ref[j, rows, cols] = sc
                for cc in range(sub_cols // LANES):
                    c = lh * (sub_cols // LANES) + cc
                    x = sc[:, cc * LANES:(cc + 1) * LANES]
                    for r in range(R):
                        cur = cand_ref[c * R + r, rows, :]
                        cand_ref[c * R + r, rows, :] = jnp.maximum(cur, x)
                        x = jnp.minimum(cur, x)
        return carry

    lax.fori_loop(0, ntile, score_tile, 0)

    def to_key(v):
        bits = pltpu.bitcast(v, jnp.int32)
        return bits ^ ((bits >> 31) & 0x7FFFFFFF)

    def from_key(k):
        return pltpu.bitcast(k ^ ((k >> 31) & 0x7FFFFFFF), F32)

    gmax, last_kept = cand_ref[0], cand_ref[R - 1]
    for c in range(1, n_lane):
        gmax = jnp.maximum(gmax, cand_ref[c * R])
        last_kept = jnp.maximum(last_kept, cand_ref[c * R + R - 1])
    hi0 = to_key(jnp.max(gmax, axis=1, keepdims=True))
    last_kept = jnp.max(last_kept, axis=1, keepdims=True)
    n_cover = -(-topk // LANES)
    lo_f = None
    for r in range(min(R, n_cover)):
        need = -(-n_cover // (r + 1))
        if need > n_lane:
            continue
        mins = [jnp.min(cand_ref[c * R + r], axis=1, keepdims=True) for c in range(n_lane)]
        for a in range(need):
            for b2 in range(n_lane - 1, a, -1):
                hi_v, lo_v = jnp.maximum(mins[b2 - 1], mins[b2]), jnp.minimum(mins[b2 - 1], mins[b2])
                mins[b2 - 1], mins[b2] = hi_v, lo_v
        lo_f = mins[need - 1] if lo_f is None else jnp.maximum(lo_f, mins[need - 1])
    assert lo_f is not None, "topk must not exceed the number of key groups"
    lo0 = to_key(lo_f)
    few = row_pos < topk
    all_finite = to_key(jnp.full((tq, 1), -jnp.inf, F32)) + 1
    lo0 = jnp.where(few, all_finite, lo0)
    hi0 = jnp.where(few, all_finite, hi0)

    def count_ge(thr, strict=False):
        outs = []
        for r in range(n_rb):
            rows = slice(r * RB, (r + 1) * RB)
            tb = jnp.broadcast_to(thr[rows], (RB, LANES))

            def body(j, cnt, rows=rows, tb=tb):
                for c in range(n_lane):
                    st = sc_ref[j, rows, c * LANES:(c + 1) * LANES]
                    cnt = cnt + jnp.where((st > tb) if strict else (st >= tb), 1.0, 0.0)
                return cnt

            cnt = lax.fori_loop(0, ntile, body, jnp.zeros((RB, LANES), F32))
            outs.append(jnp.sum(cnt, axis=1, keepdims=True))
        return jnp.concatenate(outs, axis=0)

    def sorted_count_ge(vals, tb):
        if not vals:
            return None
        mid = len(vals) // 2
        hit = vals[mid] >= tb
        base = jnp.where(hit, float(mid + 1), 0.0)
        rest = sorted_count_ge([jnp.where(hit, lo_v, hi_v) for hi_v, lo_v in zip(vals[:mid], vals[mid + 1:])], tb)
        return base if rest is None else base + rest

    def cand_count_ge(thr):
        assert (R + 1) & R == 0, "candidate lists must have 2**d - 1 entries"
        outs = []
        for r in range(n_rb):
            rows = slice(r * RB, (r + 1) * RB)
            tb = jnp.broadcast_to(thr[rows], (RB, LANES))
            cnt = jnp.zeros((RB, LANES), F32)
            for c in range(n_lane):
                cnt = cnt + sorted_count_ge([cand_ref[c * R + q, rows, :] for q in range(R)], tb)
            outs.append(jnp.sum(cnt, axis=1, keepdims=True))
        return jnp.concatenate(outs, axis=0)

    def cand_min_ge(thr):
        outs = []
        for r in range(n_rb):
            rows = slice(r * RB, (r + 1) * RB)
            tb = jnp.broadcast_to(thr[rows], (RB, LANES))
            cur = jnp.full((RB, LANES), jnp.inf, F32)
            for q in range(n_cand):
                ck = cand_ref[q, rows, :]
                cur = jnp.minimum(cur, jnp.where(ck >= tb, ck, jnp.inf))
            outs.append(jnp.min(cur, axis=1, keepdims=True))
        return jnp.concatenate(outs, axis=0)

    def n_open(lo, hi):
        return jnp.sum(jnp.where(lo < hi, 1.0, 0.0)).astype(jnp.int32)

    def search(count):
        def bisect(state):
            lo, hi, exact, _ = state
            x = lo ^ hi
            mid = (lo & hi) + (x >> 1) + (x & 1)
            cnt = count(from_key(mid))
            ge, eq = cnt >= kf, cnt == kf
            lo = jnp.where(ge, mid, lo)
            hi = jnp.where(eq, mid, jnp.where(ge, hi, mid - 1))
            return lo, hi, jnp.where(eq, 1, exact), n_open(lo, hi)

        lo, _, exact, _ = lax.while_loop(lambda s: s[3] > 0, bisect,
                                         (lo0, hi0, jnp.where(few, 1, 0), n_open(lo0, hi0)))
        return from_key(lo), exact

    thr, exact = search(cand_count_ge)
    thr = jnp.where(few, thr, cand_min_ge(thr))
    thr_ref[0] = thr
    thr_ref[1] = exact.astype(F32)
    n_unsure = jnp.sum(jnp.where((last_kept >= thr) & jnp.logical_not(few), 1.0, 0.0)).astype(jnp.int32)

    @pl.when(n_unsure > 0)
    def _():
        thr_full, exact_full = search(count_ge)
        thr_ref[0] = thr_full
        thr_ref[1] = exact_full.astype(F32)

    thr = thr_ref[0]
    n_tied = jnp.sum(jnp.where(thr_ref[1] > 0.0, 0.0, 1.0)).astype(jnp.int32)

    @pl.when(n_tied == 0)
    def _():
        thr_b = jnp.broadcast_to(thr, (tq, LANES))

        def write_tile(j, carry):
            st = sc_ref[j]
            parts = [jnp.where(st[:, c * LANES:(c + 1) * LANES] >= thr_b, 0.0, MASK_NEG) for c in range(n_lane)]
            bias_ref[j] = jnp.concatenate(parts, axis=1).astype(bias_ref.dtype)
            return carry

        lax.fori_loop(0, ntile, write_tile, 0)

    @pl.when(n_tied > 0)
    def _():
        need = kf - count_ge(thr, strict=True)
        before = (lax.broadcasted_iota(jnp.int32, (tk, tk), 0) < lax.broadcasted_iota(jnp.int32, (tk, tk), 1))
        before = jnp.where(before, 1.0, 0.0).astype(BF16)

        def write_tile(j, run):
            st = sc_ref[j]
            tied = jnp.where(st == thr, 1.0, 0.0)
            rank = jnp.dot(tied.astype(BF16), before, preferred_element_type=F32) + run
            sel = (st > thr) | ((st == thr) & (rank < need))
            bias_ref[j] = jnp.where(sel, 0.0, MASK_NEG).astype(bias_ref.dtype)
            return run + jnp.sum(tied, axis=1, keepdims=True)

        lax.fori_loop(0, ntile, write_tile, jnp.zeros((tq, 1), F32))

    def fill_tile(j, carry):
        bias_ref[j] = jnp.full((tq, tk), MASK_NEG, bias_ref.dtype)
        return carry

    lax.fori_loop(ntile, nT, fill_tile, 0)


def _dsa_mask(p_big, small, ikt, B, T, tq, tk, iq_off, w_off, topk):
    G = IDX_HEADS
    idim = ikt.shape[2]
    nQ, nT = T // tq, T // tk
    kern = functools.partial(_idx_kernel, topk=topk, w_off=w_off, idim=idim)
    return pl.pallas_call(
        kern,
        out_shape=jax.ShapeDtypeStruct((B, nQ, nT, tq, tk), BF16),
        grid=(B, nQ),
        in_specs=[
            pl.BlockSpec((tq, G * idim), lambda b, i: (b * nQ + i, iq_off // (G * idim))),
            pl.BlockSpec((tq, LANES), lambda b, i: (b * nQ + i, 0)),
            pl.BlockSpec((None, nT, idim, tk), lambda b, i: (b, 0, 0, 0)),
        ],
        out_specs=pl.BlockSpec((None, None, nT, tq, tk), lambda b, i: (b, i, 0, 0, 0)),
        scratch_shapes=[pltpu.VMEM((nT, tq, tk), F32),
                        pltpu.VMEM((G, tq, idim), BF16),
                        pltpu.VMEM((G, tq, LANES), F32),
                        pltpu.VMEM((tk // LANES * CAND_PER_GROUP, tq, LANES), F32),
                        pltpu.VMEM((2, tq, 1), F32)],
        compiler_params=_cparams(("parallel", "arbitrary")),
        name="dsa_index_topk",
    )(p_big, small, ikt)


def _dsa_attn_kernel(qi_ref, kj_ref, q_ref, k_ref, v_ref, bias_ref, sl_ref, o_ref, m_ref, l_ref, acc_ref, *, heads):
    p = pl.program_id(1)
    qi, kj = qi_ref[p], kj_ref[p]
    tk = bias_ref.shape[2]
    tq = bias_ref.shape[0] * bias_ref.shape[1]
    d = q_ref.shape[1] // heads
    n_lane = tk // LANES

    @pl.when(kj == 0)
    def _():
        m_ref[...] = jnp.full_like(m_ref, MASK_NEG)
        l_ref[...] = jnp.zeros_like(l_ref)
        acc_ref[...] = jnp.zeros_like(acc_ref)

    rel = (kj * tk - qi * tq) + lax.broadcasted_iota(jnp.int32, (tk, LANES), 0)
    lane = lax.broadcasted_iota(jnp.int32, (tk, LANES), 1)
    pos = jnp.where(lane < 2 * ALIBI_SPLIT, jnp.where(lane % 2 == 0, rel >> 8, rel & 255), 0)
    pos = pos.astype(F32).astype(BF16)
    bias = bias_ref[...].astype(F32).reshape(tq, tk)
    ones_col = jnp.where(lane == 0, 1.0, 0.0).astype(BF16)

    def masked_logits(h):
        cols = slice(h * d, (h + 1) * d)
        qa = jnp.concatenate([q_ref[:, cols], sl_ref[h]], axis=1)
        ka = jnp.concatenate([k_ref[:, cols], pos], axis=1)
        s = lax.dot_general(qa, ka, (((1,), (1,)), ((), ())), preferred_element_type=F32)
        return [s[:, c * LANES:(c + 1) * LANES] + bias[:, c * LANES:(c + 1) * LANES] for c in range(n_lane)]

    sb_next = masked_logits(0)
    for h in range(heads):
        cols = slice(h * d, (h + 1) * d)
        sb = sb_next
        if h + 1 < heads:
            sb_next = masked_logits(h + 1)
        mx = sb[0]
        for c in range(1, n_lane):
            mx = jnp.maximum(mx, sb[c])
        m_old = m_ref[h]
        m_new = jnp.maximum(m_old, jnp.max(mx, axis=1, keepdims=True))
        corr = jnp.exp2(m_old - m_new)
        ps = [jnp.exp2((sb[c] - m_new).astype(BF16)) for c in range(n_lane)]
        va = jnp.concatenate([v_ref[:, cols], ones_col], axis=1)
        pv = jnp.dot(jnp.concatenate(ps, axis=1), va, preferred_element_type=F32)
        l_ref[h] = corr * l_ref[h] + pv[:, d:]
        acc_ref[:, cols] = corr * acc_ref[:, cols] + pv[:, :d]
        m_ref[h] = m_new

    @pl.when((kj + 1) * tk >= (qi + 1) * tq)
    def _():
        for h in range(heads):
            cols = slice(h * d, (h + 1) * d)
            l = jnp.sum(l_ref[h], axis=1, keepdims=True)
            o_ref[:, cols] = (acc_ref[:, cols] / l).astype(o_ref.dtype)


def _alibi_columns(heads, tq):
    out = np.zeros((heads, LANES), np.float32)
    for h in range(heads):
        rest = np.float64(2.0 ** (-8.0 * (h + 1) / heads) * np.log2(np.e))
        for i in range(ALIBI_SPLIT):
            piece = np.float64(np.float32(rest).astype(BF16))
            out[h, 2 * i], out[h, 2 * i + 1] = 256.0 * piece, piece
            rest = rest - piece
    return jnp.asarray(np.broadcast_to(out.astype(BF16)[:, None, :], (heads, tq, LANES)))


def _dsa_attn(p_big, bias, B, T, tq_mask, tk, q_off, width):
    tq = tq_mask * ATTN_MASK_TILES
    assert T % tq == 0
    nQ, nT = T // tq, T // tk
    pairs = [(i, j) for i in range(nQ) for j in range(((i + 1) * tq + tk - 1) // tk)]
    qi_tbl = jnp.asarray(np.array([p[0] for p in pairs], np.int32))
    kj_tbl = jnp.asarray(np.array([p[1] for p in pairs], np.int32))
    qb = q_off // width
    heads = DSA_HEADS
    kern = functools.partial(_dsa_attn_kernel, heads=heads)
    grid_spec = pltpu.PrefetchScalarGridSpec(
        num_scalar_prefetch=2,
        grid=(B, len(pairs)),
        in_specs=[
            pl.BlockSpec((tq, width), lambda b, p, qi, kj: (b * nQ + qi[p], qb)),
            pl.BlockSpec((tk, width), lambda b, p, qi, kj: (b * nT + kj[p], qb + 1)),
            pl.BlockSpec((tk, width), lambda b, p, qi, kj: (b * nT + kj[p], qb + 2)),
            pl.BlockSpec((None, ATTN_MASK_TILES, None, tq_mask, tk), lambda b, p, qi, kj: (b, qi[p], kj[p], 0, 0)),
            pl.BlockSpec((heads, tq, LANES), lambda b, p, qi, kj: (0, 0, 0)),
        ],
        out_specs=pl.BlockSpec((tq, width), lambda b, p, qi, kj: (b * nQ + qi[p], 0)),
        scratch_shapes=[pltpu.VMEM((heads, tq, LANES), F32),
                        pltpu.VMEM((heads, tq, LANES), F32),
                        pltpu.VMEM((tq, width), F32)],
    )
    return pl.pallas_call(
        kern,
        out_shape=jax.ShapeDtypeStruct((B * T, width), BF16),
        grid_spec=grid_spec,
        compiler_params=_cparams(("arbitrary", "arbitrary")),
        name="dsa_attention",
    )(qi_tbl, kj_tbl, p_big, p_big, p_big, bias, _alibi_columns(heads, tq))


def _mem_attn_kernel(q_ref, k_ref, v_ref, o_ref):
    s = lax.dot_general(q_ref[...], k_ref[...], (((1,), (1,)), ((), ())), preferred_element_type=F32)
    m = jnp.max(s, axis=1, keepdims=True)
    p = jnp.exp(s - m)
    l = jnp.sum(p, axis=1, keepdims=True)
    o = jnp.dot(p.astype(BF16), v_ref[...], preferred_element_type=F32)
    o_ref[...] = (o / l).astype(o_ref.dtype)


def _mem_attn(p_big, kv, B, T, Mt, dm, q_off, tq):
    Hm = MEM_HEADS
    nq = T // tq
    qb = q_off // dm
    return pl.pallas_call(
        _mem_attn_kernel,
        out_shape=jax.ShapeDtypeStruct((B * T, Hm * dm), BF16),
        grid=(B, nq, Hm),
        in_specs=[
            pl.BlockSpec((tq, dm), lambda b, i, h: (b * nq + i, qb + h)),
            pl.BlockSpec((Mt, dm), lambda b, i, h: (b, h)),
            pl.BlockSpec((Mt, dm), lambda b, i, h: (b, Hm + h)),
        ],
        out_specs=pl.BlockSpec((tq, dm), lambda b, i, h: (b * nq + i, h)),
        compiler_params=_cparams(("parallel", "parallel", "arbitrary")),
        name="memory_attention",
    )(p_big, kv, kv)


def _merge_kernel(a_ref, d_ref, m_ref, wa_ref, wd_ref, wm_ref, g0_ref, g1_ref, g2_ref, b_ref, o_ref, *, tn):
    D = o_ref.shape[1]

    def products(cols):
        return (jnp.dot(a_ref[...], wa_ref[:, cols], preferred_element_type=F32),
                jnp.dot(d_ref[...], wd_ref[:, cols], preferred_element_type=F32),
                jnp.dot(m_ref[...], wm_ref[:, cols], preferred_element_type=F32))

    def gated_sum(cols, ys):
        acc = None
        for r, (g_ref, y) in enumerate(zip((g0_ref, g1_ref, g2_ref), ys)):
            bias = b_ref[:, r * D + cols.start:r * D + cols.stop]
            term = jax.nn.sigmoid(g_ref[:, cols].astype(F32) + bias) * y
            acc = term if acc is None else acc + term
        o_ref[:, cols] = acc.astype(o_ref.dtype)

    chunks = [slice(j * tn, (j + 1) * tn) for j in range(D // tn)]
    pending = None
    for cols in chunks:
        ys = products(cols)
        if pending is not None:
            gated_sum(*pending)
        pending = (cols, ys)
    gated_sum(*pending)


def _merge(o_gla, o_dsa, o_mem, wa, wd, wm, p_big, b_merge, D, gate_off, tm, tn):
    M = o_gla.shape[0]
    gb = gate_off // D
    row = lambda w: pl.BlockSpec((tm, w), lambda i: (i, 0))
    resident = lambda k: pl.BlockSpec((k, D), lambda i: (0, 0), pipeline_mode=pl.Buffered(1))
    gspec = lambda r: pl.BlockSpec((tm, D), lambda i: (i, gb + r))
    return pl.pallas_call(
        functools.partial(_merge_kernel, tn=tn),
        out_shape=jax.ShapeDtypeStruct((M, D), BF16),
        grid=(M // tm,),
        in_specs=[row(o_gla.shape[1]), row(o_dsa.shape[1]), row(o_mem.shape[1]),
                  resident(wa.shape[0]), resident(wd.shape[0]), resident(wm.shape[0]),
                  gspec(0), gspec(1), gspec(2),
                  pl.BlockSpec((1, N_BRANCH * D), lambda i: (0, 0))],
        out_specs=pl.BlockSpec((tm, D), lambda i: (i, 0)),
        compiler_params=_cparams(("parallel",)),
        name="gated_merge",
    )(o_gla, o_dsa, o_mem, wa, wd, wm, p_big, p_big, p_big, b_merge)


def _layer_norm(y, g, b):
    mu = jnp.mean(y, axis=-1, keepdims=True)
    yc = y - mu
    var = jnp.mean(yc * yc, axis=-1, keepdims=True)
    return yc * lax.rsqrt(var + LN_EPS) * g + b


def _proj_ln_kernel(mg_ref, wo_ref, x_ref, g_ref, b_ref, o_ref, *, alpha):
    tm = mg_ref.shape[0]
    halves = [slice(0, tm // 2), slice(tm // 2, tm)]
    ys = [jnp.dot(mg_ref[rows, :], wo_ref[...], preferred_element_type=F32) for rows in halves]
    for rows, y in zip(halves, ys):
        o_ref[rows, :] = _layer_norm(alpha * x_ref[rows, :] + y, g_ref[...], b_ref[...])


def _proj_ln(merged, w_o, x2, g, b, alpha, tm):
    M, D = x2.shape
    vec = pl.BlockSpec((1, D), lambda i: (0, 0))
    return pl.pallas_call(
        functools.partial(_proj_ln_kernel, alpha=alpha),
        out_shape=jax.ShapeDtypeStruct((M, D), F32),
        grid=(M // tm,),
        in_specs=[pl.BlockSpec((tm, D), lambda i: (i, 0)),
                  pl.BlockSpec((D, D), lambda i: (0, 0)),
                  pl.BlockSpec((tm, D), lambda i: (i, 0)), vec, vec],
        out_specs=pl.BlockSpec((tm, D), lambda i: (i, 0)),
        compiler_params=_cparams(("parallel",)),
        name="out_proj_ln",
    )(merged, w_o, x2, g, b)


def _mlp_kernel(x_ref, wu_ref, bu_ref, wd_ref, bd_ref, g_ref, b_ref, o_ref, xb_ref, acc_ref, *, alpha):
    f = pl.program_id(1)

    @pl.when(f == 0)
    def _():
        xb_ref[...] = x_ref[...].astype(BF16)
        acc_ref[...] = jnp.zeros_like(acc_ref)

    tf = wu_ref.shape[1]
    halves = [slice(0, tf // 2), slice(tf // 2, tf)]
    ups = [jnp.dot(xb_ref[...], wu_ref[:, cols], preferred_element_type=F32) for cols in halves]
    hs = [jnp.square(jnp.maximum(u + bu_ref[:, cols], 0.0)).astype(BF16) for u, cols in zip(ups, halves)]
    downs = [jnp.dot(h, wd_ref[cols, :], preferred_element_type=F32) for h, cols in zip(hs, halves)]
    acc_ref[...] += downs[0] + downs[1]

    @pl.when(f == pl.num_programs(1) - 1)
    def _():
        y = alpha * x_ref[...] + acc_ref[...] + bd_ref[...]
        o_ref[...] = _layer_norm(y, g_ref[...], b_ref[...])


def _mlp(x1, w_up, b_up, w_down, b_down, g, b, alpha, tm, tf):
    M, D = x1.shape
    F = w_up.shape[1]
    vec = pl.BlockSpec((1, D), lambda i, f: (0, 0))
    return pl.pallas_call(
        functools.partial(_mlp_kernel, alpha=alpha),
        out_shape=jax.ShapeDtypeStruct((M, D), F32),
        grid=(M // tm, F // tf),
        in_specs=[pl.BlockSpec((tm, D), lambda i, f: (i, 0)),
                  pl.BlockSpec((D, tf), lambda i, f: (0, f)),
                  pl.BlockSpec((1, tf), lambda i, f: (0, f)),
                  pl.BlockSpec((tf, D), lambda i, f: (f, 0)),
                  vec, vec, vec],
        out_specs=pl.BlockSpec((tm, D), lambda i, f: (i, 0)),
        scratch_shapes=[pltpu.VMEM((tm, D), BF16), pltpu.VMEM((tm, D), F32)],
        compiler_params=_cparams(("parallel", "arbitrary")),
        name="mlp_ln",
    )(x1, w_up, b_up, w_down, b_down, g, b)


def _tile(n, pref):
    t = min(n, pref)
    assert n % t == 0
    return t


def _layer(x2, mem2, B, T, w_in, w_gate_up, b_gate, norm_g, w_mem_kv, w_br_gla, w_br_dsa, w_br_mem,
           b_merge, w_o, ln1_g, ln1_b, w_up, b_up, w_down, b_down, ln2_g, ln2_b, alpha):
    M, D = x2.shape
    Mt = mem2.shape[0] // B
    dk, dv = D // 8, D // 4
    dh = D // 16
    idim = D // 32
    dm = D // 8
    H, G = GLA_HEADS, IDX_HEADS
    widths = [H * dk, H * dk, H * dv, H * dv, GLA_GATE_RANK, DSA_HEADS * dh, DSA_HEADS * dh, DSA_HEADS * dh,
              G * idim, idim, G, MEM_HEADS * dm, N_BRANCH * D]
    names = ['gla_q', 'gla_k', 'gla_v', 'gla_r', 'glr', 'dsa_q', 'dsa_k', 'dsa_v', 'idx_q', 'idx_k', 'idx_w',
             'mem_q', 'gate']
    offs = dict(zip(names, np.cumsum([0] + widths[:-1]).tolist()))
    wd = dict(zip(names, widths))
    col = lambda n, scale=None: (w_in[:, offs[n]:offs[n] + wd[n]] if scale is None
                                 else w_in[:, offs[n]:offs[n] + wd[n]] * scale)
    big_parts = [('gla_q', dk ** -0.5), ('gla_k', None), ('gla_v', None), ('gla_r', None), ('gate', None),
                 ('dsa_q', dh ** -0.5 * float(np.log2(np.e))), ('dsa_k', None), ('dsa_v', None), ('idx_q', None),
                 ('mem_q', dm ** -0.5)]
    w_big = jnp.concatenate([col(n, s) for n, s in big_parts], axis=1).astype(BF16)
    boff = dict(zip([n for n, _ in big_parts], np.cumsum([0] + [wd[n] for n, _ in big_parts][:-1]).tolist()))
    n_small = wd['glr'] + wd['idx_k'] + wd['idx_w']
    w_small = jnp.concatenate([col('glr'), col('idx_k'), col('idx_w'),
                               jnp.zeros((D, LANES - n_small), w_in.dtype)], axis=1).astype(BF16)

    p_big, small = _in_proj(x2, w_big, w_small, 1024, 1024)

    o_gla = _gla(p_big, small, w_gate_up, b_gate.reshape(1, -1), norm_g.reshape(1, -1), B, T, dk, dv,
                 _tile(T, 256))

    tq, tk = _tile(T, 256), _tile(T, 512)
    topk = min(DSA_TOPK_MAX, T // 4)
    ik = small[:, wd['glr']:wd['glr'] + idim].astype(BF16)
    ikt = ik.reshape(B, T // tk, tk, idim).transpose(0, 1, 3, 2)
    bias = _dsa_mask(p_big, small, ikt, B, T, tq, tk, boff['idx_q'], wd['glr'] + idim, topk)
    o_dsa = _dsa_attn(p_big, bias, B, T, tq, tk, boff['dsa_q'], DSA_HEADS * dh)

    kv = _matmul(mem2, w_mem_kv.astype(BF16), BF16, 512, 512, "mem_kv")
    o_mem = _mem_attn(p_big, kv, B, T, Mt, dm, boff['mem_q'], _tile(T, 1024))

    merged = _merge(o_gla, o_dsa, o_mem, w_br_gla.astype(BF16), w_br_dsa.astype(BF16), w_br_mem.astype(BF16),
                    p_big, b_merge.reshape(1, -1), D, boff['gate'], _tile(M, 512), 512)
    x1 = _proj_ln(merged, w_o.astype(BF16), x2, ln1_g.reshape(1, -1), ln1_b.reshape(1, -1), alpha, _tile(M, 512))
    return _mlp(x1, w_up.astype(BF16), b_up.reshape(1, -1), w_down.astype(BF16), b_down.reshape(1, -1),
                ln2_g.reshape(1, -1), ln2_b.reshape(1, -1), alpha, _tile(M, 512), 1024)


def kernel(x, mem, w_in, w_gla_gate_up, b_gla_gate, gla_norm_g, w_mem_kv, w_br_gla, w_br_dsa, w_br_mem,
           b_merge, w_o, ln1_g, ln1_b, w_up, b_up, w_down, b_down, ln2_g, ln2_b):
    B, T, D = x.shape
    depth = w_in.shape[0]
    alpha = (2 * depth) ** 0.25
    x2 = x.reshape(B * T, D)
    mem2 = mem.reshape(-1, D)
    for l in range(depth):
        x2 = _layer(x2, mem2, B, T, w_in[l], w_gla_gate_up[l], b_gla_gate[l], gla_norm_g[l], w_mem_kv[l],
                    w_br_gla[l], w_br_dsa[l], w_br_mem[l], b_merge[l], w_o[l], ln1_g[l], ln1_b[l],
                    w_up[l], b_up[l], w_down[l], b_down[l], ln2_g[l], ln2_b[l], alpha)
    return x2.reshape(B, T, D)
```

```python
import functools

import numpy as np
import jax
import jax.numpy as jnp
from jax import lax
from jax.experimental import pallas as pl
from jax.experimental.pallas import tpu as pltpu

F32 = jnp.float32
BF16 = jnp.bfloat16

GLA_HEADS = 4
GLA_GATE_RANK = 16
GLA_GATE_TEMP = 16.0
GLA_CHUNK = 64
DSA_HEADS = 8
IDX_HEADS = 16
DSA_TOPK_MAX = 256
MEM_HEADS = 4
N_BRANCH = 3
LN_EPS = 1e-5
RMS_EPS = 1e-6

V7X_VMEM_LIMIT_BYTES = 56 * 1024 * 1024
LANES = 128
MASK_NEG = -1e30
SCORE_SUB_ROWS, SCORE_SUB_COLS = 128, 256
ATTN_MASK_TILES = 2
CAND_PER_GROUP = 7
ALIBI_SPLIT = 3


def _cparams(sem):
    return pltpu.CompilerParams(dimension_semantics=sem, vmem_limit_bytes=V7X_VMEM_LIMIT_BYTES)


def _mm_cast_kernel(x_ref, w_ref, o_ref, xb_ref):
    @pl.when(pl.program_id(1) == 0)
    def _():
        xb_ref[...] = x_ref[...].astype(BF16)

    o_ref[...] = jnp.dot(xb_ref[...], w_ref[...], preferred_element_type=F32).astype(o_ref.dtype)


def _matmul(x, w, out_dtype, tm, tn, name):
    M, K = x.shape
    N = w.shape[1]
    tm, tn = min(tm, M), min(tn, N)
    assert M % tm == 0 and N % tn == 0
    return pl.pallas_call(
        _mm_cast_kernel,
        out_shape=jax.ShapeDtypeStruct((M, N), out_dtype),
        grid=(M // tm, N // tn),
        in_specs=[pl.BlockSpec((tm, K), lambda i, j: (i, 0)),
                  pl.BlockSpec((K, tn), lambda i, j: (0, j))],
        out_specs=pl.BlockSpec((tm, tn), lambda i, j: (i, j)),
        scratch_shapes=[pltpu.VMEM((tm, K), BF16)],
        compiler_params=_cparams(("parallel", "arbitrary")),
        name=name,
    )(x, w)


def _in_proj_kernel(x_ref, w_ref, ws_ref, o_ref, os_ref, xb_ref):
    @pl.when(pl.program_id(1) == 0)
    def _():
        xb_ref[...] = x_ref[...].astype(BF16)
        os_ref[...] = jnp.dot(xb_ref[...], ws_ref[...], preferred_element_type=F32)

    o_ref[...] = jnp.dot(xb_ref[...], w_ref[...], preferred_element_type=F32).astype(o_ref.dtype)


def _in_proj(x, w_big, w_small, tm, tn):
    M, K = x.shape
    N, Ns = w_big.shape[1], w_small.shape[1]
    tm, tn = min(tm, M), min(tn, N)
    assert M % tm == 0 and N % tn == 0
    return pl.pallas_call(
        _in_proj_kernel,
        out_shape=(jax.ShapeDtypeStruct((M, N), BF16), jax.ShapeDtypeStruct((M, Ns), F32)),
        grid=(M // tm, N // tn),
        in_specs=[pl.BlockSpec((tm, K), lambda i, j: (i, 0)),
                  pl.BlockSpec((K, tn), lambda i, j: (0, j)),
                  pl.BlockSpec((K, Ns), lambda i, j: (0, 0))],
        out_specs=(pl.BlockSpec((tm, tn), lambda i, j: (i, j)),
                   pl.BlockSpec((tm, Ns), lambda i, j: (i, 0))),
        scratch_shapes=[pltpu.VMEM((tm, K), BF16)],
        compiler_params=_cparams(("parallel", "arbitrary")),
        name="in_proj",
    )(x, w_big, w_small)


def _log_sigmoid(z):
    return jnp.minimum(z, 0.0) - jnp.log(1.0 + jnp.exp(-jnp.abs(z)))


def _gla_kernel(q_ref, k_ref, v_ref, r_ref, sm_ref, wg_ref, bg_ref, ng_ref, o_ref, st_ref, *, chunk, rank):
    @pl.when(pl.program_id(2) == 0)
    def _():
        st_ref[...] = jnp.zeros_like(st_ref)

    tb, dk = q_ref.shape
    C = chunk
    glr = sm_ref[:, 0:rank]
    z = jnp.dot(glr, wg_ref[...], preferred_element_type=F32) + bg_ref[...]
    la = _log_sigmoid(z) * (1.0 / GLA_GATE_TEMP)
    causal = (lax.broadcasted_iota(jnp.int32, (C, C), 1) <= lax.broadcasted_iota(jnp.int32, (C, C), 0))
    tri = jnp.where(causal, 1.0, 0.0).astype(F32)

    n_chunk = tb // C
    chunk_rows = [slice(c * C, (c + 1) * C) for c in range(n_chunk)]
    bcs = [jnp.dot(tri, la[rows], preferred_element_type=F32, precision=lax.Precision.HIGHEST)
           for rows in chunk_rows]
    scs, q_its, upds, decays = [], [], [], []
    for c, rows in enumerate(chunk_rows):
        bc = bcs[c]
        br = bc[C // 2:C // 2 + 1]
        bl = bc[C - 1:C]
        qc = q_ref[rows, :].astype(F32)
        kc = k_ref[rows, :].astype(F32)
        q_in = (qc * jnp.exp(bc - br)).astype(BF16)
        k_in = (kc * jnp.exp(br - bc)).astype(BF16)
        scs.append(lax.dot_general(q_in, k_in, (((1,), (1,)), ((), ())), preferred_element_type=F32))
        q_its.append((qc * jnp.exp(bc)).astype(BF16))
        k_st = (kc * jnp.exp(bl - bc)).astype(BF16)
        upds.append(lax.dot_general(v_ref[rows, :], k_st, (((0,), (0,)), ((), ())),
                                    preferred_element_type=F32))
        decays.append(jnp.exp(bl))
    o_intra = [jnp.dot(jnp.where(causal, scs[c], 0.0).astype(BF16), v_ref[rows, :], preferred_element_type=F32)
               for c, rows in enumerate(chunk_rows)]

    for c, rows in enumerate(chunk_rows):
        st = st_ref[...]
        o = o_intra[c] + lax.dot_general(q_its[c], st.astype(BF16), (((1,), (1,)), ((), ())),
                                         preferred_element_type=F32)
        st_ref[...] = st * decays[c] + upds[c]
        o = o * lax.rsqrt(jnp.mean(o * o, axis=-1, keepdims=True) + RMS_EPS)
        r = r_ref[rows, :].astype(F32)
        o = o * ng_ref[...] * (r / (1.0 + jnp.exp(-r)))
        o_ref[rows, :] = o.astype(o_ref.dtype)


def _gla(p_big, small, w_gate_up, b_gate, norm_g, B, T, dk, dv, tb):
    M = B * T
    H = GLA_HEADS
    nb = T // tb
    row = lambda b, h, n: b * nb + n
    kern = functools.partial(_gla_kernel, chunk=GLA_CHUNK, rank=GLA_GATE_RANK)
    koff = (H * dk) // dk
    voff = (2 * H * dk) // dv
    roff = (2 * H * dk + H * dv) // dv
    return pl.pallas_call(
        kern,
        out_shape=jax.ShapeDtypeStruct((M, H * dv), BF16),
        grid=(B, H, nb),
        in_specs=[
            pl.BlockSpec((tb, dk), lambda b, h, n: (row(b, h, n), h)),
            pl.BlockSpec((tb, dk), lambda b, h, n: (row(b, h, n), koff + h)),
            pl.BlockSpec((tb, dv), lambda b, h, n: (row(b, h, n), voff + h)),
            pl.BlockSpec((tb, dv), lambda b, h, n: (row(b, h, n), roff + h)),
            pl.BlockSpec((tb, LANES), lambda b, h, n: (row(b, h, n), 0)),
            pl.BlockSpec((GLA_GATE_RANK, dk), lambda b, h, n: (0, h)),
            pl.BlockSpec((1, dk), lambda b, h, n: (0, h)),
            pl.BlockSpec((1, dv), lambda b, h, n: (0, h)),
        ],
        out_specs=pl.BlockSpec((tb, dv), lambda b, h, n: (row(b, h, n), h)),
        scratch_shapes=[pltpu.VMEM((dv, dk), F32)],
        compiler_params=_cparams(("parallel", "parallel", "arbitrary")),
        name="gla",
    )(p_big, p_big, p_big, p_big, small, w_gate_up, b_gate, norm_g)


def _idx_kernel(iq_ref, sm_ref, ikt_ref, bias_ref, sc_ref, iqs_ref, wb_ref, cand_ref, thr_ref, *, topk, w_off, idim):
    i = pl.program_id(1)
    nT, tq, tk = sc_ref.shape
    G = iqs_ref.shape[0]
    n_lane = tk // LANES
    n_cand = cand_ref.shape[0]
    R = n_cand // n_lane
    ntile = ((i + 1) * tq + tk - 1) // tk
    kf = float(topk)
    RB = min(tq, 128)
    n_rb = tq // RB

    w = sm_ref[:, w_off:w_off + G] * ((G ** -0.5) * (idim ** -0.5))
    for g in range(G):
        iqs_ref[g] = iq_ref[:, g * idim:(g + 1) * idim]
        wb_ref[g] = jnp.broadcast_to(w[:, g:g + 1], (tq, LANES))
    row_pos = i * tq + lax.broadcasted_iota(jnp.int32, (tq, 1), 0)
    cand_ref[...] = jnp.full(cand_ref.shape, -jnp.inf, F32)

    sub_rows, sub_cols = min(tq, SCORE_SUB_ROWS), min(tk, SCORE_SUB_COLS)

    def score_tile(j, carry):
        col_pos = j * tk + lax.broadcasted_iota(jnp.int32, (1, tk), 1)
        for rh in range(tq // sub_rows):
            rows = slice(rh * sub_rows, (rh + 1) * sub_rows)
            for lh in range(tk // sub_cols):
                cols = slice(lh * sub_cols, (lh + 1) * sub_cols)
                ik = ikt_ref[j, :, cols]
                acc = jnp.zeros((sub_rows, sub_cols), F32)
                for g in range(G):
                    a = jnp.dot(iqs_ref[g, rows, :], ik, preferred_element_type=F32)
                    wg = jnp.concatenate([wb_ref[g, rows, :]] * (sub_cols // LANES), axis=1)
                    acc = acc + wg * jnp.maximum(a, 0.0)
                sc = jnp.where(col_pos[:, cols] <= row_pos[rows], acc, -jnp.inf)
                sc_ref[j, rows, cols] = sc
                for cc in range(sub_cols // LANES):
                    c = lh * (sub_cols // LANES) + cc
                    x = sc[:, cc * LANES:(cc + 1) * LANES]
                    for r in range(R):
                        cur = cand_ref[c * R + r, rows, :]
                        cand_ref[c * R + r, rows, :] = jnp.maximum(cur, x)
                        x = jnp.minimum(cur, x)
        return carry

    lax.fori_loop(0, ntile, score_tile, 0)

    def to_key(v):
        bits = pltpu.bitcast(v, jnp.int32)
        return bits ^ ((bits >> 31) & 0x7FFFFFFF)

    def from_key(k):
        return pltpu.bitcast(k ^ ((k >> 31) & 0x7FFFFFFF), F32)

    gmax, last_kept = cand_ref[0], cand_ref[R - 1]
    for c in range(1, n_lane):
        gmax = jnp.maximum(gmax, cand_ref[c * R])
        last_kept = jnp.maximum(last_kept, cand_ref[c * R + R - 1])
    hi0 = to_key(jnp.max(gmax, axis=1, keepdims=True))
    last_kept = jnp.max(last_kept, axis=1, keepdims=True)
    n_cover = -(-topk // LANES)
    lo_f = None
    for r in range(min(R, n_cover)):
        need = -(-n_cover // (r + 1))
        if need > n_lane:
            continue
        mins = [jnp.min(cand_ref[c * R + r], axis=1, keepdims=True) for c in range(n_lane)]
        for a in range(need):
            for b2 in range(n_lane - 1, a, -1):
                hi_v, lo_v = jnp.maximum(mins[b2 - 1], mins[b2]), jnp.minimum(mins[b2 - 1], mins[b2])
                mins[b2 - 1], mins[b2] = hi_v, lo_v
        lo_f = mins[need - 1] if lo_f is None else jnp.maximum(lo_f, mins[need - 1])
    assert lo_f is not None, "topk must not exceed the number of key groups"
    lo0 = to_key(lo_f)
    few = row_pos < topk
    all_finite = to_key(jnp.full((tq, 1), -jnp.inf, F32)) + 1
    lo0 = jnp.where(few, all_finite, lo0)
    hi0 = jnp.where(few, all_finite, hi0)

    def count_ge(thr, strict=False):
        outs = []
        for r in range(n_rb):
            rows = slice(r * RB, (r + 1) * RB)
            tb = jnp.broadcast_to(thr[rows], (RB, LANES))

            def body(j, cnt, rows=rows, tb=tb):
                for c in range(n_lane):
                    st = sc_ref[j, rows, c * LANES:(c + 1) * LANES]
                    cnt = cnt + jnp.where((st > tb) if strict else (st >= tb), 1.0, 0.0)
                return cnt

            cnt = lax.fori_loop(0, ntile, body, jnp.zeros((RB, LANES), F32))
            outs.append(jnp.sum(cnt, axis=1, keepdims=True))
        return jnp.concatenate(outs, axis=0)

    def sorted_count_ge(vals, tb):
        if not vals:
            return None
        mid = len(vals) // 2
        hit = vals[mid] >= tb
        base = jnp.where(hit, float(mid + 1), 0.0)
        rest = sorted_count_ge([jnp.where(hit, lo_v, hi_v) for hi_v, lo_v in zip(vals[:mid], vals[mid + 1:])], tb)
        return base if rest is None else base + rest

    def cand_count_ge(thr):
        assert (R + 1) & R == 0, "candidate lists must have 2**d - 1 entries"
        outs = []
        for r in range(n_rb):
            rows = slice(r * RB, (r + 1) * RB)
            tb = jnp.broadcast_to(thr[rows], (RB, LANES))
            cnt = jnp.zeros((RB, LANES), F32)
            for c in range(n_lane):
                cnt = cnt + sorted_count_ge([cand_ref[c * R + q, rows, :] for q in range(R)], tb)
            outs.append(jnp.sum(cnt, axis=1, keepdims=True))
        return jnp.concatenate(outs, axis=0)

    def cand_min_ge(thr):
        outs = []
        for r in range(n_rb):
            rows = slice(r * RB, (r + 1) * RB)
            tb = jnp.broadcast_to(thr[rows], (RB, LANES))
            cur = jnp.full((RB, LANES), jnp.inf, F32)
            for q in range(n_cand):
                ck = cand_ref[q, rows, :]
                cur = jnp.minimum(cur, jnp.where(ck >= tb, ck, jnp.inf))
            outs.append(jnp.min(cur, axis=1, keepdims=True))
        return jnp.concatenate(outs, axis=0)

    def n_open(lo, hi):
        return jnp.sum(jnp.where(lo < hi, 1.0, 0.0)).astype(jnp.int32)

    def search(count):
        def halve(lo, hi, exact):
            x = lo ^ hi
            mid = (lo & hi) + (x >> 1) + (x & 1)
            cnt = count(from_key(mid))
            ge, eq = cnt >= kf, cnt == kf
            lo = jnp.where(ge, mid, lo)
            hi = jnp.where(eq, mid, jnp.where(ge, hi, mid - 1))
            return lo, hi, jnp.where(eq, 1, exact)

        def bisect(state):
            lo, hi, exact = halve(*halve(*state[:3]))
            return lo, hi, exact, n_open(lo, hi)

        lo, _, exact, _ = lax.while_loop(lambda s: s[3] > 0, bisect,
                                         (lo0, hi0, jnp.where(few, 1, 0), n_open(lo0, hi0)))
        return from_key(lo), exact

    thr, exact = search(cand_count_ge)
    thr = jnp.where(few, thr, cand_min_ge(thr))
    thr_ref[0] = thr
    thr_ref[1] = exact.astype(F32)
    n_unsure = jnp.sum(jnp.where((last_kept >= thr) & jnp.logical_not(few), 1.0, 0.0)).astype(jnp.int32)

    @pl.when(n_unsure > 0)
    def _():
        thr_full, exact_full = search(count_ge)
        thr_ref[0] = thr_full
        thr_ref[1] = exact_full.astype(F32)

    thr = thr_ref[0]
    n_tied = jnp.sum(jnp.where(thr_ref[1] > 0.0, 0.0, 1.0)).astype(jnp.int32)

    @pl.when(n_tied == 0)
    def _():
        thr_b = jnp.broadcast_to(thr, (tq, LANES))

        def write_tile(j, carry):
            st = sc_ref[j]
            parts = [jnp.where(st[:, c * LANES:(c + 1) * LANES] >= thr_b, 0.0, MASK_NEG) for c in range(n_lane)]
            bias_ref[j] = jnp.concatenate(parts, axis=1).astype(bias_ref.dtype)
            return carry

        lax.fori_loop(0, ntile, write_tile, 0)

    @pl.when(n_tied > 0)
    def _():
        need = kf - count_ge(thr, strict=True)
        before = (lax.broadcasted_iota(jnp.int32, (tk, tk), 0) < lax.broadcasted_iota(jnp.int32, (tk, tk), 1))
        before = jnp.where(before, 1.0, 0.0).astype(BF16)

        def write_tile(j, run):
            st = sc_ref[j]
            tied = jnp.where(st == thr, 1.0, 0.0)
            rank = jnp.dot(tied.astype(BF16), before, preferred_element_type=F32) + run
            sel = (st > thr) | ((st == thr) & (rank < need))
            bias_ref[j] = jnp.where(sel, 0.0, MASK_NEG).astype(bias_ref.dtype)
            return run + jnp.sum(tied, axis=1, keepdims=True)

        lax.fori_loop(0, ntile, write_tile, jnp.zeros((tq, 1), F32))

    def fill_tile(j, carry):
        bias_ref[j] = jnp.full((tq, tk), MASK_NEG, bias_ref.dtype)
        return carry

    lax.fori_loop(ntile, nT, fill_tile, 0)


def _dsa_mask(p_big, small, ikt, B, T, tq, tk, iq_off, w_off, topk):
    G = IDX_HEADS
    idim = ikt.shape[2]
    nQ, nT = T // tq, T // tk
    kern = functools.partial(_idx_kernel, topk=topk, w_off=w_off, idim=idim)
    return pl.pallas_call(
        kern,
        out_shape=jax.ShapeDtypeStruct((B, nQ, nT, tq, tk), BF16),
        grid=(B, nQ),
        in_specs=[
            pl.BlockSpec((tq, G * idim), lambda b, i: (b * nQ + i, iq_off // (G * idim))),
            pl.BlockSpec((tq, LANES), lambda b, i: (b * nQ + i, 0)),
            pl.BlockSpec((None, nT, idim, tk), lambda b, i: (b, 0, 0, 0)),
        ],
        out_specs=pl.BlockSpec((None, None, nT, tq, tk), lambda b, i: (b, i, 0, 0, 0)),
        scratch_shapes=[pltpu.VMEM((nT, tq, tk), F32),
                        pltpu.VMEM((G, tq, idim), BF16),
                        pltpu.VMEM((G, tq, LANES), F32),
                        pltpu.VMEM((tk // LANES * CAND_PER_GROUP, tq, LANES), F32),
                        pltpu.VMEM((2, tq, 1), F32)],
        compiler_params=_cparams(("parallel", "arbitrary")),
        name="dsa_index_topk",
    )(p_big, small, ikt)


def _dsa_attn_kernel(qi_ref, kj_ref, q_ref, k_ref, v_ref, bias_ref, sl_ref, o_ref, m_ref, l_ref, acc_ref, *, heads):
    p = pl.program_id(1)
    qi, kj = qi_ref[p], kj_ref[p]
    tk = bias_ref.shape[2]
    tq = bias_ref.shape[0] * bias_ref.shape[1]
    d = q_ref.shape[1] // heads
    n_lane = tk // LANES

    @pl.when(kj == 0)
    def _():
        m_ref[...] = jnp.full_like(m_ref, MASK_NEG)
        l_ref[...] = jnp.zeros_like(l_ref)
        acc_ref[...] = jnp.zeros_like(acc_ref)

    rel = (kj * tk - qi * tq) + lax.broadcasted_iota(jnp.int32, (tk, LANES), 0)
    lane = lax.broadcasted_iota(jnp.int32, (tk, LANES), 1)
    pos = jnp.where(lane < 2 * ALIBI_SPLIT, jnp.where(lane % 2 == 0, rel >> 8, rel & 255), 0)
    pos = pos.astype(F32).astype(BF16)
    bias = bias_ref[...].astype(F32).reshape(tq, tk)
    ones_col = jnp.where(lane == 0, 1.0, 0.0).astype(BF16)

    def masked_logits(h):
        cols = slice(h * d, (h + 1) * d)
        qa = jnp.concatenate([q_ref[:, cols], sl_ref[h]], axis=1)
        ka = jnp.concatenate([k_ref[:, cols], pos], axis=1)
        s = lax.dot_general(qa, ka, (((1,), (1,)), ((), ())), preferred_element_type=F32)
        return [s[:, c * LANES:(c + 1) * LANES] + bias[:, c * LANES:(c + 1) * LANES] for c in range(n_lane)]

    sb_next = masked_logits(0)
    for h in range(heads):
        cols = slice(h * d, (h + 1) * d)
        sb = sb_next
        if h + 1 < heads:
            sb_next = masked_logits(h + 1)
        mx = sb[0]
        for c in range(1, n_lane):
            mx = jnp.maximum(mx, sb[c])
        m_old = m_ref[h]
        m_new = jnp.maximum(m_old, jnp.max(mx, axis=1, keepdims=True))
        corr = jnp.exp2(m_old - m_new)
        ps = [jnp.exp2((sb[c] - m_new).astype(BF16)) for c in range(n_lane)]
        va = jnp.concatenate([v_ref[:, cols], ones_col], axis=1)
        pv = jnp.dot(jnp.concatenate(ps, axis=1), va, preferred_element_type=F32)
        l_ref[h] = corr * l_ref[h] + pv[:, d:]
        acc_ref[:, cols] = corr * acc_ref[:, cols] + pv[:, :d]
        m_ref[h] = m_new

    @pl.when((kj + 1) * tk >= (qi + 1) * tq)
    def _():
        for h in range(heads):
            cols = slice(h * d, (h + 1) * d)
            l = jnp.sum(l_ref[h], axis=1, keepdims=True)
            o_ref[:, cols] = (acc_ref[:, cols] / l).astype(o_ref.dtype)


def _alibi_columns(heads, tq):
    out = np.zeros((heads, LANES), np.float32)
    for h in range(heads):
        rest = np.float64(2.0 ** (-8.0 * (h + 1) / heads) * np.log2(np.e))
        for i in range(ALIBI_SPLIT):
            piece = np.float64(np.float32(rest).astype(BF16))
            out[h, 2 * i], out[h, 2 * i + 1] = 256.0 * piece, piece
            rest = rest - piece
    return jnp.asarray(np.broadcast_to(out.astype(BF16)[:, None, :], (heads, tq, LANES)))


def _dsa_attn(p_big, bias, B, T, tq_mask, tk, q_off, width):
    tq = tq_mask * ATTN_MASK_TILES
    assert T % tq == 0
    nQ, nT = T // tq, T // tk
    pairs = [(i, j) for i in range(nQ) for j in range(((i + 1) * tq + tk - 1) // tk)]
    qi_tbl = jnp.asarray(np.array([p[0] for p in pairs], np.int32))
    kj_tbl = jnp.asarray(np.array([p[1] for p in pairs], np.int32))
    qb = q_off // width
    heads = DSA_HEADS
    kern = functools.partial(_dsa_attn_kernel, heads=heads)
    grid_spec = pltpu.PrefetchScalarGridSpec(
        num_scalar_prefetch=2,
        grid=(B, len(pairs)),
        in_specs=[
            pl.BlockSpec((tq, width), lambda b, p, qi, kj: (b * nQ + qi[p], qb)),
            pl.BlockSpec((tk, width), lambda b, p, qi, kj: (b * nT + kj[p], qb + 1)),
            pl.BlockSpec((tk, width), lambda b, p, qi, kj: (b * nT + kj[p], qb + 2)),
            pl.BlockSpec((None, ATTN_MASK_TILES, None, tq_mask, tk), lambda b, p, qi, kj: (b, qi[p], kj[p], 0, 0)),
            pl.BlockSpec((heads, tq, LANES), lambda b, p, qi, kj: (0, 0, 0)),
        ],
        out_specs=pl.BlockSpec((tq, width), lambda b, p, qi, kj: (b * nQ + qi[p], 0)),
        scratch_shapes=[pltpu.VMEM((heads, tq, LANES), F32),
                        pltpu.VMEM((heads, tq, LANES), F32),
                        pltpu.VMEM((tq, width), F32)],
    )
    return pl.pallas_call(
        kern,
        out_shape=jax.ShapeDtypeStruct((B * T, width), BF16),
        grid_spec=grid_spec,
        compiler_params=_cparams(("arbitrary", "arbitrary")),
        name="dsa_attention",
    )(qi_tbl, kj_tbl, p_big, p_big, p_big, bias, _alibi_columns(heads, tq))


def _mem_attn_kernel(q_ref, k_ref, v_ref, o_ref):
    s = lax.dot_general(q_ref[...], k_ref[...], (((1,), (1,)), ((), ())), preferred_element_type=F32)
    m = jnp.max(s, axis=1, keepdims=True)
    p = jnp.exp(s - m)
    l = jnp.sum(p, axis=1, keepdims=True)
    o = jnp.dot(p.astype(BF16), v_ref[...], preferred_element_type=F32)
    o_ref[...] = (o / l).astype(o_ref.dtype)


def _mem_attn(p_big, kv, B, T, Mt, dm, q_off, tq):
    Hm = MEM_HEADS
    nq = T // tq
    qb = q_off // dm
    return pl.pallas_call(
        _mem_attn_kernel,
        out_shape=jax.ShapeDtypeStruct((B * T, Hm * dm), BF16),
        grid=(B, nq, Hm),
        in_specs=[
            pl.BlockSpec((tq, dm), lambda b, i, h: (b * nq + i, qb + h)),
            pl.BlockSpec((Mt, dm), lambda b, i, h: (b, h)),
            pl.BlockSpec((Mt, dm), lambda b, i, h: (b, Hm + h)),
        ],
        out_specs=pl.BlockSpec((tq, dm), lambda b, i, h: (b * nq + i, h)),
        compiler_params=_cparams(("parallel", "parallel", "arbitrary")),
        name="memory_attention",
    )(p_big, kv, kv)


def _merge_kernel(a_ref, d_ref, m_ref, wa_ref, wd_ref, wm_ref, g0_ref, g1_ref, g2_ref, b_ref, o_ref, *, tn):
    D = o_ref.shape[1]

    def products(cols):
        return (jnp.dot(a_ref[...], wa_ref[:, cols], preferred_element_type=F32),
                jnp.dot(d_ref[...], wd_ref[:, cols], preferred_element_type=F32),
                jnp.dot(m_ref[...], wm_ref[:, cols], preferred_element_type=F32))

    def gated_sum(cols, ys):
        acc = None
        for r, (g_ref, y) in enumerate(zip((g0_ref, g1_ref, g2_ref), ys)):
            bias = b_ref[:, r * D + cols.start:r * D + cols.stop]
            term = jax.nn.sigmoid(g_ref[:, cols].astype(F32) + bias) * y
            acc = term if acc is None else acc + term
        o_ref[:, cols] = acc.astype(o_ref.dtype)

    chunks = [slice(j * tn, (j + 1) * tn) for j in range(D // tn)]
    pending = None
    for cols in chunks:
        ys = products(cols)
        if pending is not None:
            gated_sum(*pending)
        pending = (cols, ys)
    gated_sum(*pending)


def _merge(o_gla, o_dsa, o_mem, wa, wd, wm, p_big, b_merge, D, gate_off, tm, tn):
    M = o_gla.shape[0]
    gb = gate_off // D
    row = lambda w: pl.BlockSpec((tm, w), lambda i: (i, 0))
    resident = lambda k: pl.BlockSpec((k, D), lambda i: (0, 0), pipeline_mode=pl.Buffered(1))
    gspec = lambda r: pl.BlockSpec((tm, D), lambda i: (i, gb + r))
    return pl.pallas_call(
        functools.partial(_merge_kernel, tn=tn),
        out_shape=jax.ShapeDtypeStruct((M, D), BF16),
        grid=(M // tm,),
        in_specs=[row(o_gla.shape[1]), row(o_dsa.shape[1]), row(o_mem.shape[1]),
                  resident(wa.shape[0]), resident(wd.shape[0]), resident(wm.shape[0]),
                  gspec(0), gspec(1), gspec(2),
                  pl.BlockSpec((1, N_BRANCH * D), lambda i: (0, 0))],
        out_specs=pl.BlockSpec((tm, D), lambda i: (i, 0)),
        compiler_params=_cparams(("parallel",)),
        name="gated_merge",
    )(o_gla, o_dsa, o_mem, wa, wd, wm, p_big, p_big, p_big, b_merge)


def _layer_norm(y, g, b):
    mu = jnp.mean(y, axis=-1, keepdims=True)
    yc = y - mu
    var = jnp.mean(yc * yc, axis=-1, keepdims=True)
    return yc * lax.rsqrt(var + LN_EPS) * g + b


def _proj_ln_kernel(mg_ref, wo_ref, x_ref, g_ref, b_ref, o_ref, *, alpha):
    tm = mg_ref.shape[0]
    halves = [slice(0, tm // 2), slice(tm // 2, tm)]
    ys = [jnp.dot(mg_ref[rows, :], wo_ref[...], preferred_element_type=F32) for rows in halves]
    for rows, y in zip(halves, ys):
        o_ref[rows, :] = _layer_norm(alpha * x_ref[rows, :] + y, g_ref[...], b_ref[...])


def _proj_ln(merged, w_o, x2, g, b, alpha, tm):
    M, D = x2.shape
    vec = pl.BlockSpec((1, D), lambda i: (0, 0))
    return pl.pallas_call(
        functools.partial(_proj_ln_kernel, alpha=alpha),
        out_shape=jax.ShapeDtypeStruct((M, D), F32),
        grid=(M // tm,),
        in_specs=[pl.BlockSpec((tm, D), lambda i: (i, 0)),
                  pl.BlockSpec((D, D), lambda i: (0, 0)),
                  pl.BlockSpec((tm, D), lambda i: (i, 0)), vec, vec],
        out_specs=pl.BlockSpec((tm, D), lambda i: (i, 0)),
        compiler_params=_cparams(("parallel",)),
        name="out_proj_ln",
    )(merged, w_o, x2, g, b)


def _mlp_kernel(x_ref, wu_ref, bu_ref, wd_ref, bd_ref, g_ref, b_ref, o_ref, xb_ref, acc_ref, *, alpha):
    f = pl.program_id(1)

    @pl.when(f == 0)
    def _():
        xb_ref[...] = x_ref[...].astype(BF16)
        acc_ref[...] = jnp.zeros_like(acc_ref)

    tf = wu_ref.shape[1]
    halves = [slice(0, tf // 2), slice(tf // 2, tf)]
    ups = [jnp.dot(xb_ref[...], wu_ref[:, cols], preferred_element_type=F32) for cols in halves]
    hs = [jnp.square(jnp.maximum(u + bu_ref[:, cols], 0.0)).astype(BF16) for u, cols in zip(ups, halves)]
    downs = [jnp.dot(h, wd_ref[cols, :], preferred_element_type=F32) for h, cols in zip(hs, halves)]
    acc_ref[...] += downs[0] + downs[1]

    @pl.when(f == pl.num_programs(1) - 1)
    def _():
        y = alpha * x_ref[...] + acc_ref[...] + bd_ref[...]
        o_ref[...] = _layer_norm(y, g_ref[...], b_ref[...])


def _mlp(x1, w_up, b_up, w_down, b_down, g, b, alpha, tm, tf):
    M, D = x1.shape
    F = w_up.shape[1]
    vec = pl.BlockSpec((1, D), lambda i, f: (0, 0))
    return pl.pallas_call(
        functools.partial(_mlp_kernel, alpha=alpha),
        out_shape=jax.ShapeDtypeStruct((M, D), F32),
        grid=(M // tm, F // tf),
        in_specs=[pl.BlockSpec((tm, D), lambda i, f: (i, 0)),
                  pl.BlockSpec((D, tf), lambda i, f: (0, f)),
                  pl.BlockSpec((1, tf), lambda i, f: (0, f)),
                  pl.BlockSpec((tf, D), lambda i, f: (f, 0)),
                  vec, vec, vec],
        out_specs=pl.BlockSpec((tm, D), lambda i, f: (i, 0)),
        scratch_shapes=[pltpu.VMEM((tm, D), BF16), pltpu.VMEM((tm, D), F32)],
        compiler_params=_cparams(("parallel", "arbitrary")),
        name="mlp_ln",
    )(x1, w_up, b_up, w_down, b_down, g, b)


def _tile(n, pref):
    t = min(n, pref)
    assert n % t == 0
    return t


def _layer(x2, mem2, B, T, w_in, w_gate_up, b_gate, norm_g, w_mem_kv, w_br_gla, w_br_dsa, w_br_mem,
           b_merge, w_o, ln1_g, ln1_b, w_up, b_up, w_down, b_down, ln2_g, ln2_b, alpha):
    M, D = x2.shape
    Mt = mem2.shape[0] // B
    dk, dv = D // 8, D // 4
    dh = D // 16
    idim = D // 32
    dm = D // 8
    H, G = GLA_HEADS, IDX_HEADS
    widths = [H * dk, H * dk, H * dv, H * dv, GLA_GATE_RANK, DSA_HEADS * dh, DSA_HEADS * dh, DSA_HEADS * dh,
              G * idim, idim, G, MEM_HEADS * dm, N_BRANCH * D]
    names = ['gla_q', 'gla_k', 'gla_v', 'gla_r', 'glr', 'dsa_q', 'dsa_k', 'dsa_v', 'idx_q', 'idx_k', 'idx_w',
             'mem_q', 'gate']
    offs = dict(zip(names, np.cumsum([0] + widths[:-1]).tolist()))
    wd = dict(zip(names, widths))
    col = lambda n, scale=None: (w_in[:, offs[n]:offs[n] + wd[n]] if scale is None
                                 else w_in[:, offs[n]:offs[n] + wd[n]] * scale)
    big_parts = [('gla_q', dk ** -0.5), ('gla_k', None), ('gla_v', None), ('gla_r', None), ('gate', None),
                 ('dsa_q', dh ** -0.5 * float(np.log2(np.e))), ('dsa_k', None), ('dsa_v', None), ('idx_q', None),
                 ('mem_q', dm ** -0.5)]
    w_big = jnp.concatenate([col(n, s) for n, s in big_parts], axis=1).astype(BF16)
    boff = dict(zip([n for n, _ in big_parts], np.cumsum([0] + [wd[n] for n, _ in big_parts][:-1]).tolist()))
    n_small = wd['glr'] + wd['idx_k'] + wd['idx_w']
    w_small = jnp.concatenate([col('glr'), col('idx_k'), col('idx_w'),
                               jnp.zeros((D, LANES - n_small), w_in.dtype)], axis=1).astype(BF16)

    p_big, small = _in_proj(x2, w_big, w_small, 1024, 1024)

    o_gla = _gla(p_big, small, w_gate_up, b_gate.reshape(1, -1), norm_g.reshape(1, -1), B, T, dk, dv,
                 _tile(T, 256))

    tq, tk = _tile(T, 256), _tile(T, 512)
    topk = min(DSA_TOPK_MAX, T // 4)
    ik = small[:, wd['glr']:wd['glr'] + idim].astype(BF16)
    ikt = ik.reshape(B, T // tk, tk, idim).transpose(0, 1, 3, 2)
    bias = _dsa_mask(p_big, small, ikt, B, T, tq, tk, boff['idx_q'], wd['glr'] + idim, topk)
    o_dsa = _dsa_attn(p_big, bias, B, T, tq, tk, boff['dsa_q'], DSA_HEADS * dh)

    kv = _matmul(mem2, w_mem_kv.astype(BF16), BF16, 512, 512, "mem_kv")
    o_mem = _mem_attn(p_big, kv, B, T, Mt, dm, boff['mem_q'], _tile(T, 1024))

    merged = _merge(o_gla, o_dsa, o_mem, w_br_gla.astype(BF16), w_br_dsa.astype(BF16), w_br_mem.astype(BF16),
                    p_big, b_merge.reshape(1, -1), D, boff['gate'], _tile(M, 512), 512)
    x1 = _proj_ln(merged, w_o.astype(BF16), x2, ln1_g.reshape(1, -1), ln1_b.reshape(1, -1), alpha, _tile(M, 512))
    return _mlp(x1, w_up.astype(BF16), b_up.reshape(1, -1), w_down.astype(BF16), b_down.reshape(1, -1),
                ln2_g.reshape(1, -1), ln2_b.reshape(1, -1), alpha, _tile(M, 512), 1024)


def kernel(x, mem, w_in, w_gla_gate_up, b_gla_gate, gla_norm_g, w_mem_kv, w_br_gla, w_br_dsa, w_br_mem,
           b_merge, w_o, ln1_g, ln1_b, w_up, b_up, w_down, b_down, ln2_g, ln2_b):
    B, T, D = x.shape
    depth = w_in.shape[0]
    alpha = (2 * depth) ** 0.25
    x2 = x.reshape(B * T, D)
    mem2 = mem.reshape(-1, D)
    for l in range(depth):
        x2 = _layer(x2, mem2, B, T, w_in[l], w_gla_gate_up[l], b_gla_gate[l], gla_norm_g[l], w_mem_kv[l],
                    w_br_gla[l], w_br_dsa[l], w_br_mem[l], b_merge[l], w_o[l], ln1_g[l], ln1_b[l],
                    w_up[l], b_up[l], w_down[l], b_down[l], ln2_g[l], ln2_b[l], alpha)
    return x2.reshape(B, T, D)
```

```python
import functools

import numpy as np
import jax
import jax.numpy as jnp
from jax import lax
from jax.experimental import pallas as pl
from jax.experimental.pallas import tpu as pltpu

F32 = jnp.float32
BF16 = jnp.bfloat16

GLA_HEADS = 4
GLA_GATE_RANK = 16
GLA_GATE_TEMP = 16.0
GLA_CHUNK = 64
DSA_HEADS = 8
IDX_HEADS = 16
DSA_TOPK_MAX = 256
MEM_HEADS = 4
N_BRANCH = 3
LN_EPS = 1e-5
RMS_EPS = 1e-6

V7X_VMEM_LIMIT_BYTES = 56 * 1024 * 1024
LANES = 128
MASK_NEG = -1e30
TILES = dict(
    in_proj=(1024, 1024),
    mem_kv=(512, 512),
    gla_rows=256,
    dsa=(256, 512),
    mem_attn_rows=1024,
    merge=(512, 512),
    out_proj_rows=512,
    mlp=(512, 1024),
)
SCORE_SUB_ROWS, SCORE_SUB_COLS = 128, 256
COUNT_SUB_ROWS = 128
ALIBI_BASE_BITS = 8
ATTN_MASK_TILES = 2
CAND_PER_GROUP = 7
ALIBI_SPLIT = 3


def _cparams(sem):
    return pltpu.CompilerParams(dimension_semantics=sem, vmem_limit_bytes=V7X_VMEM_LIMIT_BYTES)


def _mm_cast_kernel(x_ref, w_ref, o_ref, xb_ref):
    @pl.when(pl.program_id(1) == 0)
    def _():
        xb_ref[...] = x_ref[...].astype(BF16)

    o_ref[...] = jnp.dot(xb_ref[...], w_ref[...], preferred_element_type=F32).astype(o_ref.dtype)


def _matmul(x, w, out_dtype, tm, tn, name):
    M, K = x.shape
    N = w.shape[1]
    tm, tn = min(tm, M), min(tn, N)
    assert M % tm == 0 and N % tn == 0
    return pl.pallas_call(
        _mm_cast_kernel,
        out_shape=jax.ShapeDtypeStruct((M, N), out_dtype),
        grid=(M // tm, N // tn),
        in_specs=[pl.BlockSpec((tm, K), lambda i, j: (i, 0)),
                  pl.BlockSpec((K, tn), lambda i, j: (0, j))],
        out_specs=pl.BlockSpec((tm, tn), lambda i, j: (i, j)),
        scratch_shapes=[pltpu.VMEM((tm, K), BF16)],
        compiler_params=_cparams(("parallel", "arbitrary")),
        name=name,
    )(x, w)


def _in_proj_kernel(x_ref, w_ref, ws_ref, o_ref, os_ref, xb_ref):
    @pl.when(pl.program_id(1) == 0)
    def _():
        xb_ref[...] = x_ref[...].astype(BF16)
        os_ref[...] = jnp.dot(xb_ref[...], ws_ref[...], preferred_element_type=F32)

    o_ref[...] = jnp.dot(xb_ref[...], w_ref[...], preferred_element_type=F32).astype(o_ref.dtype)


def _in_proj(x, w_big, w_small, tm, tn):
    M, K = x.shape
    N, Ns = w_big.shape[1], w_small.shape[1]
    tm, tn = min(tm, M), min(tn, N)
    assert M % tm == 0 and N % tn == 0
    return pl.pallas_call(
        _in_proj_kernel,
        out_shape=(jax.ShapeDtypeStruct((M, N), BF16), jax.ShapeDtypeStruct((M, Ns), F32)),
        grid=(M // tm, N // tn),
        in_specs=[pl.BlockSpec((tm, K), lambda i, j: (i, 0)),
                  pl.BlockSpec((K, tn), lambda i, j: (0, j)),
                  pl.BlockSpec((K, Ns), lambda i, j: (0, 0))],
        out_specs=(pl.BlockSpec((tm, tn), lambda i, j: (i, j)),
                   pl.BlockSpec((tm, Ns), lambda i, j: (i, 0))),
        scratch_shapes=[pltpu.VMEM((tm, K), BF16)],
        compiler_params=_cparams(("parallel", "arbitrary")),
        name="in_proj",
    )(x, w_big, w_small)


def _log_sigmoid(z):
    return jnp.minimum(z, 0.0) - jnp.log(1.0 + jnp.exp(-jnp.abs(z)))


def _gla_kernel(q_ref, k_ref, v_ref, r_ref, sm_ref, wg_ref, bg_ref, ng_ref, o_ref, st_ref, *, chunk, rank):
    @pl.when(pl.program_id(2) == 0)
    def _():
        st_ref[...] = jnp.zeros_like(st_ref)

    tb, dk = q_ref.shape
    C = chunk
    glr = sm_ref[:, 0:rank]
    z = jnp.dot(glr, wg_ref[...], preferred_element_type=F32) + bg_ref[...]
    la = _log_sigmoid(z) * (1.0 / GLA_GATE_TEMP)
    causal = (lax.broadcasted_iota(jnp.int32, (C, C), 1) <= lax.broadcasted_iota(jnp.int32, (C, C), 0))
    tri = jnp.where(causal, 1.0, 0.0).astype(F32)

    n_chunk = tb // C
    chunk_rows = [slice(c * C, (c + 1) * C) for c in range(n_chunk)]
    bcs = [jnp.dot(tri, la[rows], preferred_element_type=F32, precision=lax.Precision.HIGHEST)
           for rows in chunk_rows]
    scs, q_its, upds, decays = [], [], [], []
    for c, rows in enumerate(chunk_rows):
        bc = bcs[c]
        br = bc[C // 2:C // 2 + 1]
        bl = bc[C - 1:C]
        qc = q_ref[rows, :].astype(F32)
        kc = k_ref[rows, :].astype(F32)
        q_in = (qc * jnp.exp(bc - br)).astype(BF16)
        k_in = (kc * jnp.exp(br - bc)).astype(BF16)
        scs.append(lax.dot_general(q_in, k_in, (((1,), (1,)), ((), ())), preferred_element_type=F32))
        q_its.append((qc * jnp.exp(bc)).astype(BF16))
        k_st = (kc * jnp.exp(bl - bc)).astype(BF16)
        upds.append(lax.dot_general(v_ref[rows, :], k_st, (((0,), (0,)), ((), ())),
                                    preferred_element_type=F32))
        decays.append(jnp.exp(bl))
    o_intra = [jnp.dot(jnp.where(causal, scs[c], 0.0).astype(BF16), v_ref[rows, :], preferred_element_type=F32)
               for c, rows in enumerate(chunk_rows)]

    for c, rows in enumerate(chunk_rows):
        st = st_ref[...]
        o = o_intra[c] + lax.dot_general(q_its[c], st.astype(BF16), (((1,), (1,)), ((), ())),
                                         preferred_element_type=F32)
        st_ref[...] = st * decays[c] + upds[c]
        o = o * lax.rsqrt(jnp.mean(o * o, axis=-1, keepdims=True) + RMS_EPS)
        r = r_ref[rows, :].astype(F32)
        o = o * ng_ref[...] * (r / (1.0 + jnp.exp(-r)))
        o_ref[rows, :] = o.astype(o_ref.dtype)


def _gla(p_big, small, w_gate_up, b_gate, norm_g, B, T, dk, dv, tb):
    M = B * T
    H = GLA_HEADS
    nb = T // tb
    row = lambda b, h, n: b * nb + n
    kern = functools.partial(_gla_kernel, chunk=GLA_CHUNK, rank=GLA_GATE_RANK)
    koff = (H * dk) // dk
    voff = (2 * H * dk) // dv
    roff = (2 * H * dk + H * dv) // dv
    return pl.pallas_call(
        kern,
        out_shape=jax.ShapeDtypeStruct((M, H * dv), BF16),
        grid=(B, H, nb),
        in_specs=[
            pl.BlockSpec((tb, dk), lambda b, h, n: (row(b, h, n), h)),
            pl.BlockSpec((tb, dk), lambda b, h, n: (row(b, h, n), koff + h)),
            pl.BlockSpec((tb, dv), lambda b, h, n: (row(b, h, n), voff + h)),
            pl.BlockSpec((tb, dv), lambda b, h, n: (row(b, h, n), roff + h)),
            pl.BlockSpec((tb, LANES), lambda b, h, n: (row(b, h, n), 0)),
            pl.BlockSpec((GLA_GATE_RANK, dk), lambda b, h, n: (0, h)),
            pl.BlockSpec((1, dk), lambda b, h, n: (0, h)),
            pl.BlockSpec((1, dv), lambda b, h, n: (0, h)),
        ],
        out_specs=pl.BlockSpec((tb, dv), lambda b, h, n: (row(b, h, n), h)),
        scratch_shapes=[pltpu.VMEM((dv, dk), F32)],
        compiler_params=_cparams(("parallel", "parallel", "arbitrary")),
        name="gla",
    )(p_big, p_big, p_big, p_big, small, w_gate_up, b_gate, norm_g)


def _idx_kernel(iq_ref, sm_ref, ikt_ref, bias_ref, sc_ref, iqs_ref, wb_ref, cand_ref, thr_ref, *, topk, w_off, idim):
    i = pl.program_id(1)
    nT, tq, tk = sc_ref.shape
    G = iqs_ref.shape[0]
    n_lane = tk // LANES
    n_cand = cand_ref.shape[0]
    R = n_cand // n_lane
    ntile = ((i + 1) * tq + tk - 1) // tk
    kf = float(topk)
    RB = min(tq, COUNT_SUB_ROWS)
    n_rb = tq // RB

    w = sm_ref[:, w_off:w_off + G] * ((G ** -0.5) * (idim ** -0.5))
    for g in range(G):
        iqs_ref[g] = iq_ref[:, g * idim:(g + 1) * idim]
        wb_ref[g] = jnp.broadcast_to(w[:, g:g + 1], (tq, LANES))
    row_pos = i * tq + lax.broadcasted_iota(jnp.int32, (tq, 1), 0)
    cand_ref[...] = jnp.full(cand_ref.shape, -jnp.inf, F32)

    sub_rows, sub_cols = min(tq, SCORE_SUB_ROWS), min(tk, SCORE_SUB_COLS)

    def score_tile(j, carry):
        col_pos = j * tk + lax.broadcasted_iota(jnp.int32, (1, tk), 1)
        for rh in range(tq // sub_rows):
            rows = slice(rh * sub_rows, (rh + 1) * sub_rows)
            for lh in range(tk // sub_cols):
                cols = slice(lh * sub_cols, (lh + 1) * sub_cols)
                ik = ikt_ref[j, :, cols]
                acc = jnp.zeros((sub_rows, sub_cols), F32)
                for g in range(G):
                    a = jnp.dot(iqs_ref[g, rows, :], ik, preferred_element_type=F32)
                    wg = jnp.concatenate([wb_ref[g, rows, :]] * (sub_cols // LANES), axis=1)
                    acc = acc + wg * jnp.maximum(a, 0.0)
                sc = jnp.where(col_pos[:, cols] <= row_pos[rows], acc, -jnp.inf)
                sc_ref[j, rows, cols] = sc
                for cc in range(sub_cols // LANES):
                    c = lh * (sub_cols // LANES) + cc
                    x = sc[:, cc * LANES:(cc + 1) * LANES]
                    for r in range(R):
                        cur = cand_ref[c * R + r, rows, :]
                        cand_ref[c * R + r, rows, :] = jnp.maximum(cur, x)
                        x = jnp.minimum(cur, x)
        return carry

    lax.fori_loop(0, ntile, score_tile, 0)

    def to_key(v):
        bits = pltpu.bitcast(v, jnp.int32)
        return bits ^ ((bits >> 31) & 0x7FFFFFFF)

    def from_key(k):
        return pltpu.bitcast(k ^ ((k >> 31) & 0x7FFFFFFF), F32)

    gmax, last_kept = cand_ref[0], cand_ref[R - 1]
    for c in range(1, n_lane):
        gmax = jnp.maximum(gmax, cand_ref[c * R])
        last_kept = jnp.maximum(last_kept, cand_ref[c * R + R - 1])
    hi0 = to_key(jnp.max(gmax, axis=1, keepdims=True))
    last_kept = jnp.max(last_kept, axis=1, keepdims=True)
    n_cover = -(-topk // LANES)
    lo_f = None
    for r in range(min(R, n_cover)):
        need = -(-n_cover // (r + 1))
        if need > n_lane:
            continue
        mins = [jnp.min(cand_ref[c * R + r], axis=1, keepdims=True) for c in range(n_lane)]
        for a in range(need):
            for b2 in range(n_lane - 1, a, -1):
                hi_v, lo_v = jnp.maximum(mins[b2 - 1], mins[b2]), jnp.minimum(mins[b2 - 1], mins[b2])
                mins[b2 - 1], mins[b2] = hi_v, lo_v
        lo_f = mins[need - 1] if lo_f is None else jnp.maximum(lo_f, mins[need - 1])
    assert lo_f is not None, "topk must not exceed the number of key groups"
    lo0 = to_key(lo_f)
    few = row_pos < topk
    all_finite = to_key(jnp.full((tq, 1), -jnp.inf, F32)) + 1
    lo0 = jnp.where(few, all_finite, lo0)
    hi0 = jnp.where(few, all_finite, hi0)

    def count_ge(thr, strict=False):
        outs = []
        for r in range(n_rb):
            rows = slice(r * RB, (r + 1) * RB)
            tb = jnp.broadcast_to(thr[rows], (RB, LANES))

            def body(j, cnt, rows=rows, tb=tb):
                for c in range(n_lane):
                    st = sc_ref[j, rows, c * LANES:(c + 1) * LANES]
                    cnt = cnt + jnp.where((st > tb) if strict else (st >= tb), 1.0, 0.0)
                return cnt

            cnt = lax.fori_loop(0, ntile, body, jnp.zeros((RB, LANES), F32))
            outs.append(jnp.sum(cnt, axis=1, keepdims=True))
        return jnp.concatenate(outs, axis=0)

    def sorted_count_ge(vals, tb):
        if not vals:
            return None
        mid = len(vals) // 2
        hit = vals[mid] >= tb
        base = jnp.where(hit, float(mid + 1), 0.0)
        rest = sorted_count_ge([jnp.where(hit, lo_v, hi_v) for hi_v, lo_v in zip(vals[:mid], vals[mid + 1:])], tb)
        return base if rest is None else base + rest

    def cand_count_ge(thr):
        assert (R + 1) & R == 0, "candidate lists must have 2**d - 1 entries"
        outs = []
        for r in range(n_rb):
            rows = slice(r * RB, (r + 1) * RB)
            tb = jnp.broadcast_to(thr[rows], (RB, LANES))
            cnt = jnp.zeros((RB, LANES), F32)
            for c in range(n_lane):
                cnt = cnt + sorted_count_ge([cand_ref[c * R + q, rows, :] for q in range(R)], tb)
            outs.append(jnp.sum(cnt, axis=1, keepdims=True))
        return jnp.concatenate(outs, axis=0)

    def cand_min_ge(thr):
        outs = []
        for r in range(n_rb):
            rows = slice(r * RB, (r + 1) * RB)
            tb = jnp.broadcast_to(thr[rows], (RB, LANES))
            cur = jnp.full((RB, LANES), jnp.inf, F32)
            for q in range(n_cand):
                ck = cand_ref[q, rows, :]
                cur = jnp.minimum(cur, jnp.where(ck >= tb, ck, jnp.inf))
            outs.append(jnp.min(cur, axis=1, keepdims=True))
        return jnp.concatenate(outs, axis=0)

    def n_open(lo, hi):
        return jnp.sum(jnp.where(lo < hi, 1.0, 0.0)).astype(jnp.int32)

    def search(count):
        def halve(lo, hi, exact):
            x = lo ^ hi
            mid = (lo & hi) + (x >> 1) + (x & 1)
            cnt = count(from_key(mid))
            ge, eq = cnt >= kf, cnt == kf
            lo = jnp.where(ge, mid, lo)
            hi = jnp.where(eq, mid, jnp.where(ge, hi, mid - 1))
            return lo, hi, jnp.where(eq, 1, exact)

        def bisect(state):
            lo, hi, exact = halve(*halve(*state[:3]))
            return lo, hi, exact, n_open(lo, hi)

        lo, _, exact, _ = lax.while_loop(lambda s: s[3] > 0, bisect,
                                         (lo0, hi0, jnp.where(few, 1, 0), n_open(lo0, hi0)))
        return from_key(lo), exact

    thr, exact = search(cand_count_ge)
    thr = jnp.where(few, thr, cand_min_ge(thr))
    thr_ref[0] = thr
    thr_ref[1] = exact.astype(F32)
    n_unsure = jnp.sum(jnp.where((last_kept >= thr) & jnp.logical_not(few), 1.0, 0.0)).astype(jnp.int32)

    @pl.when(n_unsure > 0)
    def _():
        thr_full, exact_full = search(count_ge)
        thr_ref[0] = thr_full
        thr_ref[1] = exact_full.astype(F32)

    thr = thr_ref[0]
    n_tied = jnp.sum(jnp.where(thr_ref[1] > 0.0, 0.0, 1.0)).astype(jnp.int32)

    @pl.when(n_tied == 0)
    def _():
        thr_b = jnp.broadcast_to(thr, (tq, LANES))

        def write_tile(j, carry):
            st = sc_ref[j]
            parts = [jnp.where(st[:, c * LANES:(c + 1) * LANES] >= thr_b, 0.0, MASK_NEG) for c in range(n_lane)]
            bias_ref[j] = jnp.concatenate(parts, axis=1).astype(bias_ref.dtype)
            return carry

        lax.fori_loop(0, ntile, write_tile, 0)

    @pl.when(n_tied > 0)
    def _():
        need = kf - count_ge(thr, strict=True)
        before = (lax.broadcasted_iota(jnp.int32, (tk, tk), 0) < lax.broadcasted_iota(jnp.int32, (tk, tk), 1))
        before = jnp.where(before, 1.0, 0.0).astype(BF16)

        def write_tile(j, run):
            st = sc_ref[j]
            tied = jnp.where(st == thr, 1.0, 0.0)
            rank = jnp.dot(tied.astype(BF16), before, preferred_element_type=F32) + run
            sel = (st > thr) | ((st == thr) & (rank < need))
            bias_ref[j] = jnp.where(sel, 0.0, MASK_NEG).astype(bias_ref.dtype)
            return run + jnp.sum(tied, axis=1, keepdims=True)

        lax.fori_loop(0, ntile, write_tile, jnp.zeros((tq, 1), F32))

    def fill_tile(j, carry):
        bias_ref[j] = jnp.full((tq, tk), MASK_NEG, bias_ref.dtype)
        return carry

    lax.fori_loop(ntile, nT, fill_tile, 0)


def _dsa_mask(p_big, small, ikt, B, T, tq, tk, iq_off, w_off, topk):
    G = IDX_HEADS
    idim = ikt.shape[2]
    nQ, nT = T // tq, T // tk
    kern = functools.partial(_idx_kernel, topk=topk, w_off=w_off, idim=idim)
    return pl.pallas_call(
        kern,
        out_shape=jax.ShapeDtypeStruct((B, nQ, nT, tq, tk), BF16),
        grid=(B, nQ),
        in_specs=[
            pl.BlockSpec((tq, G * idim), lambda b, i: (b * nQ + i, iq_off // (G * idim))),
            pl.BlockSpec((tq, LANES), lambda b, i: (b * nQ + i, 0)),
            pl.BlockSpec((None, nT, idim, tk), lambda b, i: (b, 0, 0, 0)),
        ],
        out_specs=pl.BlockSpec((None, None, nT, tq, tk), lambda b, i: (b, i, 0, 0, 0)),
        scratch_shapes=[pltpu.VMEM((nT, tq, tk), F32),
                        pltpu.VMEM((G, tq, idim), BF16),
                        pltpu.VMEM((G, tq, LANES), F32),
                        pltpu.VMEM((tk // LANES * CAND_PER_GROUP, tq, LANES), F32),
                        pltpu.VMEM((2, tq, 1), F32)],
        compiler_params=_cparams(("parallel", "arbitrary")),
        name="dsa_index_topk",
    )(p_big, small, ikt)


def _dsa_attn_kernel(qi_ref, kj_ref, q_ref, k_ref, v_ref, bias_ref, sl_ref, o_ref, m_ref, l_ref, acc_ref, *, heads):
    p = pl.program_id(1)
    qi, kj = qi_ref[p], kj_ref[p]
    tk = bias_ref.shape[2]
    tq = bias_ref.shape[0] * bias_ref.shape[1]
    d = q_ref.shape[1] // heads
    n_lane = tk // LANES

    @pl.when(kj == 0)
    def _():
        m_ref[...] = jnp.full_like(m_ref, MASK_NEG)
        l_ref[...] = jnp.zeros_like(l_ref)
        acc_ref[...] = jnp.zeros_like(acc_ref)

    rel = (kj * tk - qi * tq) + lax.broadcasted_iota(jnp.int32, (tk, LANES), 0)
    lane = lax.broadcasted_iota(jnp.int32, (tk, LANES), 1)
    pos = jnp.where(lane < 2 * ALIBI_SPLIT,
                    jnp.where(lane % 2 == 0, rel >> ALIBI_BASE_BITS, rel & (2 ** ALIBI_BASE_BITS - 1)), 0)
    pos = pos.astype(F32).astype(BF16)
    bias = bias_ref[...].astype(F32).reshape(tq, tk)
    ones_col = jnp.where(lane == 0, 1.0, 0.0).astype(BF16)

    def masked_logits(h):
        cols = slice(h * d, (h + 1) * d)
        qa = jnp.concatenate([q_ref[:, cols], sl_ref[h]], axis=1)
        ka = jnp.concatenate([k_ref[:, cols], pos], axis=1)
        s = lax.dot_general(qa, ka, (((1,), (1,)), ((), ())), preferred_element_type=F32)
        return [s[:, c * LANES:(c + 1) * LANES] + bias[:, c * LANES:(c + 1) * LANES] for c in range(n_lane)]

    sb_next = masked_logits(0)
    for h in range(heads):
        cols = slice(h * d, (h + 1) * d)
        sb = sb_next
        if h + 1 < heads:
            sb_next = masked_logits(h + 1)
        mx = sb[0]
        for c in range(1, n_lane):
            mx = jnp.maximum(mx, sb[c])
        m_old = m_ref[h]
        m_new = jnp.maximum(m_old, jnp.max(mx, axis=1, keepdims=True))
        corr = jnp.exp2(m_old - m_new)
        ps = [jnp.exp2((sb[c] - m_new).astype(BF16)) for c in range(n_lane)]
        va = jnp.concatenate([v_ref[:, cols], ones_col], axis=1)
        pv = jnp.dot(jnp.concatenate(ps, axis=1), va, preferred_element_type=F32)
        l_ref[h] = corr * l_ref[h] + pv[:, d:]
        acc_ref[:, cols] = corr * acc_ref[:, cols] + pv[:, :d]
        m_ref[h] = m_new

    @pl.when((kj + 1) * tk >= (qi + 1) * tq)
    def _():
        for h in range(heads):
            cols = slice(h * d, (h + 1) * d)
            l = jnp.sum(l_ref[h], axis=1, keepdims=True)
            o_ref[:, cols] = (acc_ref[:, cols] / l).astype(o_ref.dtype)


def _alibi_columns(heads, tq):
    out = np.zeros((heads, LANES), np.float32)
    for h in range(heads):
        rest = np.float64(2.0 ** (-8.0 * (h + 1) / heads) * np.log2(np.e))
        for i in range(ALIBI_SPLIT):
            piece = np.float64(np.float32(rest).astype(BF16))
            out[h, 2 * i], out[h, 2 * i + 1] = 2.0 ** ALIBI_BASE_BITS * piece, piece
            rest = rest - piece
    return jnp.asarray(np.broadcast_to(out.astype(BF16)[:, None, :], (heads, tq, LANES)))


def _dsa_attn(p_big, bias, B, T, tq_mask, tk, q_off, width):
    tq = tq_mask * ATTN_MASK_TILES
    assert T % tq == 0
    nQ, nT = T // tq, T // tk
    pairs = [(i, j) for i in range(nQ) for j in range(((i + 1) * tq + tk - 1) // tk)]
    qi_tbl = jnp.asarray(np.array([p[0] for p in pairs], np.int32))
    kj_tbl = jnp.asarray(np.array([p[1] for p in pairs], np.int32))
    qb = q_off // width
    heads = DSA_HEADS
    kern = functools.partial(_dsa_attn_kernel, heads=heads)
    grid_spec = pltpu.PrefetchScalarGridSpec(
        num_scalar_prefetch=2,
        grid=(B, len(pairs)),
        in_specs=[
            pl.BlockSpec((tq, width), lambda b, p, qi, kj: (b * nQ + qi[p], qb)),
            pl.BlockSpec((tk, width), lambda b, p, qi, kj: (b * nT + kj[p], qb + 1)),
            pl.BlockSpec((tk, width), lambda b, p, qi, kj: (b * nT + kj[p], qb + 2)),
            pl.BlockSpec((None, ATTN_MASK_TILES, None, tq_mask, tk), lambda b, p, qi, kj: (b, qi[p], kj[p], 0, 0)),
            pl.BlockSpec((heads, tq, LANES), lambda b, p, qi, kj: (0, 0, 0)),
        ],
        out_specs=pl.BlockSpec((tq, width), lambda b, p, qi, kj: (b * nQ + qi[p], 0)),
        scratch_shapes=[pltpu.VMEM((heads, tq, LANES), F32),
                        pltpu.VMEM((heads, tq, LANES), F32),
                        pltpu.VMEM((tq, width), F32)],
    )
    return pl.pallas_call(
        kern,
        out_shape=jax.ShapeDtypeStruct((B * T, width), BF16),
        grid_spec=grid_spec,
        compiler_params=_cparams(("arbitrary", "arbitrary")),
        name="dsa_attention",
    )(qi_tbl, kj_tbl, p_big, p_big, p_big, bias, _alibi_columns(heads, tq))


def _mem_attn_kernel(q_ref, k_ref, v_ref, o_ref):
    s = lax.dot_general(q_ref[...], k_ref[...], (((1,), (1,)), ((), ())), preferred_element_type=F32)
    m = jnp.max(s, axis=1, keepdims=True)
    p = jnp.exp(s - m)
    l = jnp.sum(p, axis=1, keepdims=True)
    o = jnp.dot(p.astype(BF16), v_ref[...], preferred_element_type=F32)
    o_ref[...] = (o / l).astype(o_ref.dtype)


def _mem_attn(p_big, kv, B, T, Mt, dm, q_off, tq):
    Hm = MEM_HEADS
    nq = T // tq
    qb = q_off // dm
    return pl.pallas_call(
        _mem_attn_kernel,
        out_shape=jax.ShapeDtypeStruct((B * T, Hm * dm), BF16),
        grid=(B, nq, Hm),
        in_specs=[
            pl.BlockSpec((tq, dm), lambda b, i, h: (b * nq + i, qb + h)),
            pl.BlockSpec((Mt, dm), lambda b, i, h: (b, h)),
            pl.BlockSpec((Mt, dm), lambda b, i, h: (b, Hm + h)),
        ],
        out_specs=pl.BlockSpec((tq, dm), lambda b, i, h: (b * nq + i, h)),
        compiler_params=_cparams(("parallel", "parallel", "arbitrary")),
        name="memory_attention",
    )(p_big, kv, kv)


def _merge_kernel(a_ref, d_ref, m_ref, wa_ref, wd_ref, wm_ref, g0_ref, g1_ref, g2_ref, b_ref, o_ref, *, tn):
    D = o_ref.shape[1]

    def products(cols):
        return (jnp.dot(a_ref[...], wa_ref[:, cols], preferred_element_type=F32),
                jnp.dot(d_ref[...], wd_ref[:, cols], preferred_element_type=F32),
                jnp.dot(m_ref[...], wm_ref[:, cols], preferred_element_type=F32))

    def gated_sum(cols, ys):
        acc = None
        for r, (g_ref, y) in enumerate(zip((g0_ref, g1_ref, g2_ref), ys)):
            bias = b_ref[:, r * D + cols.start:r * D + cols.stop]
            term = jax.nn.sigmoid(g_ref[:, cols].astype(F32) + bias) * y
            acc = term if acc is None else acc + term
        o_ref[:, cols] = acc.astype(o_ref.dtype)

    chunks = [slice(j * tn, (j + 1) * tn) for j in range(D // tn)]
    pending = None
    for cols in chunks:
        ys = products(cols)
        if pending is not None:
            gated_sum(*pending)
        pending = (cols, ys)
    gated_sum(*pending)


def _merge(o_gla, o_dsa, o_mem, wa, wd, wm, p_big, b_merge, D, gate_off, tm, tn):
    M = o_gla.shape[0]
    assert gate_off % D == 0 and D % tn == 0
    gb = gate_off // D
    row = lambda w: pl.BlockSpec((tm, w), lambda i: (i, 0))
    resident = lambda k: pl.BlockSpec((k, D), lambda i: (0, 0), pipeline_mode=pl.Buffered(1))
    gspec = lambda r: pl.BlockSpec((tm, D), lambda i: (i, gb + r))
    return pl.pallas_call(
        functools.partial(_merge_kernel, tn=tn),
        out_shape=jax.ShapeDtypeStruct((M, D), BF16),
        grid=(M // tm,),
        in_specs=[row(o_gla.shape[1]), row(o_dsa.shape[1]), row(o_mem.shape[1]),
                  resident(wa.shape[0]), resident(wd.shape[0]), resident(wm.shape[0]),
                  gspec(0), gspec(1), gspec(2),
                  pl.BlockSpec((1, N_BRANCH * D), lambda i: (0, 0))],
        out_specs=pl.BlockSpec((tm, D), lambda i: (i, 0)),
        compiler_params=_cparams(("parallel",)),
        name="gated_merge",
    )(o_gla, o_dsa, o_mem, wa, wd, wm, p_big, p_big, p_big, b_merge)


def _layer_norm(y, g, b):
    mu = jnp.mean(y, axis=-1, keepdims=True)
    yc = y - mu
    var = jnp.mean(yc * yc, axis=-1, keepdims=True)
    return yc * lax.rsqrt(var + LN_EPS) * g + b


def _proj_ln_kernel(mg_ref, wo_ref, x_ref, g_ref, b_ref, o_ref, *, alpha):
    tm = mg_ref.shape[0]
    halves = [slice(0, tm // 2), slice(tm // 2, tm)]
    ys = [jnp.dot(mg_ref[rows, :], wo_ref[...], preferred_element_type=F32) for rows in halves]
    for rows, y in zip(halves, ys):
        o_ref[rows, :] = _layer_norm(alpha * x_ref[rows, :] + y, g_ref[...], b_ref[...])


def _proj_ln(merged, w_o, x2, g, b, alpha, tm):
    M, D = x2.shape
    vec = pl.BlockSpec((1, D), lambda i: (0, 0))
    return pl.pallas_call(
        functools.partial(_proj_ln_kernel, alpha=alpha),
        out_shape=jax.ShapeDtypeStruct((M, D), F32),
        grid=(M // tm,),
        in_specs=[pl.BlockSpec((tm, D), lambda i: (i, 0)),
                  pl.BlockSpec((D, D), lambda i: (0, 0)),
                  pl.BlockSpec((tm, D), lambda i: (i, 0)), vec, vec],
        out_specs=pl.BlockSpec((tm, D), lambda i: (i, 0)),
        compiler_params=_cparams(("parallel",)),
        name="out_proj_ln",
    )(merged, w_o, x2, g, b)


def _mlp_kernel(x_ref, wu_ref, bu_ref, wd_ref, bd_ref, g_ref, b_ref, o_ref, xb_ref, acc_ref, *, alpha):
    f = pl.program_id(1)

    @pl.when(f == 0)
    def _():
        xb_ref[...] = x_ref[...].astype(BF16)
        acc_ref[...] = jnp.zeros_like(acc_ref)

    tf = wu_ref.shape[1]
    halves = [slice(0, tf // 2), slice(tf // 2, tf)]
    ups = [jnp.dot(xb_ref[...], wu_ref[:, cols], preferred_element_type=F32) for cols in halves]
    hs = [jnp.square(jnp.maximum(u + bu_ref[:, cols], 0.0)).astype(BF16) for u, cols in zip(ups, halves)]
    downs = [jnp.dot(h, wd_ref[cols, :], preferred_element_type=F32) for h, cols in zip(hs, halves)]
    acc_ref[...] += downs[0] + downs[1]

    @pl.when(f == pl.num_programs(1) - 1)
    def _():
        y = alpha * x_ref[...] + acc_ref[...] + bd_ref[...]
        o_ref[...] = _layer_norm(y, g_ref[...], b_ref[...])


def _mlp(x1, w_up, b_up, w_down, b_down, g, b, alpha, tm, tf):
    M, D = x1.shape
    F = w_up.shape[1]
    vec = pl.BlockSpec((1, D), lambda i, f: (0, 0))
    return pl.pallas_call(
        functools.partial(_mlp_kernel, alpha=alpha),
        out_shape=jax.ShapeDtypeStruct((M, D), F32),
        grid=(M // tm, F // tf),
        in_specs=[pl.BlockSpec((tm, D), lambda i, f: (i, 0)),
                  pl.BlockSpec((D, tf), lambda i, f: (0, f)),
                  pl.BlockSpec((1, tf), lambda i, f: (0, f)),
                  pl.BlockSpec((tf, D), lambda i, f: (f, 0)),
                  vec, vec, vec],
        out_specs=pl.BlockSpec((tm, D), lambda i, f: (i, 0)),
        scratch_shapes=[pltpu.VMEM((tm, D), BF16), pltpu.VMEM((tm, D), F32)],
        compiler_params=_cparams(("parallel", "arbitrary")),
        name="mlp_ln",
    )(x1, w_up, b_up, w_down, b_down, g, b)


def _tile(n, pref):
    t = min(n, pref)
    assert n % t == 0
    return t


def _layer(x2, mem2, B, T, w_in, w_gate_up, b_gate, norm_g, w_mem_kv, w_br_gla, w_br_dsa, w_br_mem,
           b_merge, w_o, ln1_g, ln1_b, w_up, b_up, w_down, b_down, ln2_g, ln2_b, alpha):
    M, D = x2.shape
    Mt = mem2.shape[0] // B
    dk, dv = D // 8, D // 4
    dh = D // 16
    idim = D // 32
    dm = D // 8
    H, G = GLA_HEADS, IDX_HEADS
    widths = [H * dk, H * dk, H * dv, H * dv, GLA_GATE_RANK, DSA_HEADS * dh, DSA_HEADS * dh, DSA_HEADS * dh,
              G * idim, idim, G, MEM_HEADS * dm, N_BRANCH * D]
    names = ['gla_q', 'gla_k', 'gla_v', 'gla_r', 'glr', 'dsa_q', 'dsa_k', 'dsa_v', 'idx_q', 'idx_k', 'idx_w',
             'mem_q', 'gate']
    offs = dict(zip(names, np.cumsum([0] + widths[:-1]).tolist()))
    wd = dict(zip(names, widths))
    col = lambda n, scale=None: (w_in[:, offs[n]:offs[n] + wd[n]] if scale is None
                                 else w_in[:, offs[n]:offs[n] + wd[n]] * scale)
    big_parts = [('gla_q', dk ** -0.5), ('gla_k', None), ('gla_v', None), ('gla_r', None), ('gate', None),
                 ('dsa_q', dh ** -0.5 * float(np.log2(np.e))), ('dsa_k', None), ('dsa_v', None), ('idx_q', None),
                 ('mem_q', dm ** -0.5)]
    w_big = jnp.concatenate([col(n, s) for n, s in big_parts], axis=1).astype(BF16)
    boff = dict(zip([n for n, _ in big_parts], np.cumsum([0] + [wd[n] for n, _ in big_parts][:-1]).tolist()))
    n_small = wd['glr'] + wd['idx_k'] + wd['idx_w']
    w_small = jnp.concatenate([col('glr'), col('idx_k'), col('idx_w'),
                               jnp.zeros((D, LANES - n_small), w_in.dtype)], axis=1).astype(BF16)

    p_big, small = _in_proj(x2, w_big, w_small, *TILES['in_proj'])

    o_gla = _gla(p_big, small, w_gate_up, b_gate.reshape(1, -1), norm_g.reshape(1, -1), B, T, dk, dv,
                 _tile(T, TILES['gla_rows']))

    tq, tk = _tile(T, TILES['dsa'][0]), _tile(T, TILES['dsa'][1])
    topk = min(DSA_TOPK_MAX, T // 4)
    ik = small[:, wd['glr']:wd['glr'] + idim].astype(BF16)
    ikt = ik.reshape(B, T // tk, tk, idim).transpose(0, 1, 3, 2)
    bias = _dsa_mask(p_big, small, ikt, B, T, tq, tk, boff['idx_q'], wd['glr'] + idim, topk)
    o_dsa = _dsa_attn(p_big, bias, B, T, tq, tk, boff['dsa_q'], DSA_HEADS * dh)

    kv = _matmul(mem2, w_mem_kv.astype(BF16), BF16, *TILES['mem_kv'], "mem_kv")
    o_mem = _mem_attn(p_big, kv, B, T, Mt, dm, boff['mem_q'], _tile(T, TILES['mem_attn_rows']))

    merged = _merge(o_gla, o_dsa, o_mem, w_br_gla.astype(BF16), w_br_dsa.astype(BF16), w_br_mem.astype(BF16),
                    p_big, b_merge.reshape(1, -1), D, boff['gate'], _tile(M, TILES['merge'][0]), TILES['merge'][1])
    x1 = _proj_ln(merged, w_o.astype(BF16), x2, ln1_g.reshape(1, -1), ln1_b.reshape(1, -1), alpha,
                  _tile(M, TILES['out_proj_rows']))
    return _mlp(x1, w_up.astype(BF16), b_up.reshape(1, -1), w_down.astype(BF16), b_down.reshape(1, -1),
                ln2_g.reshape(1, -1), ln2_b.reshape(1, -1), alpha, _tile(M, TILES['mlp'][0]), TILES['mlp'][1])


def kernel(x, mem, w_in, w_gla_gate_up, b_gla_gate, gla_norm_g, w_mem_kv, w_br_gla, w_br_dsa, w_br_mem,
           b_merge, w_o, ln1_g, ln1_b, w_up, b_up, w_down, b_down, ln2_g, ln2_b):
    B, T, D = x.shape
    depth = w_in.shape[0]
    alpha = (2 * depth) ** 0.25
    x2 = x.reshape(B * T, D)
    mem2 = mem.reshape(-1, D)
    for l in range(depth):
        x2 = _layer(x2, mem2, B, T, w_in[l], w_gla_gate_up[l], b_gla_gate[l], gla_norm_g[l], w_mem_kv[l],
                    w_br_gla[l], w_br_dsa[l], w_br_mem[l], b_merge[l], w_o[l], ln1_g[l], ln1_b[l],
                    w_up[l], b_up[l], w_down[l], b_down[l], ln2_g[l], ln2_b[l], alpha)
    return x2.reshape(B, T, D)
```

```python
import functools

import numpy as np
import jax
import jax.numpy as jnp
from jax import lax
from jax.experimental import pallas as pl
from jax.experimental.pallas import tpu as pltpu

F32 = jnp.float32
BF16 = jnp.bfloat16

GLA_HEADS = 4
GLA_GATE_RANK = 16
GLA_GATE_TEMP = 16.0
GLA_CHUNK = 64
DSA_HEADS = 8
IDX_HEADS = 16
DSA_TOPK_MAX = 256
MEM_HEADS = 4
N_BRANCH = 3
LN_EPS = 1e-5
RMS_EPS = 1e-6

V7X_VMEM_LIMIT_BYTES = 56 * 1024 * 1024
LANES = 128
MASK_NEG = -1e30
TILES = dict(
    in_proj=(1024, 1024),
    mem_kv=(512, 512),
    gla_rows=512,
    dsa=(256, 512),
    mem_attn_rows=1024,
    merge=(512, 512),
    out_proj_rows=512,
    mlp=(512, 1024),
)
SCORE_SUB_ROWS, SCORE_SUB_COLS = 128, 256
COUNT_SUB_ROWS = 128
ALIBI_BASE_BITS = 8
ATTN_MASK_TILES = 2
CAND_PER_GROUP = 7
ALIBI_SPLIT = 3


def _cparams(sem):
    return pltpu.CompilerParams(dimension_semantics=sem, vmem_limit_bytes=V7X_VMEM_LIMIT_BYTES)


def _mm_cast_kernel(x_ref, w_ref, o_ref, xb_ref):
    @pl.when(pl.program_id(1) == 0)
    def _():
        xb_ref[...] = x_ref[...].astype(BF16)

    o_ref[...] = jnp.dot(xb_ref[...], w_ref[...], preferred_element_type=F32).astype(o_ref.dtype)


def _matmul(x, w, out_dtype, tm, tn, name):
    M, K = x.shape
    N = w.shape[1]
    tm, tn = min(tm, M), min(tn, N)
    assert M % tm == 0 and N % tn == 0
    return pl.pallas_call(
        _mm_cast_kernel,
        out_shape=jax.ShapeDtypeStruct((M, N), out_dtype),
        grid=(M // tm, N // tn),
        in_specs=[pl.BlockSpec((tm, K), lambda i, j: (i, 0)),
                  pl.BlockSpec((K, tn), lambda i, j: (0, j))],
        out_specs=pl.BlockSpec((tm, tn), lambda i, j: (i, j)),
        scratch_shapes=[pltpu.VMEM((tm, K), BF16)],
        compiler_params=_cparams(("parallel", "arbitrary")),
        name=name,
    )(x, w)


def _in_proj_kernel(x_ref, w_ref, ws_ref, o_ref, os_ref, xb_ref):
    @pl.when(pl.program_id(1) == 0)
    def _():
        xb_ref[...] = x_ref[...].astype(BF16)
        os_ref[...] = jnp.dot(xb_ref[...], ws_ref[...], preferred_element_type=F32)

    o_ref[...] = jnp.dot(xb_ref[...], w_ref[...], preferred_element_type=F32).astype(o_ref.dtype)


def _in_proj(x, w_big, w_small, tm, tn):
    M, K = x.shape
    N, Ns = w_big.shape[1], w_small.shape[1]
    tm, tn = min(tm, M), min(tn, N)
    assert M % tm == 0 and N % tn == 0
    return pl.pallas_call(
        _in_proj_kernel,
        out_shape=(jax.ShapeDtypeStruct((M, N), BF16), jax.ShapeDtypeStruct((M, Ns), F32)),
        grid=(M // tm, N // tn),
        in_specs=[pl.BlockSpec((tm, K), lambda i, j: (i, 0)),
                  pl.BlockSpec((K, tn), lambda i, j: (0, j)),
                  pl.BlockSpec((K, Ns), lambda i, j: (0, 0))],
        out_specs=(pl.BlockSpec((tm, tn), lambda i, j: (i, j)),
                   pl.BlockSpec((tm, Ns), lambda i, j: (i, 0))),
        scratch_shapes=[pltpu.VMEM((tm, K), BF16)],
        compiler_params=_cparams(("parallel", "arbitrary")),
        name="in_proj",
    )(x, w_big, w_small)


def _log_sigmoid(z):
    return jnp.minimum(z, 0.0) - jnp.log(1.0 + jnp.exp(-jnp.abs(z)))


def _gla_kernel(q_ref, k_ref, v_ref, r_ref, sm_ref, wg_ref, bg_ref, ng_ref, o_ref, st_ref, *, chunk, rank):
    @pl.when(pl.program_id(2) == 0)
    def _():
        st_ref[...] = jnp.zeros_like(st_ref)

    tb, dk = q_ref.shape
    C = chunk
    glr = sm_ref[:, 0:rank]
    z = jnp.dot(glr, wg_ref[...], preferred_element_type=F32) + bg_ref[...]
    la = _log_sigmoid(z) * (1.0 / GLA_GATE_TEMP)
    causal = (lax.broadcasted_iota(jnp.int32, (C, C), 1) <= lax.broadcasted_iota(jnp.int32, (C, C), 0))
    tri = jnp.where(causal, 1.0, 0.0).astype(F32)

    n_chunk = tb // C
    chunk_rows = [slice(c * C, (c + 1) * C) for c in range(n_chunk)]
    bcs = [jnp.dot(tri, la[rows], preferred_element_type=F32, precision=lax.Precision.HIGHEST)
           for rows in chunk_rows]
    scs, q_its, upds, decays = [], [], [], []
    for c, rows in enumerate(chunk_rows):
        bc = bcs[c]
        br = bc[C // 2:C // 2 + 1]
        bl = bc[C - 1:C]
        qc = q_ref[rows, :].astype(F32)
        kc = k_ref[rows, :].astype(F32)
        q_in = (qc * jnp.exp(bc - br)).astype(BF16)
        k_in = (kc * jnp.exp(br - bc)).astype(BF16)
        scs.append(lax.dot_general(q_in, k_in, (((1,), (1,)), ((), ())), preferred_element_type=F32))
        q_its.append((qc * jnp.exp(bc)).astype(BF16))
        k_st = (kc * jnp.exp(bl - bc)).astype(BF16)
        upds.append(lax.dot_general(v_ref[rows, :], k_st, (((0,), (0,)), ((), ())),
                                    preferred_element_type=F32))
        decays.append(jnp.exp(bl))
    o_intra = [jnp.dot(jnp.where(causal, scs[c], 0.0).astype(BF16), v_ref[rows, :], preferred_element_type=F32)
               for c, rows in enumerate(chunk_rows)]

    for c, rows in enumerate(chunk_rows):
        st = st_ref[...]
        o = o_intra[c] + lax.dot_general(q_its[c], st.astype(BF16), (((1,), (1,)), ((), ())),
                                         preferred_element_type=F32)
        st_ref[...] = st * decays[c] + upds[c]
        o = o * lax.rsqrt(jnp.mean(o * o, axis=-1, keepdims=True) + RMS_EPS)
        r = r_ref[rows, :].astype(F32)
        o = o * ng_ref[...] * (r / (1.0 + jnp.exp(-r)))
        o_ref[rows, :] = o.astype(o_ref.dtype)


def _gla(p_big, small, w_gate_up, b_gate, norm_g, B, T, dk, dv, tb):
    M = B * T
    H = GLA_HEADS
    nb = T // tb
    row = lambda b, h, n: b * nb + n
    kern = functools.partial(_gla_kernel, chunk=GLA_CHUNK, rank=GLA_GATE_RANK)
    koff = (H * dk) // dk
    voff = (2 * H * dk) // dv
    roff = (2 * H * dk + H * dv) // dv
    return pl.pallas_call(
        kern,
        out_shape=jax.ShapeDtypeStruct((M, H * dv), BF16),
        grid=(B, H, nb),
        in_specs=[
            pl.BlockSpec((tb, dk), lambda b, h, n: (row(b, h, n), h)),
            pl.BlockSpec((tb, dk), lambda b, h, n: (row(b, h, n), koff + h)),
            pl.BlockSpec((tb, dv), lambda b, h, n: (row(b, h, n), voff + h)),
            pl.BlockSpec((tb, dv), lambda b, h, n: (row(b, h, n), roff + h)),
            pl.BlockSpec((tb, LANES), lambda b, h, n: (row(b, h, n), 0)),
            pl.BlockSpec((GLA_GATE_RANK, dk), lambda b, h, n: (0, h)),
            pl.BlockSpec((1, dk), lambda b, h, n: (0, h)),
            pl.BlockSpec((1, dv), lambda b, h, n: (0, h)),
        ],
        out_specs=pl.BlockSpec((tb, dv), lambda b, h, n: (row(b, h, n), h)),
        scratch_shapes=[pltpu.VMEM((dv, dk), F32)],
        compiler_params=_cparams(("parallel", "parallel", "arbitrary")),
        name="gla",
    )(p_big, p_big, p_big, p_big, small, w_gate_up, b_gate, norm_g)


def _idx_kernel(iq_ref, sm_ref, ikt_ref, bias_ref, sc_ref, iqs_ref, wb_ref, cand_ref, thr_ref, *, topk, w_off, idim):
    i = pl.program_id(1)
    nT, tq, tk = sc_ref.shape
    G = iqs_ref.shape[0]
    n_lane = tk // LANES
    n_cand = cand_ref.shape[0]
    R = n_cand // n_lane
    ntile = ((i + 1) * tq + tk - 1) // tk
    kf = float(topk)
    RB = min(tq, COUNT_SUB_ROWS)
    n_rb = tq // RB

    w = sm_ref[:, w_off:w_off + G] * ((G ** -0.5) * (idim ** -0.5))
    for g in range(G):
        iqs_ref[g] = iq_ref[:, g * idim:(g + 1) * idim]
        wb_ref[g] = jnp.broadcast_to(w[:, g:g + 1], (tq, LANES))
    row_pos = i * tq + lax.broadcasted_iota(jnp.int32, (tq, 1), 0)
    cand_ref[...] = jnp.full(cand_ref.shape, -jnp.inf, F32)

    sub_rows, sub_cols = min(tq, SCORE_SUB_ROWS), min(tk, SCORE_SUB_COLS)

    def score_tile(j, carry):
        col_pos = j * tk + lax.broadcasted_iota(jnp.int32, (1, tk), 1)
        for rh in range(tq // sub_rows):
            rows = slice(rh * sub_rows, (rh + 1) * sub_rows)
            for lh in range(tk // sub_cols):
                cols = slice(lh * sub_cols, (lh + 1) * sub_cols)
                ik = ikt_ref[j, :, cols]
                acc = jnp.zeros((sub_rows, sub_cols), F32)
                for g in range(G):
                    a = jnp.dot(iqs_ref[g, rows, :], ik, preferred_element_type=F32)
                    wg = jnp.concatenate([wb_ref[g, rows, :]] * (sub_cols // LANES), axis=1)
                    acc = acc + wg * jnp.maximum(a, 0.0)
                sc = jnp.where(col_pos[:, cols] <= row_pos[rows], acc, -jnp.inf)
                sc_ref[j, rows, cols] = sc
                for cc in range(sub_cols // LANES):
                    c = lh * (sub_cols // LANES) + cc
                    x = sc[:, cc * LANES:(cc + 1) * LANES]
                    for r in range(R):
                        cur = cand_ref[c * R + r, rows, :]
                        cand_ref[c * R + r, rows, :] = jnp.maximum(cur, x)
                        x = jnp.minimum(cur, x)
        return carry

    lax.fori_loop(0, ntile, score_tile, 0)

    def to_key(v):
        bits = pltpu.bitcast(v, jnp.int32)
        return bits ^ ((bits >> 31) & 0x7FFFFFFF)

    def from_key(k):
        return pltpu.bitcast(k ^ ((k >> 31) & 0x7FFFFFFF), F32)

    gmax, last_kept = cand_ref[0], cand_ref[R - 1]
    for c in range(1, n_lane):
        gmax = jnp.maximum(gmax, cand_ref[c * R])
        last_kept = jnp.maximum(last_kept, cand_ref[c * R + R - 1])
    hi0 = to_key(jnp.max(gmax, axis=1, keepdims=True))
    last_kept = jnp.max(last_kept, axis=1, keepdims=True)
    n_cover = -(-topk // LANES)
    lo_f = None
    for r in range(min(R, n_cover)):
        need = -(-n_cover // (r + 1))
        if need > n_lane:
            continue
        mins = [jnp.min(cand_ref[c * R + r], axis=1, keepdims=True) for c in range(n_lane)]
        for a in range(need):
            for b2 in range(n_lane - 1, a, -1):
                hi_v, lo_v = jnp.maximum(mins[b2 - 1], mins[b2]), jnp.minimum(mins[b2 - 1], mins[b2])
                mins[b2 - 1], mins[b2] = hi_v, lo_v
        lo_f = mins[need - 1] if lo_f is None else jnp.maximum(lo_f, mins[need - 1])
    assert lo_f is not None, "topk must not exceed the number of key groups"
    lo0 = to_key(lo_f)
    few = row_pos < topk
    all_finite = to_key(jnp.full((tq, 1), -jnp.inf, F32)) + 1
    lo0 = jnp.where(few, all_finite, lo0)
    hi0 = jnp.where(few, all_finite, hi0)

    def count_ge(thr, strict=False):
        outs = []
        for r in range(n_rb):
            rows = slice(r * RB, (r + 1) * RB)
            tb = jnp.broadcast_to(thr[rows], (RB, LANES))

            def body(j, cnt, rows=rows, tb=tb):
                for c in range(n_lane):
                    st = sc_ref[j, rows, c * LANES:(c + 1) * LANES]
                    cnt = cnt + jnp.where((st > tb) if strict else (st >= tb), 1.0, 0.0)
                return cnt

            cnt = lax.fori_loop(0, ntile, body, jnp.zeros((RB, LANES), F32))
            outs.append(jnp.sum(cnt, axis=1, keepdims=True))
        return jnp.concatenate(outs, axis=0)

    def sorted_count_ge(vals, tb):
        if not vals:
            return None
        mid = len(vals) // 2
        hit = vals[mid] >= tb
        base = jnp.where(hit, float(mid + 1), 0.0)
        rest = sorted_count_ge([jnp.where(hit, lo_v, hi_v) for hi_v, lo_v in zip(vals[:mid], vals[mid + 1:])], tb)
        return base if rest is None else base + rest

    def cand_count_ge(thr):
        assert (R + 1) & R == 0, "candidate lists must have 2**d - 1 entries"
        outs = []
        for r in range(n_rb):
            rows = slice(r * RB, (r + 1) * RB)
            tb = jnp.broadcast_to(thr[rows], (RB, LANES))
            cnt = jnp.zeros((RB, LANES), F32)
            for c in range(n_lane):
                cnt = cnt + sorted_count_ge([cand_ref[c * R + q, rows, :] for q in range(R)], tb)
            outs.append(jnp.sum(cnt, axis=1, keepdims=True))
        return jnp.concatenate(outs, axis=0)

    def cand_min_ge(thr):
        outs = []
        for r in range(n_rb):
            rows = slice(r * RB, (r + 1) * RB)
            tb = jnp.broadcast_to(thr[rows], (RB, LANES))
            cur = jnp.full((RB, LANES), jnp.inf, F32)
            for q in range(n_cand):
                ck = cand_ref[q, rows, :]
                cur = jnp.minimum(cur, jnp.where(ck >= tb, ck, jnp.inf))
            outs.append(jnp.min(cur, axis=1, keepdims=True))
        return jnp.concatenate(outs, axis=0)

    def n_open(lo, hi):
        return jnp.sum(jnp.where(lo < hi, 1.0, 0.0)).astype(jnp.int32)

    def search(count):
        def halve(lo, hi, exact):
            x = lo ^ hi
            mid = (lo & hi) + (x >> 1) + (x & 1)
            cnt = count(from_key(mid))
            ge, eq = cnt >= kf, cnt == kf
            lo = jnp.where(ge, mid, lo)
            hi = jnp.where(eq, mid, jnp.where(ge, hi, mid - 1))
            return lo, hi, jnp.where(eq, 1, exact)

        def bisect(state):
            lo, hi, exact = halve(*halve(*state[:3]))
            return lo, hi, exact, n_open(lo, hi)

        lo, _, exact, _ = lax.while_loop(lambda s: s[3] > 0, bisect,
                                         (lo0, hi0, jnp.where(few, 1, 0), n_open(lo0, hi0)))
        return from_key(lo), exact

    thr, exact = search(cand_count_ge)
    thr = jnp.where(few, thr, cand_min_ge(thr))
    thr_ref[0] = thr
    thr_ref[1] = exact.astype(F32)
    n_unsure = jnp.sum(jnp.where((last_kept >= thr) & jnp.logical_not(few), 1.0, 0.0)).astype(jnp.int32)

    @pl.when(n_unsure > 0)
    def _():
        thr_full, exact_full = search(count_ge)
        thr_ref[0] = thr_full
        thr_ref[1] = exact_full.astype(F32)

    thr = thr_ref[0]
    n_tied = jnp.sum(jnp.where(thr_ref[1] > 0.0, 0.0, 1.0)).astype(jnp.int32)

    @pl.when(n_tied == 0)
    def _():
        thr_b = jnp.broadcast_to(thr, (tq, LANES))

        def write_tile(j, carry):
            st = sc_ref[j]
            parts = [jnp.where(st[:, c * LANES:(c + 1) * LANES] >= thr_b, 0.0, MASK_NEG) for c in range(n_lane)]
            bias_ref[j] = jnp.concatenate(parts, axis=1).astype(bias_ref.dtype)
            return carry

        lax.fori_loop(0, ntile, write_tile, 0)

    @pl.when(n_tied > 0)
    def _():
        need = kf - count_ge(thr, strict=True)
        before = (lax.broadcasted_iota(jnp.int32, (tk, tk), 0) < lax.broadcasted_iota(jnp.int32, (tk, tk), 1))
        before = jnp.where(before, 1.0, 0.0).astype(BF16)

        def write_tile(j, run):
            st = sc_ref[j]
            tied = jnp.where(st == thr, 1.0, 0.0)
            rank = jnp.dot(tied.astype(BF16), before, preferred_element_type=F32) + run
            sel = (st > thr) | ((st == thr) & (rank < need))
            bias_ref[j] = jnp.where(sel, 0.0, MASK_NEG).astype(bias_ref.dtype)
            return run + jnp.sum(tied, axis=1, keepdims=True)

        lax.fori_loop(0, ntile, write_tile, jnp.zeros((tq, 1), F32))

    def fill_tile(j, carry):
        bias_ref[j] = jnp.full((tq, tk), MASK_NEG, bias_ref.dtype)
        return carry

    lax.fori_loop(ntile, nT, fill_tile, 0)


def _dsa_mask(p_big, small, ikt, B, T, tq, tk, iq_off, w_off, topk):
    G = IDX_HEADS
    idim = ikt.shape[2]
    nQ, nT = T // tq, T // tk
    kern = functools.partial(_idx_kernel, topk=topk, w_off=w_off, idim=idim)
    return pl.pallas_call(
        kern,
        out_shape=jax.ShapeDtypeStruct((B, nQ, nT, tq, tk), BF16),
        grid=(B, nQ),
        in_specs=[
            pl.BlockSpec((tq, G * idim), lambda b, i: (b * nQ + i, iq_off // (G * idim))),
            pl.BlockSpec((tq, LANES), lambda b, i: (b * nQ + i, 0)),
            pl.BlockSpec((None, nT, idim, tk), lambda b, i: (b, 0, 0, 0)),
        ],
        out_specs=pl.BlockSpec((None, None, nT, tq, tk), lambda b, i: (b, i, 0, 0, 0)),
        scratch_shapes=[pltpu.VMEM((nT, tq, tk), F32),
                        pltpu.VMEM((G, tq, idim), BF16),
                        pltpu.VMEM((G, tq, LANES), F32),
                        pltpu.VMEM((tk // LANES * CAND_PER_GROUP, tq, LANES), F32),
                        pltpu.VMEM((2, tq, 1), F32)],
        compiler_params=_cparams(("parallel", "arbitrary")),
        name="dsa_index_topk",
    )(p_big, small, ikt)


def _dsa_attn_kernel(qi_ref, kj_ref, q_ref, k_ref, v_ref, bias_ref, sl_ref, o_ref, m_ref, l_ref, acc_ref, *, heads):
    p = pl.program_id(1)
    qi, kj = qi_ref[p], kj_ref[p]
    tk = bias_ref.shape[2]
    tq = bias_ref.shape[0] * bias_ref.shape[1]
    d = q_ref.shape[1] // heads
    n_lane = tk // LANES

    @pl.when(kj == 0)
    def _():
        m_ref[...] = jnp.full_like(m_ref, MASK_NEG)
        l_ref[...] = jnp.zeros_like(l_ref)
        acc_ref[...] = jnp.zeros_like(acc_ref)

    rel = (kj * tk - qi * tq) + lax.broadcasted_iota(jnp.int32, (tk, LANES), 0)
    lane = lax.broadcasted_iota(jnp.int32, (tk, LANES), 1)
    pos = jnp.where(lane < 2 * ALIBI_SPLIT,
                    jnp.where(lane % 2 == 0, rel >> ALIBI_BASE_BITS, rel & (2 ** ALIBI_BASE_BITS - 1)), 0)
    pos = pos.astype(F32).astype(BF16)
    bias = bias_ref[...].astype(F32).reshape(tq, tk)
    ones_col = jnp.where(lane == 0, 1.0, 0.0).astype(BF16)

    def masked_logits(h):
        cols = slice(h * d, (h + 1) * d)
        qa = jnp.concatenate([q_ref[:, cols], sl_ref[h]], axis=1)
        ka = jnp.concatenate([k_ref[:, cols], pos], axis=1)
        s = lax.dot_general(qa, ka, (((1,), (1,)), ((), ())), preferred_element_type=F32)
        return [s[:, c * LANES:(c + 1) * LANES] + bias[:, c * LANES:(c + 1) * LANES] for c in range(n_lane)]

    sb_next = masked_logits(0)
    for h in range(heads):
        cols = slice(h * d, (h + 1) * d)
        sb = sb_next
        if h + 1 < heads:
            sb_next = masked_logits(h + 1)
        mx = sb[0]
        for c in range(1, n_lane):
            mx = jnp.maximum(mx, sb[c])
        m_old = m_ref[h]
        m_new = jnp.maximum(m_old, jnp.max(mx, axis=1, keepdims=True))
        corr = jnp.exp2(m_old - m_new)
        ps = [jnp.exp2((sb[c] - m_new).astype(BF16)) for c in range(n_lane)]
        va = jnp.concatenate([v_ref[:, cols], ones_col], axis=1)
        pv = jnp.dot(jnp.concatenate(ps, axis=1), va, preferred_element_type=F32)
        l_ref[h] = corr * l_ref[h] + pv[:, d:]
        acc_ref[:, cols] = corr * acc_ref[:, cols] + pv[:, :d]
        m_ref[h] = m_new

    @pl.when((kj + 1) * tk >= (qi + 1) * tq)
    def _():
        for h in range(heads):
            cols = slice(h * d, (h + 1) * d)
            l = jnp.sum(l_ref[h], axis=1, keepdims=True)
            o_ref[:, cols] = (acc_ref[:, cols] / l).astype(o_ref.dtype)


def _alibi_columns(heads, tq):
    out = np.zeros((heads, LANES), np.float32)
    for h in range(heads):
        rest = np.float64(2.0 ** (-8.0 * (h + 1) / heads) * np.log2(np.e))
        for i in range(ALIBI_SPLIT):
            piece = np.float64(np.float32(rest).astype(BF16))
            out[h, 2 * i], out[h, 2 * i + 1] = 2.0 ** ALIBI_BASE_BITS * piece, piece
            rest = rest - piece
    return jnp.asarray(np.broadcast_to(out.astype(BF16)[:, None, :], (heads, tq, LANES)))


def _dsa_attn(p_big, bias, B, T, tq_mask, tk, q_off, width):
    tq = tq_mask * ATTN_MASK_TILES
    assert T % tq == 0
    nQ, nT = T // tq, T // tk
    pairs = [(i, j) for i in range(nQ) for j in range(((i + 1) * tq + tk - 1) // tk)]
    qi_tbl = jnp.asarray(np.array([p[0] for p in pairs], np.int32))
    kj_tbl = jnp.asarray(np.array([p[1] for p in pairs], np.int32))
    qb = q_off // width
    heads = DSA_HEADS
    kern = functools.partial(_dsa_attn_kernel, heads=heads)
    grid_spec = pltpu.PrefetchScalarGridSpec(
        num_scalar_prefetch=2,
        grid=(B, len(pairs)),
        in_specs=[
            pl.BlockSpec((tq, width), lambda b, p, qi, kj: (b * nQ + qi[p], qb)),
            pl.BlockSpec((tk, width), lambda b, p, qi, kj: (b * nT + kj[p], qb + 1)),
            pl.BlockSpec((tk, width), lambda b, p, qi, kj: (b * nT + kj[p], qb + 2)),
            pl.BlockSpec((None, ATTN_MASK_TILES, None, tq_mask, tk), lambda b, p, qi, kj: (b, qi[p], kj[p], 0, 0)),
            pl.BlockSpec((heads, tq, LANES), lambda b, p, qi, kj: (0, 0, 0)),
        ],
        out_specs=pl.BlockSpec((tq, width), lambda b, p, qi, kj: (b * nQ + qi[p], 0)),
        scratch_shapes=[pltpu.VMEM((heads, tq, LANES), F32),
                        pltpu.VMEM((heads, tq, LANES), F32),
                        pltpu.VMEM((tq, width), F32)],
    )
    return pl.pallas_call(
        kern,
        out_shape=jax.ShapeDtypeStruct((B * T, width), BF16),
        grid_spec=grid_spec,
        compiler_params=_cparams(("arbitrary", "arbitrary")),
        name="dsa_attention",
    )(qi_tbl, kj_tbl, p_big, p_big, p_big, bias, _alibi_columns(heads, tq))


def _mem_attn_kernel(q_ref, k_ref, v_ref, o_ref):
    s = lax.dot_general(q_ref[...], k_ref[...], (((1,), (1,)), ((), ())), preferred_element_type=F32)
    m = jnp.max(s, axis=1, keepdims=True)
    p = jnp.exp(s - m)
    l = jnp.sum(p, axis=1, keepdims=True)
    o = jnp.dot(p.astype(BF16), v_ref[...], preferred_element_type=F32)
    o_ref[...] = (o / l).astype(o_ref.dtype)


def _mem_attn(p_big, kv, B, T, Mt, dm, q_off, tq):
    Hm = MEM_HEADS
    nq = T // tq
    qb = q_off // dm
    return pl.pallas_call(
        _mem_attn_kernel,
        out_shape=jax.ShapeDtypeStruct((B * T, Hm * dm), BF16),
        grid=(B, nq, Hm),
        in_specs=[
            pl.BlockSpec((tq, dm), lambda b, i, h: (b * nq + i, qb + h)),
            pl.BlockSpec((Mt, dm), lambda b, i, h: (b, h)),
            pl.BlockSpec((Mt, dm), lambda b, i, h: (b, Hm + h)),
        ],
        out_specs=pl.BlockSpec((tq, dm), lambda b, i, h: (b * nq + i, h)),
        compiler_params=_cparams(("parallel", "parallel", "arbitrary")),
        name="memory_attention",
    )(p_big, kv, kv)


def _merge_kernel(a_ref, d_ref, m_ref, wa_ref, wd_ref, wm_ref, g0_ref, g1_ref, g2_ref, b_ref, o_ref, *, tn):
    D = o_ref.shape[1]

    def products(cols):
        return (jnp.dot(a_ref[...], wa_ref[:, cols], preferred_element_type=F32),
                jnp.dot(d_ref[...], wd_ref[:, cols], preferred_element_type=F32),
                jnp.dot(m_ref[...], wm_ref[:, cols], preferred_element_type=F32))

    def gated_sum(cols, ys):
        acc = None
        for r, (g_ref, y) in enumerate(zip((g0_ref, g1_ref, g2_ref), ys)):
            bias = b_ref[:, r * D + cols.start:r * D + cols.stop]
            term = jax.nn.sigmoid(g_ref[:, cols].astype(F32) + bias) * y
            acc = term if acc is None else acc + term
        o_ref[:, cols] = acc.astype(o_ref.dtype)

    chunks = [slice(j * tn, (j + 1) * tn) for j in range(D // tn)]
    pending = None
    for cols in chunks:
        ys = products(cols)
        if pending is not None:
            gated_sum(*pending)
        pending = (cols, ys)
    gated_sum(*pending)


def _merge(o_gla, o_dsa, o_mem, wa, wd, wm, p_big, b_merge, D, gate_off, tm, tn):
    M = o_gla.shape[0]
    assert gate_off % D == 0 and D % tn == 0
    gb = gate_off // D
    row = lambda w: pl.BlockSpec((tm, w), lambda i: (i, 0))
    resident = lambda k: pl.BlockSpec((k, D), lambda i: (0, 0), pipeline_mode=pl.Buffered(1))
    gspec = lambda r: pl.BlockSpec((tm, D), lambda i: (i, gb + r))
    return pl.pallas_call(
        functools.partial(_merge_kernel, tn=tn),
        out_shape=jax.ShapeDtypeStruct((M, D), BF16),
        grid=(M // tm,),
        in_specs=[row(o_gla.shape[1]), row(o_dsa.shape[1]), row(o_mem.shape[1]),
                  resident(wa.shape[0]), resident(wd.shape[0]), resident(wm.shape[0]),
                  gspec(0), gspec(1), gspec(2),
                  pl.BlockSpec((1, N_BRANCH * D), lambda i: (0, 0))],
        out_specs=pl.BlockSpec((tm, D), lambda i: (i, 0)),
        compiler_params=_cparams(("parallel",)),
        name="gated_merge",
    )(o_gla, o_dsa, o_mem, wa, wd, wm, p_big, p_big, p_big, b_merge)


def _layer_norm(y, g, b):
    mu = jnp.mean(y, axis=-1, keepdims=True)
    yc = y - mu
    var = jnp.mean(yc * yc, axis=-1, keepdims=True)
    return yc * lax.rsqrt(var + LN_EPS) * g + b


def _proj_ln_kernel(mg_ref, wo_ref, x_ref, g_ref, b_ref, o_ref, *, alpha):
    tm = mg_ref.shape[0]
    halves = [slice(0, tm // 2), slice(tm // 2, tm)]
    ys = [jnp.dot(mg_ref[rows, :], wo_ref[...], preferred_element_type=F32) for rows in halves]
    for rows, y in zip(halves, ys):
        o_ref[rows, :] = _layer_norm(alpha * x_ref[rows, :] + y, g_ref[...], b_ref[...])


def _proj_ln(merged, w_o, x2, g, b, alpha, tm):
    M, D = x2.shape
    vec = pl.BlockSpec((1, D), lambda i: (0, 0))
    return pl.pallas_call(
        functools.partial(_proj_ln_kernel, alpha=alpha),
        out_shape=jax.ShapeDtypeStruct((M, D), F32),
        grid=(M // tm,),
        in_specs=[pl.BlockSpec((tm, D), lambda i: (i, 0)),
                  pl.BlockSpec((D, D), lambda i: (0, 0)),
                  pl.BlockSpec((tm, D), lambda i: (i, 0)), vec, vec],
        out_specs=pl.BlockSpec((tm, D), lambda i: (i, 0)),
        compiler_params=_cparams(("parallel",)),
        name="out_proj_ln",
    )(merged, w_o, x2, g, b)


def _mlp_kernel(x_ref, wu_ref, bu_ref, wd_ref, bd_ref, g_ref, b_ref, o_ref, xb_ref, acc_ref, *, alpha):
    f = pl.program_id(1)

    @pl.when(f == 0)
    def _():
        xb_ref[...] = x_ref[...].astype(BF16)
        acc_ref[...] = jnp.zeros_like(acc_ref)

    tf = wu_ref.shape[1]
    halves = [slice(0, tf // 2), slice(tf // 2, tf)]
    ups = [jnp.dot(xb_ref[...], wu_ref[:, cols], preferred_element_type=F32) for cols in halves]
    hs = [jnp.square(jnp.maximum(u + bu_ref[:, cols], 0.0)).astype(BF16) for u, cols in zip(ups, halves)]
    downs = [jnp.dot(h, wd_ref[cols, :], preferred_element_type=F32) for h, cols in zip(hs, halves)]
    acc_ref[...] += downs[0] + downs[1]

    @pl.when(f == pl.num_programs(1) - 1)
    def _():
        y = alpha * x_ref[...] + acc_ref[...] + bd_ref[...]
        o_ref[...] = _layer_norm(y, g_ref[...], b_ref[...])


def _mlp(x1, w_up, b_up, w_down, b_down, g, b, alpha, tm, tf):
    M, D = x1.shape
    F = w_up.shape[1]
    vec = pl.BlockSpec((1, D), lambda i, f: (0, 0))
    return pl.pallas_call(
        functools.partial(_mlp_kernel, alpha=alpha),
        out_shape=jax.ShapeDtypeStruct((M, D), F32),
        grid=(M // tm, F // tf),
        in_specs=[pl.BlockSpec((tm, D), lambda i, f: (i, 0)),
                  pl.BlockSpec((D, tf), lambda i, f: (0, f)),
                  pl.BlockSpec((1, tf), lambda i, f: (0, f)),
                  pl.BlockSpec((tf, D), lambda i, f: (f, 0)),
                  vec, vec, vec],
        out_specs=pl.BlockSpec((tm, D), lambda i, f: (i, 0)),
        scratch_shapes=[pltpu.VMEM((tm, D), BF16), pltpu.VMEM((tm, D), F32)],
        compiler_params=_cparams(("parallel", "arbitrary")),
        name="mlp_ln",
    )(x1, w_up, b_up, w_down, b_down, g, b)


def _tile(n, pref):
    t = min(n, pref)
    assert n % t == 0
    return t


def _layer(x2, mem2, B, T, w_in, w_gate_up, b_gate, norm_g, w_mem_kv, w_br_gla, w_br_dsa, w_br_mem,
           b_merge, w_o, ln1_g, ln1_b, w_up, b_up, w_down, b_down, ln2_g, ln2_b, alpha):
    M, D = x2.shape
    Mt = mem2.shape[0] // B
    dk, dv = D // 8, D // 4
    dh = D // 16
    idim = D // 32
    dm = D // 8
    H, G = GLA_HEADS, IDX_HEADS
    widths = [H * dk, H * dk, H * dv, H * dv, GLA_GATE_RANK, DSA_HEADS * dh, DSA_HEADS * dh, DSA_HEADS * dh,
              G * idim, idim, G, MEM_HEADS * dm, N_BRANCH * D]
    names = ['gla_q', 'gla_k', 'gla_v', 'gla_r', 'glr', 'dsa_q', 'dsa_k', 'dsa_v', 'idx_q', 'idx_k', 'idx_w',
             'mem_q', 'gate']
    offs = dict(zip(names, np.cumsum([0] + widths[:-1]).tolist()))
    wd = dict(zip(names, widths))
    col = lambda n, scale=None: (w_in[:, offs[n]:offs[n] + wd[n]] if scale is None
                                 else w_in[:, offs[n]:offs[n] + wd[n]] * scale)
    big_parts = [('gla_q', dk ** -0.5), ('gla_k', None), ('gla_v', None), ('gla_r', None), ('gate', None),
                 ('dsa_q', dh ** -0.5 * float(np.log2(np.e))), ('dsa_k', None), ('dsa_v', None), ('idx_q', None),
                 ('mem_q', dm ** -0.5)]
    w_big = jnp.concatenate([col(n, s) for n, s in big_parts], axis=1).astype(BF16)
    boff = dict(zip([n for n, _ in big_parts], np.cumsum([0] + [wd[n] for n, _ in big_parts][:-1]).tolist()))
    n_small = wd['glr'] + wd['idx_k'] + wd['idx_w']
    w_small = jnp.concatenate([col('glr'), col('idx_k'), col('idx_w'),
                               jnp.zeros((D, LANES - n_small), w_in.dtype)], axis=1).astype(BF16)

    p_big, small = _in_proj(x2, w_big, w_small, *TILES['in_proj'])

    o_gla = _gla(p_big, small, w_gate_up, b_gate.reshape(1, -1), norm_g.reshape(1, -1), B, T, dk, dv,
                 _tile(T, TILES['gla_rows']))

    tq, tk = _tile(T, TILES['dsa'][0]), _tile(T, TILES['dsa'][1])
    topk = min(DSA_TOPK_MAX, T // 4)
    ik = small[:, wd['glr']:wd['glr'] + idim].astype(BF16)
    ikt = ik.reshape(B, T // tk, tk, idim).transpose(0, 1, 3, 2)
    bias = _dsa_mask(p_big, small, ikt, B, T, tq, tk, boff['idx_q'], wd['glr'] + idim, topk)
    o_dsa = _dsa_attn(p_big, bias, B, T, tq, tk, boff['dsa_q'], DSA_HEADS * dh)

    kv = _matmul(mem2, w_mem_kv.astype(BF16), BF16, *TILES['mem_kv'], "mem_kv")
    o_mem = _mem_attn(p_big, kv, B, T, Mt, dm, boff['mem_q'], _tile(T, TILES['mem_attn_rows']))

    merged = _merge(o_gla, o_dsa, o_mem, w_br_gla.astype(BF16), w_br_dsa.astype(BF16), w_br_mem.astype(BF16),
                    p_big, b_merge.reshape(1, -1), D, boff['gate'], _tile(M, TILES['merge'][0]), TILES['merge'][1])
    x1 = _proj_ln(merged, w_o.astype(BF16), x2, ln1_g.reshape(1, -1), ln1_b.reshape(1, -1), alpha,
                  _tile(M, TILES['out_proj_rows']))
    return _mlp(x1, w_up.astype(BF16), b_up.reshape(1, -1), w_down.astype(BF16), b_down.reshape(1, -1),
                ln2_g.reshape(1, -1), ln2_b.reshape(1, -1), alpha, _tile(M, TILES['mlp'][0]), TILES['mlp'][1])


def kernel(x, mem, w_in, w_gla_gate_up, b_gla_gate, gla_norm_g, w_mem_kv, w_br_gla, w_br_dsa, w_br_mem,
           b_merge, w_o, ln1_g, ln1_b, w_up, b_up, w_down, b_down, ln2_g, ln2_b):
    B, T, D = x.shape
    depth = w_in.shape[0]
    alpha = (2 * depth) ** 0.25
    x2 = x.reshape(B * T, D)
    mem2 = mem.reshape(-1, D)
    for l in range(depth):
        x2 = _layer(x2, mem2, B, T, w_in[l], w_gla_gate_up[l], b_gla_gate[l], gla_norm_g[l], w_mem_kv[l],
                    w_br_gla[l], w_br_dsa[l], w_br_mem[l], b_merge[l], w_o[l], ln1_g[l], ln1_b[l],
                    w_up[l], b_up[l], w_down[l], b_down[l], ln2_g[l], ln2_b[l], alpha)
    return x2.reshape(B, T, D)
```

```python
import functools

import numpy as np
import jax
import jax.numpy as jnp
from jax import lax
from jax.experimental import pallas as pl
from jax.experimental.pallas import tpu as pltpu

F32 = jnp.float32
BF16 = jnp.bfloat16

GLA_HEADS = 4
GLA_GATE_RANK = 16
GLA_GATE_TEMP = 16.0
GLA_CHUNK = 64
DSA_HEADS = 8
IDX_HEADS = 16
DSA_TOPK_MAX = 256
MEM_HEADS = 4
N_BRANCH = 3
LN_EPS = 1e-5
RMS_EPS = 1e-6

V7X_VMEM_LIMIT_BYTES = 56 * 1024 * 1024
LANES = 128
MASK_NEG = -1e30
TILES = dict(
    in_proj=(1024, 1024),
    mem_kv=(512, 512),
    gla_rows=1024,
    dsa=(256, 512),
    mem_attn_rows=1024,
    merge=(512, 512),
    out_proj_rows=512,
    mlp=(512, 1024),
)
SCORE_SUB_ROWS, SCORE_SUB_COLS = 128, 256
COUNT_SUB_ROWS = 256
ALIBI_BASE_BITS = 8
ATTN_MASK_TILES = 2
CAND_PER_GROUP = 7
ALIBI_SPLIT = 3


def _cparams(sem):
    return pltpu.CompilerParams(dimension_semantics=sem, vmem_limit_bytes=V7X_VMEM_LIMIT_BYTES)


def _mm_cast_kernel(x_ref, w_ref, o_ref, xb_ref):
    @pl.when(pl.program_id(1) == 0)
    def _():
        xb_ref[...] = x_ref[...].astype(BF16)

    o_ref[...] = jnp.dot(xb_ref[...], w_ref[...], preferred_element_type=F32).astype(o_ref.dtype)


def _matmul(x, w, out_dtype, tm, tn, name):
    M, K = x.shape
    N = w.shape[1]
    tm, tn = min(tm, M), min(tn, N)
    assert M % tm == 0 and N % tn == 0
    return pl.pallas_call(
        _mm_cast_kernel,
        out_shape=jax.ShapeDtypeStruct((M, N), out_dtype),
        grid=(M // tm, N // tn),
        in_specs=[pl.BlockSpec((tm, K), lambda i, j: (i, 0)),
                  pl.BlockSpec((K, tn), lambda i, j: (0, j))],
        out_specs=pl.BlockSpec((tm, tn), lambda i, j: (i, j)),
        scratch_shapes=[pltpu.VMEM((tm, K), BF16)],
        compiler_params=_cparams(("parallel", "arbitrary")),
        name=name,
    )(x, w)


def _in_proj_kernel(x_ref, w_ref, ws_ref, o_ref, os_ref, xb_ref):
    @pl.when(pl.program_id(1) == 0)
    def _():
        xb_ref[...] = x_ref[...].astype(BF16)
        os_ref[...] = jnp.dot(xb_ref[...], ws_ref[...], preferred_element_type=F32)

    o_ref[...] = jnp.dot(xb_ref[...], w_ref[...], preferred_element_type=F32).astype(o_ref.dtype)


def _in_proj(x, w_big, w_small, tm, tn):
    M, K = x.shape
    N, Ns = w_big.shape[1], w_small.shape[1]
    tm, tn = min(tm, M), min(tn, N)
    assert M % tm == 0 and N % tn == 0
    return pl.pallas_call(
        _in_proj_kernel,
        out_shape=(jax.ShapeDtypeStruct((M, N), BF16), jax.ShapeDtypeStruct((M, Ns), F32)),
        grid=(M // tm, N // tn),
        in_specs=[pl.BlockSpec((tm, K), lambda i, j: (i, 0)),
                  pl.BlockSpec((K, tn), lambda i, j: (0, j)),
                  pl.BlockSpec((K, Ns), lambda i, j: (0, 0))],
        out_specs=(pl.BlockSpec((tm, tn), lambda i, j: (i, j)),
                   pl.BlockSpec((tm, Ns), lambda i, j: (i, 0))),
        scratch_shapes=[pltpu.VMEM((tm, K), BF16)],
        compiler_params=_cparams(("parallel", "arbitrary")),
        name="in_proj",
    )(x, w_big, w_small)


def _log_sigmoid(z):
    return jnp.minimum(z, 0.0) - jnp.log(1.0 + jnp.exp(-jnp.abs(z)))


def _gla_kernel(q_ref, k_ref, v_ref, r_ref, sm_ref, wg_ref, bg_ref, ng_ref, o_ref, st_ref, *, chunk, rank):
    @pl.when(pl.program_id(2) == 0)
    def _():
        st_ref[...] = jnp.zeros_like(st_ref)

    tb, dk = q_ref.shape
    C = chunk
    glr = sm_ref[:, 0:rank]
    z = jnp.dot(glr, wg_ref[...], preferred_element_type=F32) + bg_ref[...]
    la = _log_sigmoid(z) * (1.0 / GLA_GATE_TEMP)
    causal = (lax.broadcasted_iota(jnp.int32, (C, C), 1) <= lax.broadcasted_iota(jnp.int32, (C, C), 0))
    tri = jnp.where(causal, 1.0, 0.0).astype(F32)

    n_chunk = tb // C
    chunk_rows = [slice(c * C, (c + 1) * C) for c in range(n_chunk)]
    bcs = [jnp.dot(tri, la[rows], preferred_element_type=F32, precision=lax.Precision.HIGHEST)
           for rows in chunk_rows]
    scs, q_its, upds, decays = [], [], [], []
    for c, rows in enumerate(chunk_rows):
        bc = bcs[c]
        br = bc[C // 2:C // 2 + 1]
        bl = bc[C - 1:C]
        qc = q_ref[rows, :].astype(F32)
        kc = k_ref[rows, :].astype(F32)
        q_in = (qc * jnp.exp(bc - br)).astype(BF16)
        k_in = (kc * jnp.exp(br - bc)).astype(BF16)
        scs.append(lax.dot_general(q_in, k_in, (((1,), (1,)), ((), ())), preferred_element_type=F32))
        q_its.append((qc * jnp.exp(bc)).astype(BF16))
        k_st = (kc * jnp.exp(bl - bc)).astype(BF16)
        upds.append(lax.dot_general(v_ref[rows, :], k_st, (((0,), (0,)), ((), ())),
                                    preferred_element_type=F32))
        decays.append(jnp.exp(bl))
    o_intra = [jnp.dot(jnp.where(causal, scs[c], 0.0).astype(BF16), v_ref[rows, :], preferred_element_type=F32)
               for c, rows in enumerate(chunk_rows)]

    for c, rows in enumerate(chunk_rows):
        st = st_ref[...]
        o = o_intra[c] + lax.dot_general(q_its[c], st.astype(BF16), (((1,), (1,)), ((), ())),
                                         preferred_element_type=F32)
        st_ref[...] = st * decays[c] + upds[c]
        o = o * lax.rsqrt(jnp.mean(o * o, axis=-1, keepdims=True) + RMS_EPS)
        r = r_ref[rows, :].astype(F32)
        o = o * ng_ref[...] * (r / (1.0 + jnp.exp(-r)))
        o_ref[rows, :] = o.astype(o_ref.dtype)


def _gla(p_big, small, w_gate_up, b_gate, norm_g, B, T, dk, dv, tb):
    M = B * T
    H = GLA_HEADS
    nb = T // tb
    row = lambda b, h, n: b * nb + n
    kern = functools.partial(_gla_kernel, chunk=GLA_CHUNK, rank=GLA_GATE_RANK)
    koff = (H * dk) // dk
    voff = (2 * H * dk) // dv
    roff = (2 * H * dk + H * dv) // dv
    return pl.pallas_call(
        kern,
        out_shape=jax.ShapeDtypeStruct((M, H * dv), BF16),
        grid=(B, H, nb),
        in_specs=[
            pl.BlockSpec((tb, dk), lambda b, h, n: (row(b, h, n), h)),
            pl.BlockSpec((tb, dk), lambda b, h, n: (row(b, h, n), koff + h)),
            pl.BlockSpec((tb, dv), lambda b, h, n: (row(b, h, n), voff + h)),
            pl.BlockSpec((tb, dv), lambda b, h, n: (row(b, h, n), roff + h)),
            pl.BlockSpec((tb, LANES), lambda b, h, n: (row(b, h, n), 0)),
            pl.BlockSpec((GLA_GATE_RANK, dk), lambda b, h, n: (0, h)),
            pl.BlockSpec((1, dk), lambda b, h, n: (0, h)),
            pl.BlockSpec((1, dv), lambda b, h, n: (0, h)),
        ],
        out_specs=pl.BlockSpec((tb, dv), lambda b, h, n: (row(b, h, n), h)),
        scratch_shapes=[pltpu.VMEM((dv, dk), F32)],
        compiler_params=_cparams(("parallel", "parallel", "arbitrary")),
        name="gla",
    )(p_big, p_big, p_big, p_big, small, w_gate_up, b_gate, norm_g)


def _idx_kernel(iq_ref, sm_ref, ikt_ref, bias_ref, sc_ref, iqs_ref, wb_ref, cand_ref, thr_ref, *, topk, w_off, idim):
    i = pl.program_id(1)
    nT, tq, tk = sc_ref.shape
    G = iqs_ref.shape[0]
    n_lane = tk // LANES
    n_cand = cand_ref.shape[0]
    R = n_cand // n_lane
    ntile = ((i + 1) * tq + tk - 1) // tk
    kf = float(topk)
    RB = min(tq, COUNT_SUB_ROWS)
    n_rb = tq // RB

    w = sm_ref[:, w_off:w_off + G] * ((G ** -0.5) * (idim ** -0.5))
    for g in range(G):
        iqs_ref[g] = iq_ref[:, g * idim:(g + 1) * idim]
        wb_ref[g] = jnp.broadcast_to(w[:, g:g + 1], (tq, LANES))
    row_pos = i * tq + lax.broadcasted_iota(jnp.int32, (tq, 1), 0)
    cand_ref[...] = jnp.full(cand_ref.shape, -jnp.inf, F32)

    sub_rows, sub_cols = min(tq, SCORE_SUB_ROWS), min(tk, SCORE_SUB_COLS)

    def score_tile(j, carry):
        col_pos = j * tk + lax.broadcasted_iota(jnp.int32, (1, tk), 1)
        for rh in range(tq // sub_rows):
            rows = slice(rh * sub_rows, (rh + 1) * sub_rows)
            for lh in range(tk // sub_cols):
                cols = slice(lh * sub_cols, (lh + 1) * sub_cols)
                ik = ikt_ref[j, :, cols]
                acc = jnp.zeros((sub_rows, sub_cols), F32)
                for g in range(G):
                    a = jnp.dot(iqs_ref[g, rows, :], ik, preferred_element_type=F32)
                    wg = jnp.concatenate([wb_ref[g, rows, :]] * (sub_cols // LANES), axis=1)
                    acc = acc + wg * jnp.maximum(a, 0.0)
                sc = jnp.where(col_pos[:, cols] <= row_pos[rows], acc, -jnp.inf)
                sc_ref[j, rows, cols] = sc
                for cc in range(sub_cols // LANES):
                    c = lh * (sub_cols // LANES) + cc
                    x = sc[:, cc * LANES:(cc + 1) * LANES]
                    for r in range(R):
                        cur = cand_ref[c * R + r, rows, :]
                        cand_ref[c * R + r, rows, :] = jnp.maximum(cur, x)
                        x = jnp.minimum(cur, x)
        return carry

    lax.fori_loop(0, ntile, score_tile, 0)

    def to_key(v):
        bits = pltpu.bitcast(v, jnp.int32)
        return bits ^ ((bits >> 31) & 0x7FFFFFFF)

    def from_key(k):
        return pltpu.bitcast(k ^ ((k >> 31) & 0x7FFFFFFF), F32)

    gmax, last_kept = cand_ref[0], cand_ref[R - 1]
    for c in range(1, n_lane):
        gmax = jnp.maximum(gmax, cand_ref[c * R])
        last_kept = jnp.maximum(last_kept, cand_ref[c * R + R - 1])
    hi0 = to_key(jnp.max(gmax, axis=1, keepdims=True))
    last_kept = jnp.max(last_kept, axis=1, keepdims=True)
    n_cover = -(-topk // LANES)
    lo_f = None
    for r in range(min(R, n_cover)):
        need = -(-n_cover // (r + 1))
        if need > n_lane:
            continue
        mins = [jnp.min(cand_ref[c * R + r], axis=1, keepdims=True) for c in range(n_lane)]
        for a in range(need):
            for b2 in range(n_lane - 1, a, -1):
                hi_v, lo_v = jnp.maximum(mins[b2 - 1], mins[b2]), jnp.minimum(mins[b2 - 1], mins[b2])
                mins[b2 - 1], mins[b2] = hi_v, lo_v
        lo_f = mins[need - 1] if lo_f is None else jnp.maximum(lo_f, mins[need - 1])
    assert lo_f is not None, "topk must not exceed the number of key groups"
    lo0 = to_key(lo_f)
    few = row_pos < topk
    all_finite = to_key(jnp.full((tq, 1), -jnp.inf, F32)) + 1
    lo0 = jnp.where(few, all_finite, lo0)
    hi0 = jnp.where(few, all_finite, hi0)

    def count_ge(thr, strict=False):
        outs = []
        for r in range(n_rb):
            rows = slice(r * RB, (r + 1) * RB)
            tb = jnp.broadcast_to(thr[rows], (RB, LANES))

            def body(j, cnt, rows=rows, tb=tb):
                for c in range(n_lane):
                    st = sc_ref[j, rows, c * LANES:(c + 1) * LANES]
                    cnt = cnt + jnp.where((st > tb) if strict else (st >= tb), 1.0, 0.0)
                return cnt

            cnt = lax.fori_loop(0, ntile, body, jnp.zeros((RB, LANES), F32))
            outs.append(jnp.sum(cnt, axis=1, keepdims=True))
        return jnp.concatenate(outs, axis=0)

    def sorted_count_ge(vals, tb):
        if not vals:
            return None
        mid = len(vals) // 2
        hit = vals[mid] >= tb
        base = jnp.where(hit, float(mid + 1), 0.0)
        rest = sorted_count_ge([jnp.where(hit, lo_v, hi_v) for hi_v, lo_v in zip(vals[:mid], vals[mid + 1:])], tb)
        return base if rest is None else base + rest

    def cand_count_ge(thr):
        assert (R + 1) & R == 0, "candidate lists must have 2**d - 1 entries"
        outs = []
        for r in range(n_rb):
            rows = slice(r * RB, (r + 1) * RB)
            tb = jnp.broadcast_to(thr[rows], (RB, LANES))
            cnt = jnp.zeros((RB, LANES), F32)
            for c in range(n_lane):
                cnt = cnt + sorted_count_ge([cand_ref[c * R + q, rows, :] for q in range(R)], tb)
            outs.append(jnp.sum(cnt, axis=1, keepdims=True))
        return jnp.concatenate(outs, axis=0)

    def cand_min_ge(thr):
        outs = []
        for r in range(n_rb):
            rows = slice(r * RB, (r + 1) * RB)
            tb = jnp.broadcast_to(thr[rows], (RB, LANES))
            cur = jnp.full((RB, LANES), jnp.inf, F32)
            for q in range(n_cand):
                ck = cand_ref[q, rows, :]
                cur = jnp.minimum(cur, jnp.where(ck >= tb, ck, jnp.inf))
            outs.append(jnp.min(cur, axis=1, keepdims=True))
        return jnp.concatenate(outs, axis=0)

    def n_open(lo, hi):
        return jnp.sum(jnp.where(lo < hi, 1.0, 0.0)).astype(jnp.int32)

    def search(count):
        def halve(lo, hi, exact):
            x = lo ^ hi
            mid = (lo & hi) + (x >> 1) + (x & 1)
            cnt = count(from_key(mid))
            ge, eq = cnt >= kf, cnt == kf
            lo = jnp.where(ge, mid, lo)
            hi = jnp.where(eq, mid, jnp.where(ge, hi, mid - 1))
            return lo, hi, jnp.where(eq, 1, exact)

        def bisect(state):
            lo, hi, exact = halve(*halve(*state[:3]))
            return lo, hi, exact, n_open(lo, hi)

        lo, _, exact, _ = lax.while_loop(lambda s: s[3] > 0, bisect,
                                         (lo0, hi0, jnp.where(few, 1, 0), n_open(lo0, hi0)))
        return from_key(lo), exact

    thr, exact = search(cand_count_ge)
    thr = jnp.where(few, thr, cand_min_ge(thr))
    thr_ref[0] = thr
    thr_ref[1] = exact.astype(F32)
    n_unsure = jnp.sum(jnp.where((last_kept >= thr) & jnp.logical_not(few), 1.0, 0.0)).astype(jnp.int32)

    @pl.when(n_unsure > 0)
    def _():
        thr_full, exact_full = search(count_ge)
        thr_ref[0] = thr_full
        thr_ref[1] = exact_full.astype(F32)

    thr = thr_ref[0]
    n_tied = jnp.sum(jnp.where(thr_ref[1] > 0.0, 0.0, 1.0)).astype(jnp.int32)

    @pl.when(n_tied == 0)
    def _():
        thr_b = jnp.broadcast_to(thr, (tq, LANES))

        def write_tile(j, carry):
            st = sc_ref[j]
            parts = [jnp.where(st[:, c * LANES:(c + 1) * LANES] >= thr_b, 0.0, MASK_NEG) for c in range(n_lane)]
            bias_ref[j] = jnp.concatenate(parts, axis=1).astype(bias_ref.dtype)
            return carry

        lax.fori_loop(0, ntile, write_tile, 0)

    @pl.when(n_tied > 0)
    def _():
        need = kf - count_ge(thr, strict=True)
        before = (lax.broadcasted_iota(jnp.int32, (tk, tk), 0) < lax.broadcasted_iota(jnp.int32, (tk, tk), 1))
        before = jnp.where(before, 1.0, 0.0).astype(BF16)

        def write_tile(j, run):
            st = sc_ref[j]
            tied = jnp.where(st == thr, 1.0, 0.0)
            rank = jnp.dot(tied.astype(BF16), before, preferred_element_type=F32) + run
            sel = (st > thr) | ((st == thr) & (rank < need))
            bias_ref[j] = jnp.where(sel, 0.0, MASK_NEG).astype(bias_ref.dtype)
            return run + jnp.sum(tied, axis=1, keepdims=True)

        lax.fori_loop(0, ntile, write_tile, jnp.zeros((tq, 1), F32))

    def fill_tile(j, carry):
        bias_ref[j] = jnp.full((tq, tk), MASK_NEG, bias_ref.dtype)
        return carry

    lax.fori_loop(ntile, nT, fill_tile, 0)


def _dsa_mask(p_big, small, ikt, B, T, tq, tk, iq_off, w_off, topk):
    G = IDX_HEADS
    idim = ikt.shape[2]
    nQ, nT = T // tq, T // tk
    kern = functools.partial(_idx_kernel, topk=topk, w_off=w_off, idim=idim)
    return pl.pallas_call(
        kern,
        out_shape=jax.ShapeDtypeStruct((B, nQ, nT, tq, tk), BF16),
        grid=(B, nQ),
        in_specs=[
            pl.BlockSpec((tq, G * idim), lambda b, i: (b * nQ + i, iq_off // (G * idim))),
            pl.BlockSpec((tq, LANES), lambda b, i: (b * nQ + i, 0)),
            pl.BlockSpec((None, nT, idim, tk), lambda b, i: (b, 0, 0, 0)),
        ],
        out_specs=pl.BlockSpec((None, None, nT, tq, tk), lambda b, i: (b, i, 0, 0, 0)),
        scratch_shapes=[pltpu.VMEM((nT, tq, tk), F32),
                        pltpu.VMEM((G, tq, idim), BF16),
                        pltpu.VMEM((G, tq, LANES), F32),
                        pltpu.VMEM((tk // LANES * CAND_PER_GROUP, tq, LANES), F32),
                        pltpu.VMEM((2, tq, 1), F32)],
        compiler_params=_cparams(("parallel", "arbitrary")),
        name="dsa_index_topk",
    )(p_big, small, ikt)


def _dsa_attn_kernel(qi_ref, kj_ref, q_ref, k_ref, v_ref, bias_ref, sl_ref, o_ref, m_ref, l_ref, acc_ref, *, heads):
    p = pl.program_id(1)
    qi, kj = qi_ref[p], kj_ref[p]
    tk = bias_ref.shape[2]
    tq = bias_ref.shape[0] * bias_ref.shape[1]
    d = q_ref.shape[1] // heads
    n_lane = tk // LANES

    @pl.when(kj == 0)
    def _():
        m_ref[...] = jnp.full_like(m_ref, MASK_NEG)
        l_ref[...] = jnp.zeros_like(l_ref)
        acc_ref[...] = jnp.zeros_like(acc_ref)

    rel = (kj * tk - qi * tq) + lax.broadcasted_iota(jnp.int32, (tk, LANES), 0)
    lane = lax.broadcasted_iota(jnp.int32, (tk, LANES), 1)
    pos = jnp.where(lane < 2 * ALIBI_SPLIT,
                    jnp.where(lane % 2 == 0, rel >> ALIBI_BASE_BITS, rel & (2 ** ALIBI_BASE_BITS - 1)), 0)
    pos = pos.astype(F32).astype(BF16)
    bias = bias_ref[...].astype(F32).reshape(tq, tk)
    ones_col = jnp.where(lane == 0, 1.0, 0.0).astype(BF16)

    def masked_logits(h):
        cols = slice(h * d, (h + 1) * d)
        qa = jnp.concatenate([q_ref[:, cols], sl_ref[h]], axis=1)
        ka = jnp.concatenate([k_ref[:, cols], pos], axis=1)
        s = lax.dot_general(qa, ka, (((1,), (1,)), ((), ())), preferred_element_type=F32)
        return [s[:, c * LANES:(c + 1) * LANES] + bias[:, c * LANES:(c + 1) * LANES] for c in range(n_lane)]

    sb_next = masked_logits(0)
    for h in range(heads):
        cols = slice(h * d, (h + 1) * d)
        sb = sb_next
        if h + 1 < heads:
            sb_next = masked_logits(h + 1)
        mx = sb[0]
        for c in range(1, n_lane):
            mx = jnp.maximum(mx, sb[c])
        m_old = m_ref[h]
        m_new = jnp.maximum(m_old, jnp.max(mx, axis=1, keepdims=True))
        corr = jnp.exp2(m_old - m_new)
        ps = [jnp.exp2((sb[c] - m_new).astype(BF16)) for c in range(n_lane)]
        va = jnp.concatenate([v_ref[:, cols], ones_col], axis=1)
        pv = jnp.dot(jnp.concatenate(ps, axis=1), va, preferred_element_type=F32)
        l_ref[h] = corr * l_ref[h] + pv[:, d:]
        acc_ref[:, cols] = corr * acc_ref[:, cols] + pv[:, :d]
        m_ref[h] = m_new

    @pl.when((kj + 1) * tk >= (qi + 1) * tq)
    def _():
        for h in range(heads):
            cols = slice(h * d, (h + 1) * d)
            l = jnp.sum(l_ref[h], axis=1, keepdims=True)
            o_ref[:, cols] = (acc_ref[:, cols] / l).astype(o_ref.dtype)


def _alibi_columns(heads, tq):
    out = np.zeros((heads, LANES), np.float32)
    for h in range(heads):
        rest = np.float64(2.0 ** (-8.0 * (h + 1) / heads) * np.log2(np.e))
        for i in range(ALIBI_SPLIT):
            piece = np.float64(np.float32(rest).astype(BF16))
            out[h, 2 * i], out[h, 2 * i + 1] = 2.0 ** ALIBI_BASE_BITS * piece, piece
            rest = rest - piece
    return jnp.asarray(np.broadcast_to(out.astype(BF16)[:, None, :], (heads, tq, LANES)))


def _dsa_attn(p_big, bias, B, T, tq_mask, tk, q_off, width):
    tq = tq_mask * ATTN_MASK_TILES
    assert T % tq == 0
    nQ, nT = T // tq, T // tk
    pairs = [(i, j) for i in range(nQ) for j in range(((i + 1) * tq + tk - 1) // tk)]
    qi_tbl = jnp.asarray(np.array([p[0] for p in pairs], np.int32))
    kj_tbl = jnp.asarray(np.array([p[1] for p in pairs], np.int32))
    qb = q_off // width
    heads = DSA_HEADS
    kern = functools.partial(_dsa_attn_kernel, heads=heads)
    grid_spec = pltpu.PrefetchScalarGridSpec(
        num_scalar_prefetch=2,
        grid=(B, len(pairs)),
        in_specs=[
            pl.BlockSpec((tq, width), lambda b, p, qi, kj: (b * nQ + qi[p], qb)),
            pl.BlockSpec((tk, width), lambda b, p, qi, kj: (b * nT + kj[p], qb + 1)),
            pl.BlockSpec((tk, width), lambda b, p, qi, kj: (b * nT + kj[p], qb + 2)),
            pl.BlockSpec((None, ATTN_MASK_TILES, None, tq_mask, tk), lambda b, p, qi, kj: (b, qi[p], kj[p], 0, 0)),
            pl.BlockSpec((heads, tq, LANES), lambda b, p, qi, kj: (0, 0, 0)),
        ],
        out_specs=pl.BlockSpec((tq, width), lambda b, p, qi, kj: (b * nQ + qi[p], 0)),
        scratch_shapes=[pltpu.VMEM((heads, tq, LANES), F32),
                        pltpu.VMEM((heads, tq, LANES), F32),
                        pltpu.VMEM((tq, width), F32)],
    )
    return pl.pallas_call(
        kern,
        out_shape=jax.ShapeDtypeStruct((B * T, width), BF16),
        grid_spec=grid_spec,
        compiler_params=_cparams(("arbitrary", "arbitrary")),
        name="dsa_attention",
    )(qi_tbl, kj_tbl, p_big, p_big, p_big, bias, _alibi_columns(heads, tq))


def _mem_attn_kernel(q_ref, k_ref, v_ref, o_ref):
    s = lax.dot_general(q_ref[...], k_ref[...], (((1,), (1,)), ((), ())), preferred_element_type=F32)
    m = jnp.max(s, axis=1, keepdims=True)
    p = jnp.exp(s - m)
    l = jnp.sum(p, axis=1, keepdims=True)
    o = jnp.dot(p.astype(BF16), v_ref[...], preferred_element_type=F32)
    o_ref[...] = (o / l).astype(o_ref.dtype)


def _mem_attn(p_big, kv, B, T, Mt, dm, q_off, tq):
    Hm = MEM_HEADS
    nq = T // tq
    qb = q_off // dm
    return pl.pallas_call(
        _mem_attn_kernel,
        out_shape=jax.ShapeDtypeStruct((B * T, Hm * dm), BF16),
        grid=(B, nq, Hm),
        in_specs=[
            pl.BlockSpec((tq, dm), lambda b, i, h: (b * nq + i, qb + h)),
            pl.BlockSpec((Mt, dm), lambda b, i, h: (b, h)),
            pl.BlockSpec((Mt, dm), lambda b, i, h: (b, Hm + h)),
        ],
        out_specs=pl.BlockSpec((tq, dm), lambda b, i, h: (b * nq + i, h)),
        compiler_params=_cparams(("parallel", "parallel", "arbitrary")),
        name="memory_attention",
    )(p_big, kv, kv)


def _merge_kernel(a_ref, d_ref, m_ref, wa_ref, wd_ref, wm_ref, g0_ref, g1_ref, g2_ref, b_ref, o_ref, *, tn):
    D = o_ref.shape[1]

    def products(cols):
        return (jnp.dot(a_ref[...], wa_ref[:, cols], preferred_element_type=F32),
                jnp.dot(d_ref[...], wd_ref[:, cols], preferred_element_type=F32),
                jnp.dot(m_ref[...], wm_ref[:, cols], preferred_element_type=F32))

    def gated_sum(cols, ys):
        acc = None
        for r, (g_ref, y) in enumerate(zip((g0_ref, g1_ref, g2_ref), ys)):
            bias = b_ref[:, r * D + cols.start:r * D + cols.stop]
            term = jax.nn.sigmoid(g_ref[:, cols].astype(F32) + bias) * y
            acc = term if acc is None else acc + term
        o_ref[:, cols] = acc.astype(o_ref.dtype)

    chunks = [slice(j * tn, (j + 1) * tn) for j in range(D // tn)]
    pending = None
    for cols in chunks:
        ys = products(cols)
        if pending is not None:
            gated_sum(*pending)
        pending = (cols, ys)
    gated_sum(*pending)


def _merge(o_gla, o_dsa, o_mem, wa, wd, wm, p_big, b_merge, D, gate_off, tm, tn):
    M = o_gla.shape[0]
    assert gate_off % D == 0 and D % tn == 0
    gb = gate_off // D
    row = lambda w: pl.BlockSpec((tm, w), lambda i: (i, 0))
    resident = lambda k: pl.BlockSpec((k, D), lambda i: (0, 0), pipeline_mode=pl.Buffered(1))
    gspec = lambda r: pl.BlockSpec((tm, D), lambda i: (i, gb + r))
    return pl.pallas_call(
        functools.partial(_merge_kernel, tn=tn),
        out_shape=jax.ShapeDtypeStruct((M, D), BF16),
        grid=(M // tm,),
        in_specs=[row(o_gla.shape[1]), row(o_dsa.shape[1]), row(o_mem.shape[1]),
                  resident(wa.shape[0]), resident(wd.shape[0]), resident(wm.shape[0]),
                  gspec(0), gspec(1), gspec(2),
                  pl.BlockSpec((1, N_BRANCH * D), lambda i: (0, 0))],
        out_specs=pl.BlockSpec((tm, D), lambda i: (i, 0)),
        compiler_params=_cparams(("parallel",)),
        name="gated_merge",
    )(o_gla, o_dsa, o_mem, wa, wd, wm, p_big, p_big, p_big, b_merge)


def _layer_norm(y, g, b):
    mu = jnp.mean(y, axis=-1, keepdims=True)
    yc = y - mu
    var = jnp.mean(yc * yc, axis=-1, keepdims=True)
    return yc * lax.rsqrt(var + LN_EPS) * g + b


def _proj_ln_kernel(mg_ref, wo_ref, x_ref, g_ref, b_ref, o_ref, *, alpha):
    tm = mg_ref.shape[0]
    halves = [slice(0, tm // 2), slice(tm // 2, tm)]
    ys = [jnp.dot(mg_ref[rows, :], wo_ref[...], preferred_element_type=F32) for rows in halves]
    for rows, y in zip(halves, ys):
        o_ref[rows, :] = _layer_norm(alpha * x_ref[rows, :] + y, g_ref[...], b_ref[...])


def _proj_ln(merged, w_o, x2, g, b, alpha, tm):
    M, D = x2.shape
    vec = pl.BlockSpec((1, D), lambda i: (0, 0))
    return pl.pallas_call(
        functools.partial(_proj_ln_kernel, alpha=alpha),
        out_shape=jax.ShapeDtypeStruct((M, D), F32),
        grid=(M // tm,),
        in_specs=[pl.BlockSpec((tm, D), lambda i: (i, 0)),
                  pl.BlockSpec((D, D), lambda i: (0, 0)),
                  pl.BlockSpec((tm, D), lambda i: (i, 0)), vec, vec],
        out_specs=pl.BlockSpec((tm, D), lambda i: (i, 0)),
        compiler_params=_cparams(("parallel",)),
        name="out_proj_ln",
    )(merged, w_o, x2, g, b)


def _mlp_kernel(x_ref, wu_ref, bu_ref, wd_ref, bd_ref, g_ref, b_ref, o_ref, xb_ref, acc_ref, *, alpha):
    f = pl.program_id(1)

    @pl.when(f == 0)
    def _():
        xb_ref[...] = x_ref[...].astype(BF16)
        acc_ref[...] = jnp.zeros_like(acc_ref)

    tf = wu_ref.shape[1]
    halves = [slice(0, tf // 2), slice(tf // 2, tf)]
    ups = [jnp.dot(xb_ref[...], wu_ref[:, cols], preferred_element_type=F32) for cols in halves]
    hs = [jnp.square(jnp.maximum(u + bu_ref[:, cols], 0.0)).astype(BF16) for u, cols in zip(ups, halves)]
    downs = [jnp.dot(h, wd_ref[cols, :], preferred_element_type=F32) for h, cols in zip(hs, halves)]
    acc_ref[...] += downs[0] + downs[1]

    @pl.when(f == pl.num_programs(1) - 1)
    def _():
        y = alpha * x_ref[...] + acc_ref[...] + bd_ref[...]
        o_ref[...] = _layer_norm(y, g_ref[...], b_ref[...])


def _mlp(x1, w_up, b_up, w_down, b_down, g, b, alpha, tm, tf):
    M, D = x1.shape
    F = w_up.shape[1]
    vec = pl.BlockSpec((1, D), lambda i, f: (0, 0))
    return pl.pallas_call(
        functools.partial(_mlp_kernel, alpha=alpha),
        out_shape=jax.ShapeDtypeStruct((M, D), F32),
        grid=(M // tm, F // tf),
        in_specs=[pl.BlockSpec((tm, D), lambda i, f: (i, 0)),
                  pl.BlockSpec((D, tf), lambda i, f: (0, f)),
                  pl.BlockSpec((1, tf), lambda i, f: (0, f)),
                  pl.BlockSpec((tf, D), lambda i, f: (f, 0)),
                  vec, vec, vec],
        out_specs=pl.BlockSpec((tm, D), lambda i, f: (i, 0)),
        scratch_shapes=[pltpu.VMEM((tm, D), BF16), pltpu.VMEM((tm, D), F32)],
        compiler_params=_cparams(("parallel", "arbitrary")),
        name="mlp_ln",
    )(x1, w_up, b_up, w_down, b_down, g, b)


def _tile(n, pref):
    t = min(n, pref)
    assert n % t == 0
    return t


def _layer(x2, mem2, B, T, w_in, w_gate_up, b_gate, norm_g, w_mem_kv, w_br_gla, w_br_dsa, w_br_mem,
           b_merge, w_o, ln1_g, ln1_b, w_up, b_up, w_down, b_down, ln2_g, ln2_b, alpha):
    M, D = x2.shape
    Mt = mem2.shape[0] // B
    dk, dv = D // 8, D // 4
    dh = D // 16
    idim = D // 32
    dm = D // 8
    H, G = GLA_HEADS, IDX_HEADS
    widths = [H * dk, H * dk, H * dv, H * dv, GLA_GATE_RANK, DSA_HEADS * dh, DSA_HEADS * dh, DSA_HEADS * dh,
              G * idim, idim, G, MEM_HEADS * dm, N_BRANCH * D]
    names = ['gla_q', 'gla_k', 'gla_v', 'gla_r', 'glr', 'dsa_q', 'dsa_k', 'dsa_v', 'idx_q', 'idx_k', 'idx_w',
             'mem_q', 'gate']
    offs = dict(zip(names, np.cumsum([0] + widths[:-1]).tolist()))
    wd = dict(zip(names, widths))
    col = lambda n, scale=None: (w_in[:, offs[n]:offs[n] + wd[n]] if scale is None
                                 else w_in[:, offs[n]:offs[n] + wd[n]] * scale)
    big_parts = [('gla_q', dk ** -0.5), ('gla_k', None), ('gla_v', None), ('gla_r', None), ('gate', None),
                 ('dsa_q', dh ** -0.5 * float(np.log2(np.e))), ('dsa_k', None), ('dsa_v', None), ('idx_q', None),
                 ('mem_q', dm ** -0.5)]
    w_big = jnp.concatenate([col(n, s) for n, s in big_parts], axis=1).astype(BF16)
    boff = dict(zip([n for n, _ in big_parts], np.cumsum([0] + [wd[n] for n, _ in big_parts][:-1]).tolist()))
    n_small = wd['glr'] + wd['idx_k'] + wd['idx_w']
    w_small = jnp.concatenate([col('glr'), col('idx_k'), col('idx_w'),
                               jnp.zeros((D, LANES - n_small), w_in.dtype)], axis=1).astype(BF16)

    p_big, small = _in_proj(x2, w_big, w_small, *TILES['in_proj'])

    o_gla = _gla(p_big, small, w_gate_up, b_gate.reshape(1, -1), norm_g.reshape(1, -1), B, T, dk, dv,
                 _tile(T, TILES['gla_rows']))

    tq, tk = _tile(T, TILES['dsa'][0]), _tile(T, TILES['dsa'][1])
    topk = min(DSA_TOPK_MAX, T // 4)
    ik = small[:, wd['glr']:wd['glr'] + idim].astype(BF16)
    ikt = ik.reshape(B, T // tk, tk, idim).transpose(0, 1, 3, 2)
    bias = _dsa_mask(p_big, small, ikt, B, T, tq, tk, boff['idx_q'], wd['glr'] + idim, topk)
    o_dsa = _dsa_attn(p_big, bias, B, T, tq, tk, boff['dsa_q'], DSA_HEADS * dh)

    kv = _matmul(mem2, w_mem_kv.astype(BF16), BF16, *TILES['mem_kv'], "mem_kv")
    o_mem = _mem_attn(p_big, kv, B, T, Mt, dm, boff['mem_q'], _tile(T, TILES['mem_attn_rows']))

    merged = _merge(o_gla, o_dsa, o_mem, w_br_gla.astype(BF16), w_br_dsa.astype(BF16), w_br_mem.astype(BF16),
                    p_big, b_merge.reshape(1, -1), D, boff['gate'], _tile(M, TILES['merge'][0]), TILES['merge'][1])
    x1 = _proj_ln(merged, w_o.astype(BF16), x2, ln1_g.reshape(1, -1), ln1_b.reshape(1, -1), alpha,
                  _tile(M, TILES['out_proj_rows']))
    return _mlp(x1, w_up.astype(BF16), b_up.reshape(1, -1), w_down.astype(BF16), b_down.reshape(1, -1),
                ln2_g.reshape(1, -1), ln2_b.reshape(1, -1), alpha, _tile(M, TILES['mlp'][0]), TILES['mlp'][1])


def kernel(x, mem, w_in, w_gla_gate_up, b_gla_gate, gla_norm_g, w_mem_kv, w_br_gla, w_br_dsa, w_br_mem,
           b_merge, w_o, ln1_g, ln1_b, w_up, b_up, w_down, b_down, ln2_g, ln2_b):
    B, T, D = x.shape
    depth = w_in.shape[0]
    alpha = (2 * depth) ** 0.25
    x2 = x.reshape(B * T, D)
    mem2 = mem.reshape(-1, D)
    for l in range(depth):
        x2 = _layer(x2, mem2, B, T, w_in[l], w_gla_gate_up[l], b_gla_gate[l], gla_norm_g[l], w_mem_kv[l],
                    w_br_gla[l], w_br_dsa[l], w_br_mem[l], b_merge[l], w_o[l], ln1_g[l], ln1_b[l],
                    w_up[l], b_up[l], w_down[l], b_down[l], ln2_g[l], ln2_b[l], alpha)
    return x2.reshape(B, T, D)
```
